```python
import jax, jax.numpy as jnp
from jax import lax
import numpy as np

D_MODEL = 2048
BATCH = 16
SEQ = 2048
DEPTH = 2

GRID_W = 64
CTX_LEN = 256
EPS = 1e-6

NA_HEADS = 8
NA_HEAD_DIM = 128
WIN_R = 8
WIN_C = 16

MLA_HEADS = 8
MLA_Q_RANK = 512
MLA_KV_RANK = 512
MLA_NOPE_DIM = 128
MLA_ROPE_DIM = 64
MLA_V_DIM = 128
MLA_SCALE = (MLA_NOPE_DIM + MLA_ROPE_DIM) ** -0.5
Q_BLOCK = 128
ROPE_BASE = 10000.0

DN_QK_HEADS = 16
DN_V_HEADS = 32
DN_K_DIM = 128
DN_V_DIM = 128
CONV_K = 5
CHUNK = 64

NA_WIDTH = NA_HEADS * NA_HEAD_DIM
MLA_WIDTH = MLA_HEADS * MLA_V_DIM
AB_MIX_WIDTH = NA_WIDTH + MLA_WIDTH
AB_SPLITS = (NA_WIDTH, NA_WIDTH, NA_WIDTH, MLA_Q_RANK, MLA_KV_RANK, MLA_ROPE_DIM, AB_MIX_WIDTH)
AB_IN_WIDTH = sum(AB_SPLITS)
DN_QK_WIDTH = DN_QK_HEADS * DN_K_DIM
DN_MIX_WIDTH = DN_V_HEADS * DN_V_DIM
DN_CONV_WIDTH = 2 * DN_QK_WIDTH + DN_MIX_WIDTH
DN_SPLITS = (DN_CONV_WIDTH, DN_MIX_WIDTH, 2 * DN_V_HEADS, 2 * DN_V_HEADS)
DN_IN_WIDTH = sum(DN_SPLITS)

F32 = jnp.float32

kernel_name = 'hybrid_na_mla_gdn_prefix_trunk'


def split_cols(t, sizes):
    return jnp.split(t, np.cumsum(sizes)[:-1].tolist(), axis=-1)


def rmsnorm(x, gain):
    xf = x.astype(F32)
    y = xf * lax.rsqrt(jnp.mean(xf * xf, axis=-1, keepdims=True) + EPS)
    return (y * gain.astype(F32)).astype(x.dtype)


def l2norm(x):
    xf = x.astype(F32)
    return (xf * lax.rsqrt(jnp.sum(xf * xf, axis=-1, keepdims=True) + EPS)).astype(x.dtype)


def adaln_params(cond, w_mod, b_mod):
    return jnp.split(jax.nn.silu(cond) @ w_mod + b_mod, 3, axis=-1)


def modulate(x, gain, shift, scale):
    return rmsnorm(x, gain) * (1 + scale) + shift


def to_heads(t, n_heads):
    b, t_len, _ = t.shape
    return t.reshape(b, t_len, n_heads, -1).transpose(0, 2, 1, 3)


def merge_heads(o):
    b, h, t_len, d = o.shape
    return o.transpose(0, 2, 1, 3).reshape(b, t_len, h * d)


def rope_2d(x):
    t_len, r_dim = x.shape[-2], x.shape[-1]
    half = r_dim // 2
    quarter = half // 2
    pos = jnp.arange(t_len)
    inv = ROPE_BASE ** (-jnp.arange(quarter, dtype=F32) / quarter)

    def rot(xa, p):
        ang = p.astype(F32)[:, None] * inv[None, :]
        cos, sin = jnp.cos(ang), jnp.sin(ang)
        x1, x2 = xa[..., :quarter], xa[..., quarter:]
        return jnp.concatenate([x1 * cos - x2 * sin, x1 * sin + x2 * cos], axis=-1)

    xf = x.astype(F32)
    out = jnp.concatenate([rot(xf[..., :half], pos // GRID_W), rot(xf[..., half:], pos % GRID_W)], axis=-1)
    return out.astype(x.dtype)


def joint_softmax(s_a, s_b, dtype):
    m = jnp.maximum(s_a.max(-1, keepdims=True), s_b.max(-1, keepdims=True))
    e_a = jnp.exp(s_a - m)
    e_b = jnp.exp(s_b - m)
    inv = 1.0 / (e_a.sum(-1, keepdims=True) + e_b.sum(-1, keepdims=True))
    return (e_a * inv).astype(dtype), (e_b * inv).astype(dtype)


def softmax_attend(q, k, v, scale):
    s = jnp.einsum('bhqd,bhkd->bhqk', q, k).astype(F32) * scale
    p = jax.nn.softmax(s, axis=-1).astype(v.dtype)
    return jnp.einsum('bhqk,bhkd->bhqd', p, v)


def neighbourhood_index(rows):
    kr = min(WIN_R, rows)
    r = jnp.arange(rows)
    col = jnp.arange(GRID_W)
    r0 = jnp.clip(r - kr // 2, 0, rows - kr)
    c0 = jnp.clip(col - WIN_C // 2, 0, GRID_W - WIN_C)
    key_rows = r0[:, None] + jnp.arange(kr)
    key_cols = c0[:, None] + jnp.arange(WIN_C)
    idx = key_rows[:, None, :, None] * GRID_W + key_cols[None, :, None, :]
    idx = idx.reshape(rows, GRID_W, kr * WIN_C)
    d_row = key_rows - r[:, None] + (WIN_R - 1)
    d_col = key_cols - col[:, None] + (WIN_C - 1)
    return idx, d_row, d_col


def neighbourhood_attention(q, k, v, k_ctx, v_ctx, rpb):
    b, h, s_len, d = q.shape
    rows = s_len // GRID_W
    idx, d_row, d_col = neighbourhood_index(rows)
    scale = d ** -0.5
    q_rows = q.reshape(b, h, rows, GRID_W, d).transpose(2, 0, 1, 3, 4)

    def row_block(args):
        q_i, idx_i, dr_i = args
        k_g = jnp.take(k, idx_i, axis=2)
        v_g = jnp.take(v, idx_i, axis=2)
        bias = rpb[:, dr_i[None, :, None], d_col[:, None, :]].reshape(h, GRID_W, -1)
        s_nb = jnp.einsum('bhqd,bhqkd->bhqk', q_i, k_g).astype(F32) * scale + bias.astype(F32)
        s_cx = jnp.einsum('bhqd,bhkd->bhqk', q_i, k_ctx).astype(F32) * scale
        p_nb, p_cx = joint_softmax(s_nb, s_cx, v.dtype)
        return jnp.einsum('bhqk,bhqkd->bhqd', p_nb, v_g) + jnp.einsum('bhqk,bhkd->bhqd', p_cx, v_ctx)

    o = lax.map(row_block, (q_rows, idx, d_row))
    return o.transpose(1, 2, 0, 3, 4).reshape(b, h, s_len, d)


def mla_latent_attention(qn, qr, kn, kr, v, kn_c, kr_c, v_c):
    b, h, s_len, _ = qn.shape
    nb = s_len // Q_BLOCK
    blocks = lambda t: t.reshape(b, h, nb, Q_BLOCK, t.shape[-1]).transpose(2, 0, 1, 3, 4)

    def attend(args):
        qn_i, qr_i = args
        s_lat = (jnp.einsum('bhqd,bhkd->bhqk', qn_i, kn) + jnp.einsum('bhqd,bkd->bhqk', qr_i, kr)).astype(F32) * MLA_SCALE
        s_ctx = (jnp.einsum('bhqd,bhkd->bhqk', qn_i, kn_c) + jnp.einsum('bhqd,bkd->bhqk', qr_i, kr_c)).astype(F32) * MLA_SCALE
        p_lat, p_ctx = joint_softmax(s_lat, s_ctx, v.dtype)
        return jnp.einsum('bhqk,bhkd->bhqd', p_lat, v) + jnp.einsum('bhqk,bhkd->bhqd', p_ctx, v_c)

    o = lax.map(attend, (blocks(qn), blocks(qr)))
    return o.transpose(1, 2, 0, 3, 4).reshape(b, h, s_len, -1)


def mla_context_attention(qn, qr, kn, kr, v):
    s = (jnp.einsum('bhqd,bhkd->bhqk', qn, kn) + jnp.einsum('bhqd,bkd->bhqk', qr, kr)).astype(F32) * MLA_SCALE
    p = jax.nn.softmax(s, axis=-1).astype(v.dtype)
    return jnp.einsum('bhqk,bhkd->bhqd', p, v)


def na_mla_layer(x, ctx, c, c_ctx, w_mod, b_mod, norm, w_in, rpb, q_norm, w_qb, kv_norm, w_kvb, w_out, update_ctx):
    sh, sc, gt = adaln_params(c, w_mod, b_mod)
    sh_c, sc_c, gt_c = adaln_params(c_ctx, w_mod, b_mod)
    hx = modulate(x, norm, sh[:, None], sc[:, None]) @ w_in
    hc = modulate(ctx, norm, sh_c, sc_c) @ w_in
    qa, ka, va, cq, ckv, kr, z = split_cols(hx, AB_SPLITS)
    qa_c, ka_c, va_c, cq_c, ckv_c, kr_c, z_c = split_cols(hc, AB_SPLITS)

    def mla_q(cq_t):
        q = to_heads(rmsnorm(cq_t, q_norm) @ w_qb, MLA_HEADS)
        return q[..., :MLA_NOPE_DIM], q[..., MLA_NOPE_DIM:]

    def mla_kv(ckv_t):
        kv = to_heads(rmsnorm(ckv_t, kv_norm) @ w_kvb, MLA_HEADS)
        return kv[..., :MLA_NOPE_DIM], kv[..., MLA_NOPE_DIM:]

    ka_ch, va_ch = to_heads(ka_c, NA_HEADS), to_heads(va_c, NA_HEADS)
    o_na = neighbourhood_attention(to_heads(qa, NA_HEADS), to_heads(ka, NA_HEADS), to_heads(va, NA_HEADS), ka_ch, va_ch, rpb)

    qn, qr = mla_q(cq)
    kn, v = mla_kv(ckv)
    kn_c, v_c = mla_kv(ckv_c)
    o_mla = mla_latent_attention(qn, rope_2d(qr), kn, v=v, kr=rope_2d(kr), kn_c=kn_c, kr_c=kr_c, v_c=v_c)

    y = (jnp.concatenate([merge_heads(o_na), merge_heads(o_mla)], axis=-1) * jax.nn.silu(z)) @ w_out
    x = x + gt[:, None] * y
    if update_ctx:
        o_na_c = softmax_attend(to_heads(qa_c, NA_HEADS), ka_ch, va_ch, NA_HEAD_DIM ** -0.5)
        qn_c, qr_c = mla_q(cq_c)
        o_mla_c = mla_context_attention(qn_c, qr_c, kn_c, kr_c, v_c)
        y_c = (jnp.concatenate([merge_heads(o_na_c), merge_heads(o_mla_c)], axis=-1) * jax.nn.silu(z_c)) @ w_out
        ctx = ctx + gt_c * y_c
    return x, ctx


def depthwise_conv(x, w):
    ch = x.shape[-1]
    return lax.conv_general_dilated(x, w[:, None, :], window_strides=(1,), padding=[(CONV_K // 2, CONV_K // 2)],
                                    dimension_numbers=('NWC', 'WIO', 'NWC'), feature_group_count=ch)


def dn_qk_heads(t, scale):
    b, t_len, _ = t.shape
    t = l2norm(t.reshape(b, t_len, DN_QK_HEADS, DN_K_DIM)) * scale
    return jnp.repeat(t, DN_V_HEADS // DN_QK_HEADS, axis=2).transpose(0, 2, 1, 3)


def deltanet_inputs(h, conv_w, a_log, dt_bias):
    b, t_len, _ = h.shape
    qkv, z, b_raw, a_raw = split_cols(h, DN_SPLITS)
    qkv = jax.nn.silu(depthwise_conv(qkv, conv_w))
    q, k, v = split_cols(qkv, (DN_QK_WIDTH, DN_QK_WIDTH, DN_MIX_WIDTH))
    q = dn_qk_heads(q, DN_K_DIM ** -0.5)
    k = dn_qk_heads(k, 1.0)
    v = to_heads(v, DN_V_HEADS)
    dirs = lambda t: t.astype(F32).reshape(b, t_len, 2, DN_V_HEADS).transpose(2, 0, 3, 1)
    beta = jax.nn.sigmoid(dirs(b_raw))
    g = -jnp.exp(a_log.astype(F32))[:, None, :, None] * jax.nn.softplus(dirs(a_raw) + dt_bias.astype(F32)[:, None, :, None])
    return q, k, v, z, g, beta


def gated_delta_chunked(q, k, v, g, beta, state0):
    b, h, t_len, _ = q.shape
    dv = v.shape[-1]
    n = t_len // CHUNK
    chunks = lambda t: t.astype(F32).reshape(b, h, n, CHUNK, *t.shape[3:])
    qf, kf, vf, gf, bf = map(chunks, (q, k, v, g, beta))
    gf = jnp.cumsum(gf, axis=-1)
    lower = jnp.tril(jnp.ones((CHUNK, CHUNK), bool))
    strict = jnp.tril(jnp.ones((CHUNK, CHUNK), bool), -1)
    diff = gf[..., :, None] - gf[..., None, :]
    decay = jnp.where(lower, jnp.exp(jnp.where(lower, diff, 0.0)), 0.0)
    kb = kf * bf[..., None]
    lmat = jnp.where(strict, jnp.einsum('bhncd,bhnsd->bhncs', kb, kf) * decay, 0.0)
    rhs = jnp.concatenate([vf * bf[..., None], kb * jnp.exp(gf)[..., None]], axis=-1)
    sol = lax.linalg.triangular_solve(lmat + jnp.eye(CHUNK, dtype=F32), rhs, left_side=True, lower=True)
    u, w = sol[..., :dv], sol[..., dv:]
    intra = jnp.where(lower, jnp.einsum('bhncd,bhnsd->bhncs', qf, kf) * decay, 0.0)

    def step(state, xs):
        q_i, k_i, u_i, w_i, g_i, a_i = xs
        v_new = u_i - jnp.einsum('bhcd,bhde->bhce', w_i, state)
        o_i = jnp.einsum('bhcd,bhde->bhce', q_i * jnp.exp(g_i)[..., None], state) + jnp.einsum('bhcs,bhse->bhce', a_i, v_new)
        g_last = g_i[..., -1:]
        state = state * jnp.exp(g_last)[..., None] + jnp.einsum('bhcd,bhce->bhde', k_i * jnp.exp(g_last - g_i)[..., None], v_new)
        return state, o_i

    xs = tuple(jnp.moveaxis(t, 2, 0) for t in (qf, kf, u, w, gf, intra))
    state, o = lax.scan(step, state0, xs)
    o = jnp.moveaxis(o, 0, 2).reshape(b, h, t_len, dv)
    return o.astype(v.dtype), state


def flip_t(t):
    return jnp.flip(t, axis=2)


def deltanet_out(o, z, o_norm, w_out):
    b, t_len = z.shape[:2]
    o = rmsnorm(o.transpose(0, 2, 1, 3), o_norm)
    return (o.reshape(b, t_len, DN_MIX_WIDTH) * jax.nn.silu(z)) @ w_out


def deltanet_layer(x, ctx, c, c_ctx, w_mod, b_mod, norm, w_in, conv_w, a_log, dt_bias, o_norm, w_out, update_ctx):
    sh, sc, gt = adaln_params(c, w_mod, b_mod)
    sh_c, sc_c, gt_c = adaln_params(c_ctx, w_mod, b_mod)
    qx, kx, vx, zx, gx, bx = deltanet_inputs(modulate(x, norm, sh[:, None], sc[:, None]) @ w_in, conv_w, a_log, dt_bias)
    qc, kc, vc, zc, gc, bc = deltanet_inputs(modulate(ctx, norm, sh_c, sc_c) @ w_in, conv_w, a_log, dt_bias)
    zero = jnp.zeros((ctx.shape[0], DN_V_HEADS, DN_K_DIM, DN_V_DIM), F32)
    o_cf, s_cf = gated_delta_chunked(qc, kc, vc, gc[0], bc[0], zero)
    o_xf, _ = gated_delta_chunked(qx, kx, vx, gx[0], bx[0], s_cf)
    o_cb, s_cb = gated_delta_chunked(flip_t(qc), flip_t(kc), flip_t(vc), flip_t(gc[1]), flip_t(bc[1]), zero)
    o_xb, _ = gated_delta_chunked(flip_t(qx), flip_t(kx), flip_t(vx), flip_t(gx[1]), flip_t(bx[1]), s_cb)
    x = x + gt[:, None] * deltanet_out(o_xf + flip_t(o_xb), zx, o_norm, w_out)
    if update_ctx:
        ctx = ctx + gt_c * deltanet_out(o_cf + flip_t(o_cb), zc, o_norm, w_out)
    return x, ctx


def setup_inputs(seed: int = 0) -> dict:
    key = jax.random.key(seed)
    ks = iter(jax.random.split(key, 32))
    nrm = lambda shape, scale: jax.random.normal(next(ks), shape, F32) * scale
    ne, no = (DEPTH + 1) // 2, DEPTH // 2
    d = D_MODEL
    dt = jnp.exp(jax.random.uniform(next(ks), (no, 2, DN_V_HEADS), F32, minval=float(np.log(1e-3)), maxval=float(np.log(1e-1))))
    return {
        'x': nrm((BATCH, SEQ, d), 1.0),
        'c': nrm((BATCH, d), 1.0),
        'ctx': nrm((BATCH, CTX_LEN, d), 1.0),
        'c_ctx': nrm((d,), 1.0),
        'ab_w_mod': nrm((ne, d, 3 * d), d ** -0.5),
        'ab_b_mod': nrm((ne, 3 * d), 0.02),
        'ab_norm': 1.0 + nrm((ne, d), 0.1),
        'ab_w_in': nrm((ne, d, AB_IN_WIDTH), d ** -0.5),
        'ab_rpb': nrm((ne, NA_HEADS, 2 * WIN_R - 1, 2 * WIN_C - 1), 0.5),
        'ab_q_norm': 1.0 + nrm((ne, MLA_Q_RANK), 0.1),
        'ab_w_qb': nrm((ne, MLA_Q_RANK, MLA_HEADS * (MLA_NOPE_DIM + MLA_ROPE_DIM)), MLA_Q_RANK ** -0.5),
        'ab_kv_norm': 1.0 + nrm((ne, MLA_KV_RANK), 0.1),
        'ab_w_kvb': nrm((ne, MLA_KV_RANK, MLA_HEADS * (MLA_NOPE_DIM + MLA_V_DIM)), MLA_KV_RANK ** -0.5),
        'ab_w_out': nrm((ne, AB_MIX_WIDTH, d), AB_MIX_WIDTH ** -0.5),
        'dn_w_mod': nrm((no, d, 3 * d), d ** -0.5),
        'dn_b_mod': nrm((no, 3 * d), 0.02),
        'dn_norm': 1.0 + nrm((no, d), 0.1),
        'dn_w_in': nrm((no, d, DN_IN_WIDTH), d ** -0.5),
        'dn_conv': nrm((no, CONV_K, DN_CONV_WIDTH), CONV_K ** -0.5),
        'dn_a_log': jnp.log(jax.random.uniform(next(ks), (no, 2, DN_V_HEADS), F32, minval=1.0, maxval=16.0)),
        'dn_dt_bias': dt + jnp.log(-jnp.expm1(-dt)),
        'dn_o_norm': 1.0 + nrm((no, DN_V_DIM), 0.1),
        'dn_w_out': nrm((no, DN_MIX_WIDTH, d), DN_MIX_WIDTH ** -0.5),
        'final_norm': 1.0 + nrm((d,), 0.1),
    }


def reference(x, c, ctx, c_ctx, ab_w_mod, ab_b_mod, ab_norm, ab_w_in, ab_rpb, ab_q_norm, ab_w_qb, ab_kv_norm, ab_w_kvb,
              ab_w_out, dn_w_mod, dn_b_mod, dn_norm, dn_w_in, dn_conv, dn_a_log, dn_dt_bias, dn_o_norm, dn_w_out, final_norm):
    for i in range(DEPTH):
        j = i // 2
        update_ctx = i < DEPTH - 1
        if i % 2 == 0:
            x, ctx = na_mla_layer(x, ctx, c, c_ctx, ab_w_mod[j], ab_b_mod[j], ab_norm[j], ab_w_in[j], ab_rpb[j],
                                  ab_q_norm[j], ab_w_qb[j], ab_kv_norm[j], ab_w_kvb[j], ab_w_out[j], update_ctx)
        else:
            x, ctx = deltanet_layer(x, ctx, c, c_ctx, dn_w_mod[j], dn_b_mod[j], dn_norm[j], dn_w_in[j], dn_conv[j],
                                    dn_a_log[j], dn_dt_bias[j], dn_o_norm[j], dn_w_out[j], update_ctx)
    return rmsnorm(x, final_norm)
```

```python
import functools

import numpy as np
import jax
import jax.numpy as jnp
from jax import lax
from jax.experimental import pallas as pl
from jax.experimental.pallas import tpu as pltpu

F32 = jnp.float32
BF16 = jnp.bfloat16

EPS = 1e-6
GRID_W = 64
WIN_R = 8
WIN_C = 16
NA_HEADS = 8
HEAD_DIM = 128
MLA_HEADS = 8
MLA_RANK = 512
MLA_ROPE_DIM = 64
MLA_SCALE = (HEAD_DIM + MLA_ROPE_DIM) ** -0.5
ROPE_BASE = 10000.0
DN_QK_HEADS = 16
DN_V_HEADS = 32
CONV_K = 5
CHUNK = 64
NEG_BIG = -1e30

LANES = 128
MLA_QK_PAD = 256
NA_Q_ROWS = 8
NA_KEY_ROWS = 16
VMEM_LIMIT = 56 * 1024 * 1024


def _cparams(*sem):
    return pltpu.CompilerParams(dimension_semantics=sem, vmem_limit_bytes=VMEM_LIMIT)


def _adaln_body(c_ref, w_ref, b_ref, o_ref):
    cf = c_ref[...]
    a = (cf * jax.nn.sigmoid(cf)).astype(BF16)
    o_ref[...] = jnp.dot(a, w_ref[...].astype(BF16), preferred_element_type=F32) + b_ref[...]


def _adaln(cond, w_mod, b_mod):
    r, d = cond.shape
    n = w_mod.shape[1]
    tn = 512
    return pl.pallas_call(
        _adaln_body,
        grid=(n // tn,),
        in_specs=[pl.BlockSpec((r, d), lambda j: (0, 0)),
                  pl.BlockSpec((d, tn), lambda j: (0, j)),
                  pl.BlockSpec((1, tn), lambda j: (0, j))],
        out_specs=pl.BlockSpec((r, tn), lambda j: (0, j)),
        out_shape=jax.ShapeDtypeStruct((r, n), F32),
        compiler_params=_cparams("parallel"),
        name="adaln",
    )(cond, w_mod, b_mod.reshape(1, n))


def _proj_body(*refs, modulate, has_small, rope_main, rope_small):
    it = iter(refs)
    x_ref, gain_ref = next(it), next(it)
    shift_ref = scale_ref = ws_ref = c1_ref = c2_ref = s1_ref = s2_ref = os_ref = None
    if modulate:
        shift_ref, scale_ref = next(it), next(it)
    w_ref = next(it)
    if has_small:
        ws_ref = next(it)
    if rope_main:
        c1_ref, c2_ref = next(it), next(it)
    if rope_small:
        s1_ref, s2_ref = next(it), next(it)
    o_ref = next(it)
    if has_small:
        os_ref = next(it)
    xm_ref = next(it)

    @pl.when(pl.program_id(2) == 0)
    def _():
        xf = x_ref[0].astype(F32)
        ms = jnp.mean(xf * xf, axis=-1, keepdims=True)
        y = xf * lax.rsqrt(ms + EPS) * gain_ref[...]
        if modulate:
            y = y * (1.0 + scale_ref[0]) + shift_ref[0]
        xm = y.astype(BF16)
        xm_ref[...] = xm
        if has_small:
            s = jnp.dot(xm, ws_ref[...], preferred_element_type=F32)
            if rope_small:
                s = s * s1_ref[...] + pltpu.roll(s, LANES // 2, axis=1) * s2_ref[...]
            os_ref[0] = s.astype(os_ref.dtype)

    acc = jnp.dot(xm_ref[...], w_ref[...], preferred_element_type=F32)
    if rope_main:
        for h in range(acc.shape[1] // MLA_QK_PAD):
            a = acc[:, h * MLA_QK_PAD:(h + 1) * MLA_QK_PAD]
            a = a * c1_ref[...] + pltpu.roll(a, MLA_QK_PAD - MLA_ROPE_DIM, axis=1) * c2_ref[...]
            o_ref[0, :, h * MLA_QK_PAD:(h + 1) * MLA_QK_PAD] = a.astype(o_ref.dtype)
    else:
        o_ref[0] = acc.astype(o_ref.dtype)


def _norm_proj(x, kblk, k, gain, w, *, tm, tn, shift=None, scale=None, w_small=None,
               rope_main=None, rope_small=None, small_dtype=F32, name="proj"):
    b, t, _ = x.shape
    n = w.shape[1]
    modulate = shift is not None
    has_small = w_small is not None
    tm = min(tm, t)
    args = [x, gain.reshape(1, k).astype(F32)]
    in_specs = [pl.BlockSpec((1, tm, k), lambda bi, i, j: (bi, i, kblk)),
                pl.BlockSpec((1, k), lambda bi, i, j: (0, 0))]
    if modulate:
        args += [shift, scale]
        in_specs += [pl.BlockSpec((1, 1, k), lambda bi, i, j: (bi, 0, 0))] * 2
    args.append(w)
    in_specs.append(pl.BlockSpec((k, tn), lambda bi, i, j: (0, j)))
    if has_small:
        ns = w_small.shape[1]
        args.append(w_small)
        in_specs.append(pl.BlockSpec((k, ns), lambda bi, i, j: (0, 0)))
    if rope_main is not None:
        args += list(rope_main)
        in_specs += [pl.BlockSpec((tm, MLA_QK_PAD), lambda bi, i, j: (i, 0))] * 2
    if rope_small is not None:
        args += list(rope_small)
        in_specs += [pl.BlockSpec((tm, LANES), lambda bi, i, j: (i, 0))] * 2
    out_shape = [jax.ShapeDtypeStruct((b, t, n), BF16)]
    out_specs = [pl.BlockSpec((1, tm, tn), lambda bi, i, j: (bi, i, j))]
    if has_small:
        out_shape.append(jax.ShapeDtypeStruct((b, t, ns), small_dtype))
        out_specs.append(pl.BlockSpec((1, tm, ns), lambda bi, i, j: (bi, i, 0)))
    body = functools.partial(_proj_body, modulate=modulate, has_small=has_small,
                             rope_main=rope_main is not None, rope_small=rope_small is not None)
    out = pl.pallas_call(
        body,
        grid=(b, t // tm, n // tn),
        in_specs=in_specs,
        out_specs=out_specs,
        out_shape=out_shape,
        scratch_shapes=[pltpu.VMEM((tm, k), BF16)],
        compiler_params=_cparams("parallel", "parallel", "arbitrary"),
        name=name,
    )(*args)
    return out if has_small else out[0]


def _attn_body(*refs, seg_lens, has_extra, scale):
    it = iter(refs)
    q_ref = next(it)
    segs = []
    for _ in seg_lens:
        km = next(it)
        ke = next(it) if has_extra else None
        segs.append((km, ke, next(it)))
    o_ref, k_scr, v_scr = next(it), next(it), next(it)

    @pl.when(pl.program_id(2) == 0)
    def _():
        off = 0
        for (km, ke, v), ln in zip(segs, seg_lens):
            k_scr[off:off + ln, 0:HEAD_DIM] = km[0]
            if has_extra:
                k_scr[off:off + ln, HEAD_DIM:2 * HEAD_DIM] = ke[0]
            v_scr[off:off + ln, :] = v[0]
            off += ln

    s = lax.dot_general(q_ref[0], k_scr[...], (((1,), (1,)), ((), ())), preferred_element_type=F32)
    if scale != 1.0:
        s = s * scale
    m = jnp.max(s, axis=-1, keepdims=True)
    e = jnp.exp(s - m)
    l = jnp.sum(e, axis=-1, keepdims=True)
    o = jnp.dot(e.astype(BF16), v_scr[...], preferred_element_type=F32)
    o_ref[0] = (o / l).astype(o_ref.dtype)


def _attention(q, q_blk0, dq, segs, heads, *, tq, scale, name):
    b, t, _ = q.shape
    tq = min(tq, t)
    has_extra = segs[0][2] is not None
    seg_lens = tuple(s[0].shape[1] for s in segs)
    tk = sum(seg_lens)
    args = [q]
    in_specs = [pl.BlockSpec((1, tq, dq), lambda bi, h, i: (bi, i, q_blk0 + h))]
    for km, kb0, ke, v, vb0 in segs:
        ln = km.shape[1]
        args.append(km)
        in_specs.append(pl.BlockSpec((1, ln, HEAD_DIM), lambda bi, h, i, kb0=kb0: (bi, 0, kb0 + h)))
        if has_extra:
            args.append(ke)
            in_specs.append(pl.BlockSpec((1, ln, HEAD_DIM), lambda bi, h, i: (bi, 0, 0)))
        args.append(v)
        in_specs.append(pl.BlockSpec((1, ln, HEAD_DIM), lambda bi, h, i, vb0=vb0: (bi, 0, vb0 + h)))
    body = functools.partial(_attn_body, seg_lens=seg_lens, has_extra=has_extra, scale=scale)
    return pl.pallas_call(
        body,
        grid=(b, heads, t // tq),
        in_specs=in_specs,
        out_specs=pl.BlockSpec((1, tq, HEAD_DIM), lambda bi, h, i: (bi, i, h)),
        out_shape=jax.ShapeDtypeStruct((b, t, heads * HEAD_DIM), BF16),
        scratch_shapes=[pltpu.VMEM((tk, dq), BF16), pltpu.VMEM((tk, HEAD_DIM), BF16)],
        compiler_params=_cparams("parallel", "parallel", "arbitrary"),
        name=name,
    )(*args)


def _na_geometry(rows):
    groups = rows // NA_Q_ROWS
    key_rows = min(NA_KEY_ROWS, rows)
    return groups, key_rows


def _na_bias_table(rpb, rows):
    heads = rpb.shape[0]
    groups, key_rows = _na_geometry(rows)
    g = np.arange(groups)
    slab0 = np.clip(NA_Q_ROWS * g - WIN_R // 2, 0, rows - key_rows)
    r = NA_Q_ROWS * g[:, None] + np.arange(NA_Q_ROWS)[None]
    k_abs = slab0[:, None] + np.arange(key_rows)[None]
    win0 = np.clip(r - WIN_R // 2, 0, rows - WIN_R)
    d_row = k_abs[:, None, :] - r[:, :, None] + (WIN_R - 1)
    ok_row = (k_abs[:, None, :] >= win0[:, :, None]) & (k_abs[:, None, :] < win0[:, :, None] + WIN_R)
    col = np.arange(GRID_W)
    c0 = np.clip(col - WIN_C // 2, 0, GRID_W - WIN_C)
    d_col = col[None, :] - col[:, None] + (WIN_C - 1)
    ok_col = (col[None, :] >= c0[:, None]) & (col[None, :] < c0[:, None] + WIN_C)
    t = rpb[:, np.clip(d_row, 0, 2 * WIN_R - 2)]
    t = t[..., np.clip(d_col, 0, 2 * WIN_C - 2)]
    ok = ok_row[:, :, :, None, None] & ok_col[None, None, None, :, :]
    t = jnp.where(jnp.asarray(ok)[None], t.astype(F32), NEG_BIG)
    t = t.transpose(0, 1, 2, 4, 3, 5)
    return t.reshape(heads, groups, NA_Q_ROWS * GRID_W, key_rows * GRID_W)


def _na_body(q_ref, k_ref, v_ref, kc_ref, vc_ref, bias_ref, o_ref, *, rows, scale):
    groups, key_rows = _na_geometry(rows)
    g = pl.program_id(2)
    slab0 = jnp.clip(NA_Q_ROWS * g - WIN_R // 2, 0, rows - key_rows)
    start = pl.multiple_of(slab0 * GRID_W, 4 * GRID_W)
    nk = key_rows * GRID_W
    q = q_ref[0]
    nt = (((1,), (1,)), ((), ()))
    s_nb = lax.dot_general(q, k_ref[0, pl.ds(start, nk), :], nt, preferred_element_type=F32) * scale + bias_ref[0, g]
    s_cx = lax.dot_general(q, kc_ref[0], nt, preferred_element_type=F32) * scale
    m = jnp.maximum(jnp.max(s_nb, axis=-1, keepdims=True), jnp.max(s_cx, axis=-1, keepdims=True))
    e_nb = jnp.exp(s_nb - m)
    e_cx = jnp.exp(s_cx - m)
    l = jnp.sum(e_nb, axis=-1, keepdims=True) + jnp.sum(e_cx, axis=-1, keepdims=True)
    o = jnp.dot(e_nb.astype(BF16), v_ref[0, pl.ds(start, nk), :], preferred_element_type=F32)
    o = o + jnp.dot(e_cx.astype(BF16), vc_ref[0], preferred_element_type=F32)
    o_ref[0] = (o / l).astype(o_ref.dtype)


def _na_attention(hx, hc, bias):
    b, s, _ = hx.shape
    l = hc.shape[1]
    rows = s // GRID_W
    groups, key_rows = _na_geometry(rows)
    tq = NA_Q_ROWS * GRID_W
    body = functools.partial(_na_body, rows=rows, scale=HEAD_DIM ** -0.5)
    return pl.pallas_call(
        body,
        grid=(NA_HEADS, b, groups),
        in_specs=[pl.BlockSpec((1, tq, HEAD_DIM), lambda h, bi, g: (bi, g, h)),
                  pl.BlockSpec((1, s, HEAD_DIM), lambda h, bi, g: (bi, 0, NA_HEADS + h)),
                  pl.BlockSpec((1, s, HEAD_DIM), lambda h, bi, g: (bi, 0, 2 * NA_HEADS + h)),
                  pl.BlockSpec((1, l, HEAD_DIM), lambda h, bi, g: (bi, 0, NA_HEADS + h)),
                  pl.BlockSpec((1, l, HEAD_DIM), lambda h, bi, g: (bi, 0, 2 * NA_HEADS + h)),
                  pl.BlockSpec((1, groups, tq, key_rows * GRID_W), lambda h, bi, g: (h, 0, 0, 0))],
        out_specs=pl.BlockSpec((1, tq, HEAD_DIM), lambda h, bi, g: (bi, g, h)),
        out_shape=jax.ShapeDtypeStruct((b, s, NA_HEADS * HEAD_DIM), BF16),
        compiler_params=_cparams("parallel", "parallel", "parallel"),
        name="na_attention",
    )(hx, hx, hx, hc, hc, bias)


def _silu(x):
    return x * jax.nn.sigmoid(x)


def _out_ab_body(x_ref, gt_ref, o1_ref, o2_ref, z1_ref, z2_ref, w_ref, y_ref):
    half = o1_ref.shape[2]
    a1 = (o1_ref[0].astype(F32) * _silu(z1_ref[0].astype(F32))).astype(BF16)
    a2 = (o2_ref[0].astype(F32) * _silu(z2_ref[0].astype(F32))).astype(BF16)
    acc = jnp.dot(a1, w_ref[0:half, :], preferred_element_type=F32)
    acc = acc + jnp.dot(a2, w_ref[half:2 * half, :], preferred_element_type=F32)
    y_ref[0] = x_ref[0] + gt_ref[0] * acc


def _out_proj_ab(x, gt, o1, o2, h, z_blk0, w, *, tm):
    b, t, d = x.shape
    half = o1.shape[2]
    tm = min(tm, t)
    return pl.pallas_call(
        _out_ab_body,
        grid=(b, t // tm),
        in_specs=[pl.BlockSpec((1, tm, d), lambda bi, i: (bi, i, 0)),
                  pl.BlockSpec((1, 1, d), lambda bi, i: (bi, 0, 0)),
                  pl.BlockSpec((1, tm, half), lambda bi, i: (bi, i, 0)),
                  pl.BlockSpec((1, tm, half), lambda bi, i: (bi, i, 0)),
                  pl.BlockSpec((1, tm, half), lambda bi, i: (bi, i, z_blk0)),
                  pl.BlockSpec((1, tm, half), lambda bi, i: (bi, i, z_blk0 + 1)),
                  pl.BlockSpec((2 * half, d), lambda bi, i: (0, 0))],
        out_specs=pl.BlockSpec((1, tm, d), lambda bi, i: (bi, i, 0)),
        out_shape=jax.ShapeDtypeStruct((b, t, d), F32),
        compiler_params=_cparams("parallel", "parallel"),
        name="out_proj_ab",
    )(x, gt, o1, o2, h, h, w)


def _out_dn_body(x_ref, gt_ref, o_ref, z_ref, on_ref, w_ref, fn_ref, y_ref, a_scr):
    for h in range(o_ref.shape[2] // HEAD_DIM):
        sl = slice(h * HEAD_DIM, (h + 1) * HEAD_DIM)
        oh = o_ref[0, :, sl].astype(F32)
        ms = jnp.mean(oh * oh, axis=-1, keepdims=True)
        a = oh * lax.rsqrt(ms + EPS) * on_ref[...] * _silu(z_ref[0, :, sl].astype(F32))
        a_scr[:, sl] = a.astype(BF16)
    acc = jnp.dot(a_scr[...], w_ref[...], preferred_element_type=F32)
    y = x_ref[0] + gt_ref[0] * acc
    ms = jnp.mean(y * y, axis=-1, keepdims=True)
    y_ref[0] = y * lax.rsqrt(ms + EPS) * fn_ref[...]


def _out_proj_dn(x, gt, o, h, z_blk, o_norm, w, final_norm, *, tm):
    b, t, d = x.shape
    km = o.shape[2]
    tm = min(tm, t)
    return pl.pallas_call(
        _out_dn_body,
        grid=(b, t // tm),
        in_specs=[pl.BlockSpec((1, tm, d), lambda bi, i: (bi, i, 0)),
                  pl.BlockSpec((1, 1, d), lambda bi, i: (bi, 0, 0)),
                  pl.BlockSpec((1, tm, km), lambda bi, i: (bi, i, 0)),
                  pl.BlockSpec((1, tm, km), lambda bi, i: (bi, i, z_blk)),
                  pl.BlockSpec((1, HEAD_DIM), lambda bi, i: (0, 0)),
                  pl.BlockSpec((km, d), lambda bi, i: (0, 0)),
                  pl.BlockSpec((1, d), lambda bi, i: (0, 0))],
        out_specs=pl.BlockSpec((1, tm, d), lambda bi, i: (bi, i, 0)),
        out_shape=jax.ShapeDtypeStruct((b, t, d), F32),
        scratch_shapes=[pltpu.VMEM((tm, km), BF16)],
        compiler_params=_cparams("parallel", "parallel"),
        name="out_proj_dn",
    )(x, gt, o, h, o_norm.reshape(1, HEAD_DIM).astype(F32), w, final_norm.reshape(1, d).astype(F32))


CONV_HALO = 16


def _conv_body(xp_ref, x_ref, xn_ref, w_ref, o_ref, pad_scr, *, l2_scale):
    i = pl.program_id(1)
    nt = pl.num_programs(1)
    tt = x_ref.shape[1]
    prev = xp_ref[0].astype(F32)
    nxt = xn_ref[0].astype(F32)
    pad_scr[0:CONV_HALO, :] = jnp.where(i > 0, prev, 0.0)
    pad_scr[CONV_HALO:CONV_HALO + tt, :] = x_ref[0].astype(F32)
    pad_scr[CONV_HALO + tt:2 * CONV_HALO + tt, :] = jnp.where(i < nt - 1, nxt, 0.0)
    acc = None
    for j in range(CONV_K):
        lo = CONV_HALO + j - CONV_K // 2
        term = pad_scr[lo:lo + tt, :] * w_ref[j:j + 1, :]
        acc = term if acc is None else acc + term
    y = _silu(acc)
    if l2_scale is None:
        o_ref[0] = y.astype(o_ref.dtype)
    else:
        for h in range(y.shape[1] // HEAD_DIM):
            yh = y[:, h * HEAD_DIM:(h + 1) * HEAD_DIM]
            ss = jnp.sum(yh * yh, axis=-1, keepdims=True)
            o_ref[0, :, h * HEAD_DIM:(h + 1) * HEAD_DIM] = (yh * lax.rsqrt(ss + EPS) * l2_scale).astype(o_ref.dtype)


def _dn_conv(h, conv_w, col0, width, l2_scale, *, tt, cb=512):
    b, t, _ = h.shape
    tt = min(tt, t)
    blk0 = col0 // cb
    hb = tt // CONV_HALO
    nhalo = t // CONV_HALO
    body = functools.partial(_conv_body, l2_scale=l2_scale)
    return pl.pallas_call(
        body,
        grid=(b, t // tt, width // cb),
        in_specs=[pl.BlockSpec((1, CONV_HALO, cb), lambda bi, i, c: (bi, jnp.maximum(i * hb - 1, 0), blk0 + c)),
                  pl.BlockSpec((1, tt, cb), lambda bi, i, c: (bi, i, blk0 + c)),
                  pl.BlockSpec((1, CONV_HALO, cb), lambda bi, i, c: (bi, jnp.minimum((i + 1) * hb, nhalo - 1), blk0 + c)),
                  pl.BlockSpec((CONV_K, cb), lambda bi, i, c: (0, blk0 + c))],
        out_specs=pl.BlockSpec((1, tt, cb), lambda bi, i, c: (bi, i, c)),
        out_shape=jax.ShapeDtypeStruct((b, t, width), BF16),
        scratch_shapes=[pltpu.VMEM((tt + 2 * CONV_HALO, cb), F32)],
        compiler_params=_cparams("parallel", "parallel", "parallel"),
        name="dn_conv",
    )(h, h, h, conv_w)


GATE_TILE = 256


def _split3(x):
    hi = x.astype(BF16)
    r1 = x - hi.astype(F32)
    mid = r1.astype(BF16)
    lo = (r1 - mid.astype(F32)).astype(BF16)
    return hi, mid, lo


def _gates_body(h_ref, alog_ref, dtb_ref, o_ref):
    t = h_ref[0]
    lane = lax.broadcasted_iota(jnp.int32, t.shape, 1) % 4
    beta = jax.nn.sigmoid(t)
    z = t + dtb_ref[...]
    softplus = jnp.maximum(z, 0.0) + jnp.log1p(jnp.exp(-jnp.abs(z)))
    g = -jnp.exp(alog_ref[...]) * softplus
    n = t.shape[0]
    r = lax.broadcasted_iota(jnp.int32, (n, n), 0)
    c = lax.broadcasted_iota(jnp.int32, (n, n), 1)
    same = (r // CHUNK) == (c // CHUNK)
    tri_f = jnp.where(same & (r >= c), 1.0, 0.0).astype(BF16)
    tri_b = jnp.where(same & (r <= c), 1.0, 0.0).astype(BF16)
    pf = pb = None
    for part in _split3(g):
        df = jnp.dot(tri_f, part, preferred_element_type=F32)
        db = jnp.dot(tri_b, part, preferred_element_type=F32)
        pf = df if pf is None else pf + df
        pb = db if pb is None else pb + db
    o_ref[0] = jnp.where(lane < 2, beta, jnp.where(lane == 2, pf, pb))


def _dn_gates(h_small, alog, dtb):
    b, t, n = h_small.shape
    tt = min(GATE_TILE, t)
    return pl.pallas_call(
        _gates_body,
        grid=(b, t // tt),
        in_specs=[pl.BlockSpec((1, tt, n), lambda bi, i: (bi, i, 0)),
                  pl.BlockSpec((1, n), lambda bi, i: (0, 0)),
                  pl.BlockSpec((1, n), lambda bi, i: (0, 0))],
        out_specs=pl.BlockSpec((1, tt, n), lambda bi, i: (bi, i, 0)),
        out_shape=jax.ShapeDtypeStruct((b, t, n), F32),
        compiler_params=_cparams("parallel", "parallel"),
        name="dn_gates",
    )(h_small, alog, dtb)


def _mm_split(a, b):
    ah = a.astype(BF16)
    al = (a - ah.astype(F32)).astype(BF16)
    bh = b.astype(BF16)
    bl = (b - bh.astype(F32)).astype(BF16)
    out = jnp.dot(ah, bh, preferred_element_type=F32)
    out = out + jnp.dot(ah, bl, preferred_element_type=F32)
    return out + jnp.dot(al, bh, preferred_element_type=F32)


def _unit_tri_inverse(lm, eye, same16, same32):
    l16 = jnp.where(same16, lm, 0.0)
    tinv = eye - l16
    p = l16
    for _ in range(3):
        p = _mm_split(p, p)
        tinv = tinv + _mm_split(tinv, p)
    c32 = jnp.where(same32, lm, 0.0) - l16
    tinv = tinv - _mm_split(_mm_split(tinv, c32), tinv)
    c64 = jnp.where(same32, 0.0, lm)
    return tinv - _mm_split(_mm_split(tinv, c64), tinv)


def _chunk_body(qc_ref, kc_ref, vc_ref, gcc_ref, grc_ref, qx_ref, kx_ref, vx_ref, gcx_ref, grx_ref,
                o_ref, s_scr, o_scr):
    j = pl.program_id(1)
    lane_shift = (LANES - 8 * j) % LANES
    s_scr[...] = jnp.zeros_like(s_scr)
    o_scr[...] = jnp.zeros_like(o_scr)
    row = lax.broadcasted_iota(jnp.int32, (CHUNK, CHUNK), 0)
    col = lax.broadcasted_iota(jnp.int32, (CHUNK, CHUNK), 1)
    eye = jnp.where(row == col, 1.0, 0.0)
    same16 = (row // 16) == (col // 16)
    same32 = (row // 32) == (col // 32)
    incl = (row >= col, row <= col)
    strict = (row > col, row < col)
    nt = (((1,), (1,)), ((), ()))
    tn = (((0,), (0,)), ((), ()))

    def run(q_ref, k_ref, v_ref, gc_ref, gr_ref, write_out):
        n_chunks = q_ref.shape[1] // CHUNK

        def step(i, carry):
            for d in range(2):
                n = i if d == 0 else n_chunks - 1 - i
                r0 = pl.multiple_of(n * CHUNK, CHUNK)
                q = q_ref[0, pl.ds(r0, CHUNK), :]
                k = k_ref[0, pl.ds(r0, CHUNK), :]
                kk = lax.dot_general(k, k, nt, preferred_element_type=F32)
                qk = lax.dot_general(q, k, nt, preferred_element_type=F32)
                gcol = pltpu.roll(gc_ref[0, pl.ds(r0, CHUNK), :], lane_shift, axis=1)
                grow = gr_ref[0, 0, n]
                for vh in range(2):
                    ci = 2 * vh + d
                    beta_c = gcol[:, 4 * vh + d:4 * vh + d + 1]
                    g_c = gcol[:, 4 * vh + 2 + d:4 * vh + 3 + d]
                    beta_r = grow[4 * vh + d:4 * vh + d + 1, :]
                    g_r = grow[4 * vh + 2 + d:4 * vh + 3 + d, :]
                    g_last = g_r[:, CHUNK - 1:CHUNK] if d == 0 else g_r[:, 0:1]
                    v = v_ref[0, pl.ds(r0, CHUNK), vh * HEAD_DIM:(vh + 1) * HEAD_DIM]
                    diff = g_c - g_r
                    dec = jnp.where(incl[d], jnp.exp(jnp.where(incl[d], diff, 0.0)), 0.0)
                    lm = jnp.where(strict[d], kk * dec, 0.0) * beta_c
                    am = qk * dec
                    tinv = _unit_tri_inverse(lm, eye, same16, same32)
                    tb = tinv * beta_r
                    u = jnp.dot(tb.astype(BF16), v, preferred_element_type=F32)
                    w = jnp.dot((tb * jnp.exp(g_r)).astype(BF16), k, preferred_element_type=F32)
                    s_old = s_scr[ci]
                    sb = s_old.astype(BF16)
                    v_new = u - jnp.dot(w.astype(BF16), sb, preferred_element_type=F32)
                    if write_out:
                        o = jnp.dot(q, sb, preferred_element_type=F32) * jnp.exp(g_c)
                        o = o + jnp.dot(am.astype(BF16), v_new.astype(BF16), preferred_element_type=F32)
                        o_scr[pl.ds(r0, CHUNK), vh * HEAD_DIM:(vh + 1) * HEAD_DIM] += o
                    vs = (v_new * jnp.exp(g_last - g_c)).astype(BF16)
                    s_scr[ci] = s_old * jnp.exp(g_last) + lax.dot_general(k, vs, tn, preferred_element_type=F32)
            return carry

        lax.fori_loop(0, n_chunks, step, 0)

    run(qc_ref, kc_ref, vc_ref, gcc_ref, grc_ref, False)
    run(qx_ref, kx_ref, vx_ref, gcx_ref, grx_ref, True)
    o_ref[0] = o_scr[...].astype(o_ref.dtype)


def _gate_rows(gcol):
    b, t, n = gcol.shape
    g = gcol.reshape(b, t // CHUNK, CHUNK, n // 8, 8)
    return g.transpose(0, 3, 1, 4, 2)


def _dn_chunk(qc, kc, vc, gcc, qx, kx, vx, gcx):
    b, s, _ = qx.shape
    l = qc.shape[1]
    grc, grx = _gate_rows(gcc), _gate_rows(gcx)
    pair = 2 * HEAD_DIM

    def seq_specs(t):
        return [pl.BlockSpec((1, t, HEAD_DIM), lambda bi, j: (bi, 0, j)),
                pl.BlockSpec((1, t, HEAD_DIM), lambda bi, j: (bi, 0, j)),
                pl.BlockSpec((1, t, pair), lambda bi, j: (bi, 0, j)),
                pl.BlockSpec((1, t, LANES), lambda bi, j: (bi, 0, 0)),
                pl.BlockSpec((1, 1, t // CHUNK, 8, CHUNK), lambda bi, j: (bi, j, 0, 0, 0))]

    return pl.pallas_call(
        _chunk_body,
        grid=(b, DN_QK_HEADS),
        in_specs=seq_specs(l) + seq_specs(s),
        out_specs=pl.BlockSpec((1, s, pair), lambda bi, j: (bi, 0, j)),
        out_shape=jax.ShapeDtypeStruct((b, s, DN_V_HEADS * HEAD_DIM), BF16),
        scratch_shapes=[pltpu.VMEM((4, HEAD_DIM, HEAD_DIM), F32), pltpu.VMEM((s, pair), F32)],
        compiler_params=_cparams("parallel", "parallel"),
        name="dn_chunk",
    )(qc, kc, vc, gcc, grc, qx, kx, vx, gcx, grx)


def _rot_cols(w):
    q = MLA_ROPE_DIM // 4
    return jnp.concatenate([-w[..., q:2 * q], w[..., 0:q], -w[..., 3 * q:4 * q], w[..., 2 * q:3 * q]], axis=-1)


def _rope_cos_sin(t_len):
    q = MLA_ROPE_DIM // 4
    pos = np.arange(t_len)
    inv = ROPE_BASE ** (-np.arange(q, dtype=np.float32) / q)
    ang_r = (pos // GRID_W).astype(np.float32)[:, None] * inv[None, :]
    ang_c = (pos % GRID_W).astype(np.float32)[:, None] * inv[None, :]
    ang = jnp.asarray(np.concatenate([ang_r, ang_r, ang_c, ang_c], axis=1))
    return jnp.cos(ang), jnp.sin(ang)


def _q_tables(cos, sin):
    t = cos.shape[0]
    one, zero = jnp.ones((t, HEAD_DIM), F32), jnp.zeros((t, MLA_ROPE_DIM), F32)
    c1 = jnp.concatenate([one, cos, zero], axis=1) * MLA_SCALE
    c2 = jnp.concatenate([jnp.zeros((t, HEAD_DIM), F32), sin, zero], axis=1) * MLA_SCALE
    return c1, c2


def _k_tables(cos, sin):
    zero = jnp.zeros_like(cos)
    return jnp.concatenate([cos, zero], axis=1), jnp.concatenate([sin, zero], axis=1)


def _mod_params(c, c_ctx, w_mod, b_mod):
    b, d = c.shape
    rows = ((b + 1 + 7) // 8) * 8
    cond = jnp.concatenate([c, c_ctx[None], jnp.zeros((rows - b - 1, d), F32)], axis=0)
    mod = _adaln(cond, w_mod, b_mod)
    sh, sc, gt = (mod[:b, i * d:(i + 1) * d].reshape(b, 1, d) for i in range(3))
    sh_c, sc_c, gt_c = (jnp.broadcast_to(mod[b, i * d:(i + 1) * d].reshape(1, 1, d), (b, 1, d)) for i in range(3))
    return (sh, sc, gt), (sh_c, sc_c, gt_c)


def _na_mla_layer(x, ctx, c, c_ctx, w_mod, b_mod, norm, w_in, rpb, q_norm, w_qb, kv_norm, w_kvb, w_out):
    b, s, d = x.shape
    l = ctx.shape[1]
    (sh, sc, gt), (sh_c, sc_c, gt_c) = _mod_params(c, c_ctx, w_mod, b_mod)
    na_w = NA_HEADS * HEAD_DIM
    n_main = 3 * na_w + 2 * MLA_RANK
    kr_w = w_in[:, n_main:n_main + MLA_ROPE_DIM]
    w_main = jnp.concatenate([w_in[:, :n_main], w_in[:, n_main + MLA_ROPE_DIM:]], axis=1).astype(BF16)
    w_small = jnp.concatenate([kr_w, _rot_cols(kr_w)], axis=1).astype(BF16)
    cos, sin = _rope_cos_sin(s)
    ident = (jnp.ones((l, MLA_ROPE_DIM), F32), jnp.zeros((l, MLA_ROPE_DIM), F32))
    hx, krx = _norm_proj(x, 0, d, norm, w_main, tm=512, tn=1024, shift=sh, scale=sc, w_small=w_small,
                         rope_small=_k_tables(cos, sin), small_dtype=BF16, name="in_proj_ab_x")
    hc, krc = _norm_proj(ctx, 0, d, norm, w_main, tm=512, tn=1024, shift=sh_c, scale=sc_c, w_small=w_small,
                         rope_small=_k_tables(*ident), small_dtype=BF16, name="in_proj_ab_ctx")

    wq = w_qb.reshape(MLA_RANK, MLA_HEADS, HEAD_DIM + MLA_ROPE_DIM)
    wq_r = wq[..., HEAD_DIM:]
    wq_p = jnp.concatenate([wq[..., :HEAD_DIM], wq_r, _rot_cols(wq_r)], axis=-1)
    wq_p = wq_p.reshape(MLA_RANK, MLA_HEADS * MLA_QK_PAD).astype(BF16)
    wkv = w_kvb.reshape(MLA_RANK, MLA_HEADS, 2 * HEAD_DIM)
    wkv_p = jnp.concatenate([wkv[..., :HEAD_DIM].reshape(MLA_RANK, -1), wkv[..., HEAD_DIM:].reshape(MLA_RANK, -1)],
                            axis=1).astype(BF16)
    cq_blk = 3 * na_w // MLA_RANK
    nq = MLA_HEADS * MLA_QK_PAD
    qx = _norm_proj(hx, cq_blk, MLA_RANK, q_norm, wq_p, tm=1024, tn=nq, rope_main=_q_tables(cos, sin), name="mla_q_x")
    qc = _norm_proj(hc, cq_blk, MLA_RANK, q_norm, wq_p, tm=1024, tn=nq, rope_main=_q_tables(*ident), name="mla_q_ctx")
    kvx = _norm_proj(hx, cq_blk + 1, MLA_RANK, kv_norm, wkv_p, tm=1024, tn=nq, name="mla_kv_x")
    kvc = _norm_proj(hc, cq_blk + 1, MLA_RANK, kv_norm, wkv_p, tm=1024, tn=nq, name="mla_kv_ctx")

    o_na = _na_attention(hx, hc, _na_bias_table(rpb, s // GRID_W))
    seg_x = (kvx, 0, krx, kvx, MLA_HEADS)
    seg_c = (kvc, 0, krc, kvc, MLA_HEADS)
    o_mla = _attention(qx, 0, MLA_QK_PAD, [seg_x, seg_c], MLA_HEADS, tq=512, scale=1.0, name="mla_attention")
    z_blk0 = n_main // na_w
    w_out_b = w_out.astype(BF16)
    x_new = _out_proj_ab(x, gt, o_na, o_mla, hx, z_blk0, w_out_b, tm=512)

    o_na_c = _attention(hc, 0, HEAD_DIM, [(hc, NA_HEADS, None, hc, 2 * NA_HEADS)], NA_HEADS, tq=l,
                        scale=HEAD_DIM ** -0.5, name="na_ctx_attention")
    o_mla_c = _attention(qc, 0, MLA_QK_PAD, [seg_c], MLA_HEADS, tq=l, scale=1.0, name="mla_ctx_attention")
    ctx_new = _out_proj_ab(ctx, gt_c, o_na_c, o_mla_c, hc, z_blk0, w_out_b, tm=512)
    return x_new, ctx_new


def _deltanet_final_layer(x, ctx, c, c_ctx, w_mod, b_mod, norm, w_in, conv_w, a_log, dt_bias, o_norm, w_out, final_norm):
    b, s, d = x.shape
    (sh, sc, gt), (sh_c, sc_c, _) = _mod_params(c, c_ctx, w_mod, b_mod)
    qk_w = DN_QK_HEADS * HEAD_DIM
    v_w = DN_V_HEADS * HEAD_DIM
    n_main = 2 * qk_w + 2 * v_w
    lane = np.arange(4 * DN_V_HEADS)
    perm = (lane % 4) * DN_V_HEADS + lane // 4
    w_main = w_in[:, :n_main].astype(BF16)
    w_small = w_in[:, n_main:][:, perm].astype(BF16)
    zeros = jnp.zeros((2 * DN_V_HEADS,), F32)
    alog = jnp.concatenate([zeros, a_log.reshape(-1).astype(F32)])[perm].reshape(1, -1)
    dtb = jnp.concatenate([zeros, dt_bias.reshape(-1).astype(F32)])[perm].reshape(1, -1)
    conv_w = conv_w.astype(F32)

    def inputs(t, shift, scale, name):
        h, hs = _norm_proj(t, 0, d, norm, w_main, tm=512, tn=1024, shift=shift, scale=scale, w_small=w_small, name=name)
        q = _dn_conv(h, conv_w, 0, qk_w, HEAD_DIM ** -0.5, tt=512)
        k = _dn_conv(h, conv_w, qk_w, qk_w, 1.0, tt=512)
        v = _dn_conv(h, conv_w, 2 * qk_w, v_w, None, tt=512)
        return h, q, k, v, _dn_gates(hs, alog, dtb)

    hx, qx, kx, vx, gx = inputs(x, sh, sc, "in_proj_dn_x")
    _, qc, kc, vc, gc = inputs(ctx, sh_c, sc_c, "in_proj_dn_ctx")
    o = _dn_chunk(qc, kc, vc, gc, qx, kx, vx, gx)
    return _out_proj_dn(x, gt, o, hx, (2 * qk_w + v_w) // v_w, o_norm, w_out.astype(BF16), final_norm, tm=256)


def kernel(x, c, ctx, c_ctx, ab_w_mod, ab_b_mod, ab_norm, ab_w_in, ab_rpb, ab_q_norm, ab_w_qb, ab_kv_norm, ab_w_kvb, ab_w_out, dn_w_mod, dn_b_mod, dn_norm, dn_w_in, dn_conv, dn_a_log, dn_dt_bias, dn_o_norm, dn_w_out, final_norm):
    assert ab_w_mod.shape[0] == 1 and dn_w_mod.shape[0] == 1, "depth-2 trunk: one even and one odd layer"
    x, ctx = _na_mla_layer(x, ctx, c, c_ctx, ab_w_mod[0], ab_b_mod[0], ab_norm[0], ab_w_in[0], ab_rpb[0],
                           ab_q_norm[0], ab_w_qb[0], ab_kv_norm[0], ab_w_kvb[0], ab_w_out[0])
    return _deltanet_final_layer(x, ctx, c, c_ctx, dn_w_mod[0], dn_b_mod[0], dn_norm[0], dn_w_in[0], dn_conv[0],
                                 dn_a_log[0], dn_dt_bias[0], dn_o_norm[0], dn_w_out[0], final_norm)
```

```python
import functools

import numpy as np
import jax
import jax.numpy as jnp
from jax import lax
from jax.experimental import pallas as pl
from jax.experimental.pallas import tpu as pltpu

F32 = jnp.float32
BF16 = jnp.bfloat16

EPS = 1e-6
GRID_W = 64
WIN_R = 8
WIN_C = 16
NA_HEADS = 8
HEAD_DIM = 128
MLA_HEADS = 8
MLA_RANK = 512
MLA_ROPE_DIM = 64
MLA_SCALE = (HEAD_DIM + MLA_ROPE_DIM) ** -0.5
ROPE_BASE = 10000.0
DN_QK_HEADS = 16
DN_V_HEADS = 32
CONV_K = 5
CHUNK = 64
NEG_BIG = -1e30

LANES = 128
MLA_QK_PAD = 256
NA_Q_ROWS = 8
NA_KEY_ROWS = 16
VMEM_LIMIT = 56 * 1024 * 1024


def _cparams(*sem):
    return pltpu.CompilerParams(dimension_semantics=sem, vmem_limit_bytes=VMEM_LIMIT)


def _adaln_body(c_ref, w_ref, b_ref, o_ref):
    cf = c_ref[...]
    a = (cf * jax.nn.sigmoid(cf)).astype(BF16)
    o_ref[...] = jnp.dot(a, w_ref[...].astype(BF16), preferred_element_type=F32) + b_ref[...]


def _adaln(cond, w_mod, b_mod):
    r, d = cond.shape
    n = w_mod.shape[1]
    tn = 512
    return pl.pallas_call(
        _adaln_body,
        grid=(n // tn,),
        in_specs=[pl.BlockSpec((r, d), lambda j: (0, 0)),
                  pl.BlockSpec((d, tn), lambda j: (0, j)),
                  pl.BlockSpec((1, tn), lambda j: (0, j))],
        out_specs=pl.BlockSpec((r, tn), lambda j: (0, j)),
        out_shape=jax.ShapeDtypeStruct((r, n), F32),
        compiler_params=_cparams("parallel"),
        name="adaln",
    )(cond, w_mod, b_mod.reshape(1, n))


def _proj_body(*refs, modulate, has_small, rope_main, rope_small):
    it = iter(refs)
    x_ref, gain_ref = next(it), next(it)
    shift_ref = scale_ref = ws_ref = c1_ref = c2_ref = s1_ref = s2_ref = os_ref = None
    if modulate:
        shift_ref, scale_ref = next(it), next(it)
    w_ref = next(it)
    if has_small:
        ws_ref = next(it)
    if rope_main:
        c1_ref, c2_ref = next(it), next(it)
    if rope_small:
        s1_ref, s2_ref = next(it), next(it)
    o_ref = next(it)
    if has_small:
        os_ref = next(it)
    xm_ref = next(it)

    @pl.when(pl.program_id(2) == 0)
    def _():
        xf = x_ref[0].astype(F32)
        ms = jnp.mean(xf * xf, axis=-1, keepdims=True)
        y = xf * lax.rsqrt(ms + EPS) * gain_ref[...]
        if modulate:
            y = y * (1.0 + scale_ref[0]) + shift_ref[0]
        xm = y.astype(BF16)
        xm_ref[...] = xm
        if has_small:
            s = jnp.dot(xm, ws_ref[...], preferred_element_type=F32)
            if rope_small:
                s = s * s1_ref[...] + pltpu.roll(s, LANES // 2, axis=1) * s2_ref[...]
            os_ref[0] = s.astype(os_ref.dtype)

    acc = jnp.dot(xm_ref[...], w_ref[...], preferred_element_type=F32)
    if rope_main:
        for h in range(acc.shape[1] // MLA_QK_PAD):
            a = acc[:, h * MLA_QK_PAD:(h + 1) * MLA_QK_PAD]
            a = a * c1_ref[...] + pltpu.roll(a, MLA_QK_PAD - MLA_ROPE_DIM, axis=1) * c2_ref[...]
            o_ref[0, :, h * MLA_QK_PAD:(h + 1) * MLA_QK_PAD] = a.astype(o_ref.dtype)
    else:
        o_ref[0] = acc.astype(o_ref.dtype)


def _norm_proj(x, kblk, k, gain, w, *, tm, tn, shift=None, scale=None, w_small=None,
               rope_main=None, rope_small=None, small_dtype=F32, name="proj"):
    b, t, _ = x.shape
    n = w.shape[1]
    modulate = shift is not None
    has_small = w_small is not None
    tm = min(tm, t)
    args = [x, gain.reshape(1, k).astype(F32)]
    in_specs = [pl.BlockSpec((1, tm, k), lambda bi, i, j: (bi, i, kblk)),
                pl.BlockSpec((1, k), lambda bi, i, j: (0, 0))]
    if modulate:
        args += [shift, scale]
        in_specs += [pl.BlockSpec((1, 1, k), lambda bi, i, j: (bi, 0, 0))] * 2
    args.append(w)
    in_specs.append(pl.BlockSpec((k, tn), lambda bi, i, j: (0, j)))
    if has_small:
        ns = w_small.shape[1]
        args.append(w_small)
        in_specs.append(pl.BlockSpec((k, ns), lambda bi, i, j: (0, 0)))
    if rope_main is not None:
        args += list(rope_main)
        in_specs += [pl.BlockSpec((tm, MLA_QK_PAD), lambda bi, i, j: (i, 0))] * 2
    if rope_small is not None:
        args += list(rope_small)
        in_specs += [pl.BlockSpec((tm, LANES), lambda bi, i, j: (i, 0))] * 2
    out_shape = [jax.ShapeDtypeStruct((b, t, n), BF16)]
    out_specs = [pl.BlockSpec((1, tm, tn), lambda bi, i, j: (bi, i, j))]
    if has_small:
        out_shape.append(jax.ShapeDtypeStruct((b, t, ns), small_dtype))
        out_specs.append(pl.BlockSpec((1, tm, ns), lambda bi, i, j: (bi, i, 0)))
    body = functools.partial(_proj_body, modulate=modulate, has_small=has_small,
                             rope_main=rope_main is not None, rope_small=rope_small is not None)
    out = pl.pallas_call(
        body,
        grid=(b, t // tm, n // tn),
        in_specs=in_specs,
        out_specs=out_specs,
        out_shape=out_shape,
        scratch_shapes=[pltpu.VMEM((tm, k), BF16)],
        compiler_params=_cparams("parallel", "parallel", "arbitrary"),
        name=name,
    )(*args)
    return out if has_small else out[0]


def _attn_body(*refs, seg_lens, has_extra, scale):
    it = iter(refs)
    q_ref = next(it)
    segs = []
    for _ in seg_lens:
        km = next(it)
        ke = next(it) if has_extra else None
        segs.append((km, ke, next(it)))
    o_ref, k_scr, v_scr = next(it), next(it), next(it)

    @pl.when(pl.program_id(2) == 0)
    def _():
        off = 0
        for (km, ke, v), ln in zip(segs, seg_lens):
            k_scr[off:off + ln, 0:HEAD_DIM] = km[0]
            if has_extra:
                k_scr[off:off + ln, HEAD_DIM:2 * HEAD_DIM] = ke[0]
            v_scr[off:off + ln, :] = v[0]
            off += ln

    s = lax.dot_general(q_ref[0], k_scr[...], (((1,), (1,)), ((), ())), preferred_element_type=F32)
    if scale != 1.0:
        s = s * scale
    m = jnp.max(s, axis=-1, keepdims=True)
    e = jnp.exp(s - m)
    l = jnp.sum(e, axis=-1, keepdims=True)
    o = jnp.dot(e.astype(BF16), v_scr[...], preferred_element_type=F32)
    o_ref[0] = (o / l).astype(o_ref.dtype)


def _attention(q, q_blk0, dq, segs, heads, *, tq, scale, name):
    b, t, _ = q.shape
    tq = min(tq, t)
    has_extra = segs[0][2] is not None
    seg_lens = tuple(s[0].shape[1] for s in segs)
    tk = sum(seg_lens)
    args = [q]
    in_specs = [pl.BlockSpec((1, tq, dq), lambda bi, h, i: (bi, i, q_blk0 + h))]
    for km, kb0, ke, v, vb0 in segs:
        ln = km.shape[1]
        args.append(km)
        in_specs.append(pl.BlockSpec((1, ln, HEAD_DIM), lambda bi, h, i, kb0=kb0: (bi, 0, kb0 + h)))
        if has_extra:
            args.append(ke)
            in_specs.append(pl.BlockSpec((1, ln, HEAD_DIM), lambda bi, h, i: (bi, 0, 0)))
        args.append(v)
        in_specs.append(pl.BlockSpec((1, ln, HEAD_DIM), lambda bi, h, i, vb0=vb0: (bi, 0, vb0 + h)))
    body = functools.partial(_attn_body, seg_lens=seg_lens, has_extra=has_extra, scale=scale)
    return pl.pallas_call(
        body,
        grid=(b, heads, t // tq),
        in_specs=in_specs,
        out_specs=pl.BlockSpec((1, tq, HEAD_DIM), lambda bi, h, i: (bi, i, h)),
        out_shape=jax.ShapeDtypeStruct((b, t, heads * HEAD_DIM), BF16),
        scratch_shapes=[pltpu.VMEM((tk, dq), BF16), pltpu.VMEM((tk, HEAD_DIM), BF16)],
        compiler_params=_cparams("parallel", "parallel", "arbitrary"),
        name=name,
    )(*args)


def _na_geometry(rows):
    groups = rows // NA_Q_ROWS
    key_rows = min(NA_KEY_ROWS, rows)
    return groups, key_rows


def _na_bias_table(rpb, rows):
    heads = rpb.shape[0]
    groups, key_rows = _na_geometry(rows)
    g = np.arange(groups)
    slab0 = np.clip(NA_Q_ROWS * g - WIN_R // 2, 0, rows - key_rows)
    r = NA_Q_ROWS * g[:, None] + np.arange(NA_Q_ROWS)[None]
    k_abs = slab0[:, None] + np.arange(key_rows)[None]
    win0 = np.clip(r - WIN_R // 2, 0, rows - WIN_R)
    d_row = k_abs[:, None, :] - r[:, :, None] + (WIN_R - 1)
    ok_row = (k_abs[:, None, :] >= win0[:, :, None]) & (k_abs[:, None, :] < win0[:, :, None] + WIN_R)
    col = np.arange(GRID_W)
    c0 = np.clip(col - WIN_C // 2, 0, GRID_W - WIN_C)
    d_col = col[None, :] - col[:, None] + (WIN_C - 1)
    ok_col = (col[None, :] >= c0[:, None]) & (col[None, :] < c0[:, None] + WIN_C)
    t = rpb[:, np.clip(d_row, 0, 2 * WIN_R - 2)]
    t = t[..., np.clip(d_col, 0, 2 * WIN_C - 2)]
    ok = ok_row[:, :, :, None, None] & ok_col[None, None, None, :, :]
    t = jnp.where(jnp.asarray(ok)[None], t.astype(F32), NEG_BIG)
    t = t.transpose(0, 1, 2, 4, 3, 5)
    return t.reshape(heads, groups, NA_Q_ROWS * GRID_W, key_rows * GRID_W)


def _na_body(q_ref, k_ref, v_ref, kc_ref, vc_ref, bias_ref, o_ref, *, rows, scale):
    groups, key_rows = _na_geometry(rows)
    g = pl.program_id(2)
    slab0 = jnp.clip(NA_Q_ROWS * g - WIN_R // 2, 0, rows - key_rows)
    start = pl.multiple_of(slab0 * GRID_W, 4 * GRID_W)
    nk = key_rows * GRID_W
    q = q_ref[0]
    nt = (((1,), (1,)), ((), ()))
    s_nb = lax.dot_general(q, k_ref[0, pl.ds(start, nk), :], nt, preferred_element_type=F32) * scale + bias_ref[0, g]
    s_cx = lax.dot_general(q, kc_ref[0], nt, preferred_element_type=F32) * scale
    m = jnp.maximum(jnp.max(s_nb, axis=-1, keepdims=True), jnp.max(s_cx, axis=-1, keepdims=True))
    e_nb = jnp.exp(s_nb - m)
    e_cx = jnp.exp(s_cx - m)
    l = jnp.sum(e_nb, axis=-1, keepdims=True) + jnp.sum(e_cx, axis=-1, keepdims=True)
    o = jnp.dot(e_nb.astype(BF16), v_ref[0, pl.ds(start, nk), :], preferred_element_type=F32)
    o = o + jnp.dot(e_cx.astype(BF16), vc_ref[0], preferred_element_type=F32)
    o_ref[0] = (o / l).astype(o_ref.dtype)


def _na_attention(hx, hc, bias):
    b, s, _ = hx.shape
    l = hc.shape[1]
    rows = s // GRID_W
    groups, key_rows = _na_geometry(rows)
    tq = NA_Q_ROWS * GRID_W
    body = functools.partial(_na_body, rows=rows, scale=HEAD_DIM ** -0.5)
    return pl.pallas_call(
        body,
        grid=(NA_HEADS, b, groups),
        in_specs=[pl.BlockSpec((1, tq, HEAD_DIM), lambda h, bi, g: (bi, g, h)),
                  pl.BlockSpec((1, s, HEAD_DIM), lambda h, bi, g: (bi, 0, NA_HEADS + h)),
                  pl.BlockSpec((1, s, HEAD_DIM), lambda h, bi, g: (bi, 0, 2 * NA_HEADS + h)),
                  pl.BlockSpec((1, l, HEAD_DIM), lambda h, bi, g: (bi, 0, NA_HEADS + h)),
                  pl.BlockSpec((1, l, HEAD_DIM), lambda h, bi, g: (bi, 0, 2 * NA_HEADS + h)),
                  pl.BlockSpec((1, groups, tq, key_rows * GRID_W), lambda h, bi, g: (h, 0, 0, 0))],
        out_specs=pl.BlockSpec((1, tq, HEAD_DIM), lambda h, bi, g: (bi, g, h)),
        out_shape=jax.ShapeDtypeStruct((b, s, NA_HEADS * HEAD_DIM), BF16),
        compiler_params=_cparams("parallel", "parallel", "parallel"),
        name="na_attention",
    )(hx, hx, hx, hc, hc, bias)


def _silu(x):
    return x * jax.nn.sigmoid(x)


def _out_ab_body(x_ref, gt_ref, o1_ref, o2_ref, z1_ref, z2_ref, w_ref, y_ref):
    half = o1_ref.shape[2]
    a1 = (o1_ref[0].astype(F32) * _silu(z1_ref[0].astype(F32))).astype(BF16)
    a2 = (o2_ref[0].astype(F32) * _silu(z2_ref[0].astype(F32))).astype(BF16)
    acc = jnp.dot(a1, w_ref[0:half, :], preferred_element_type=F32)
    acc = acc + jnp.dot(a2, w_ref[half:2 * half, :], preferred_element_type=F32)
    y_ref[0] = x_ref[0] + gt_ref[0] * acc


def _out_proj_ab(x, gt, o1, o2, h, z_blk0, w, *, tm):
    b, t, d = x.shape
    half = o1.shape[2]
    tm = min(tm, t)
    return pl.pallas_call(
        _out_ab_body,
        grid=(b, t // tm),
        in_specs=[pl.BlockSpec((1, tm, d), lambda bi, i: (bi, i, 0)),
                  pl.BlockSpec((1, 1, d), lambda bi, i: (bi, 0, 0)),
                  pl.BlockSpec((1, tm, half), lambda bi, i: (bi, i, 0)),
                  pl.BlockSpec((1, tm, half), lambda bi, i: (bi, i, 0)),
                  pl.BlockSpec((1, tm, half), lambda bi, i: (bi, i, z_blk0)),
                  pl.BlockSpec((1, tm, half), lambda bi, i: (bi, i, z_blk0 + 1)),
                  pl.BlockSpec((2 * half, d), lambda bi, i: (0, 0))],
        out_specs=pl.BlockSpec((1, tm, d), lambda bi, i: (bi, i, 0)),
        out_shape=jax.ShapeDtypeStruct((b, t, d), F32),
        compiler_params=_cparams("parallel", "parallel"),
        name="out_proj_ab",
    )(x, gt, o1, o2, h, h, w)


def _out_dn_body(x_ref, gt_ref, o_ref, z_ref, on_ref, w_ref, fn_ref, y_ref, a_scr):
    for h in range(o_ref.shape[2] // HEAD_DIM):
        sl = slice(h * HEAD_DIM, (h + 1) * HEAD_DIM)
        oh = o_ref[0, :, sl].astype(F32)
        ms = jnp.mean(oh * oh, axis=-1, keepdims=True)
        a = oh * lax.rsqrt(ms + EPS) * on_ref[...] * _silu(z_ref[0, :, sl].astype(F32))
        a_scr[:, sl] = a.astype(BF16)
    acc = jnp.dot(a_scr[...], w_ref[...], preferred_element_type=F32)
    y = x_ref[0] + gt_ref[0] * acc
    ms = jnp.mean(y * y, axis=-1, keepdims=True)
    y_ref[0] = y * lax.rsqrt(ms + EPS) * fn_ref[...]


def _out_proj_dn(x, gt, o, h, z_blk, o_norm, w, final_norm, *, tm):
    b, t, d = x.shape
    km = o.shape[2]
    tm = min(tm, t)
    return pl.pallas_call(
        _out_dn_body,
        grid=(b, t // tm),
        in_specs=[pl.BlockSpec((1, tm, d), lambda bi, i: (bi, i, 0)),
                  pl.BlockSpec((1, 1, d), lambda bi, i: (bi, 0, 0)),
                  pl.BlockSpec((1, tm, km), lambda bi, i: (bi, i, 0)),
                  pl.BlockSpec((1, tm, km), lambda bi, i: (bi, i, z_blk)),
                  pl.BlockSpec((1, HEAD_DIM), lambda bi, i: (0, 0)),
                  pl.BlockSpec((km, d), lambda bi, i: (0, 0)),
                  pl.BlockSpec((1, d), lambda bi, i: (0, 0))],
        out_specs=pl.BlockSpec((1, tm, d), lambda bi, i: (bi, i, 0)),
        out_shape=jax.ShapeDtypeStruct((b, t, d), F32),
        scratch_shapes=[pltpu.VMEM((tm, km), BF16)],
        compiler_params=_cparams("parallel", "parallel"),
        name="out_proj_dn",
    )(x, gt, o, h, o_norm.reshape(1, HEAD_DIM).astype(F32), w, final_norm.reshape(1, d).astype(F32))


CONV_HALO = 16


def _conv_body(xp_ref, x_ref, xn_ref, w_ref, o_ref, pad_scr, *, l2_scale):
    i = pl.program_id(1)
    nt = pl.num_programs(1)
    tt = x_ref.shape[1]
    prev = xp_ref[0].astype(F32)
    nxt = xn_ref[0].astype(F32)
    pad_scr[0:CONV_HALO, :] = jnp.where(i > 0, prev, 0.0)
    pad_scr[CONV_HALO:CONV_HALO + tt, :] = x_ref[0].astype(F32)
    pad_scr[CONV_HALO + tt:2 * CONV_HALO + tt, :] = jnp.where(i < nt - 1, nxt, 0.0)
    acc = None
    for j in range(CONV_K):
        lo = CONV_HALO + j - CONV_K // 2
        term = pad_scr[lo:lo + tt, :] * w_ref[j:j + 1, :]
        acc = term if acc is None else acc + term
    y = _silu(acc)
    if l2_scale is None:
        o_ref[0] = y.astype(o_ref.dtype)
    else:
        for h in range(y.shape[1] // HEAD_DIM):
            yh = y[:, h * HEAD_DIM:(h + 1) * HEAD_DIM]
            ss = jnp.sum(yh * yh, axis=-1, keepdims=True)
            o_ref[0, :, h * HEAD_DIM:(h + 1) * HEAD_DIM] = (yh * lax.rsqrt(ss + EPS) * l2_scale).astype(o_ref.dtype)


def _dn_conv(h, conv_w, col0, width, l2_scale, *, tt, cb=512):
    b, t, _ = h.shape
    tt = min(tt, t)
    blk0 = col0 // cb
    hb = tt // CONV_HALO
    nhalo = t // CONV_HALO
    body = functools.partial(_conv_body, l2_scale=l2_scale)
    return pl.pallas_call(
        body,
        grid=(b, t // tt, width // cb),
        in_specs=[pl.BlockSpec((1, CONV_HALO, cb), lambda bi, i, c: (bi, jnp.maximum(i * hb - 1, 0), blk0 + c)),
                  pl.BlockSpec((1, tt, cb), lambda bi, i, c: (bi, i, blk0 + c)),
                  pl.BlockSpec((1, CONV_HALO, cb), lambda bi, i, c: (bi, jnp.minimum((i + 1) * hb, nhalo - 1), blk0 + c)),
                  pl.BlockSpec((CONV_K, cb), lambda bi, i, c: (0, blk0 + c))],
        out_specs=pl.BlockSpec((1, tt, cb), lambda bi, i, c: (bi, i, c)),
        out_shape=jax.ShapeDtypeStruct((b, t, width), BF16),
        scratch_shapes=[pltpu.VMEM((tt + 2 * CONV_HALO, cb), F32)],
        compiler_params=_cparams("parallel", "parallel", "parallel"),
        name="dn_conv",
    )(h, h, h, conv_w)


GATE_TILE = 256


def _split3(x):
    hi = x.astype(BF16)
    r1 = x - hi.astype(F32)
    mid = r1.astype(BF16)
    lo = (r1 - mid.astype(F32)).astype(BF16)
    return hi, mid, lo


def _gates_body(h_ref, alog_ref, dtb_ref, o_ref):
    t = h_ref[0]
    lane = lax.broadcasted_iota(jnp.int32, t.shape, 1) % 4
    beta = jax.nn.sigmoid(t)
    z = t + dtb_ref[...]
    softplus = jnp.maximum(z, 0.0) + jnp.log1p(jnp.exp(-jnp.abs(z)))
    g = -jnp.exp(alog_ref[...]) * softplus
    n = t.shape[0]
    r = lax.broadcasted_iota(jnp.int32, (n, n), 0)
    c = lax.broadcasted_iota(jnp.int32, (n, n), 1)
    same = (r // CHUNK) == (c // CHUNK)
    tri_f = jnp.where(same & (r >= c), 1.0, 0.0).astype(BF16)
    tri_b = jnp.where(same & (r <= c), 1.0, 0.0).astype(BF16)
    pf = pb = None
    for part in _split3(g):
        df = jnp.dot(tri_f, part, preferred_element_type=F32)
        db = jnp.dot(tri_b, part, preferred_element_type=F32)
        pf = df if pf is None else pf + df
        pb = db if pb is None else pb + db
    o_ref[0] = jnp.where(lane < 2, beta, jnp.where(lane == 2, pf, pb))


def _dn_gates(h_small, alog, dtb):
    b, t, n = h_small.shape
    tt = min(GATE_TILE, t)
    return pl.pallas_call(
        _gates_body,
        grid=(b, t // tt),
        in_specs=[pl.BlockSpec((1, tt, n), lambda bi, i: (bi, i, 0)),
                  pl.BlockSpec((1, n), lambda bi, i: (0, 0)),
                  pl.BlockSpec((1, n), lambda bi, i: (0, 0))],
        out_specs=pl.BlockSpec((1, tt, n), lambda bi, i: (bi, i, 0)),
        out_shape=jax.ShapeDtypeStruct((b, t, n), F32),
        compiler_params=_cparams("parallel", "parallel"),
        name="dn_gates",
    )(h_small, alog, dtb)


N_CHAINS = 4
MERGE_LEVELS = (4, 8, 16, 32)


def _bdot(a, b):
    return jnp.dot(a.astype(BF16), b.astype(BF16), preferred_element_type=F32)


def _unit_tri_inverses(lms, eye, level):
    l4s = [jnp.where(level == 0, lm, 0.0) for lm in lms]
    ts = [eye - l4 for l4 in l4s]
    sq = [_bdot(l4, l4) for l4 in l4s]
    ts = [t + _bdot(t, p) for t, p in zip(ts, sq)]
    for m in range(1, len(MERGE_LEVELS) + 1):
        cs = [jnp.where(level == m, lm, 0.0).astype(BF16) for lm in lms]
        tbs = [t.astype(BF16) for t in ts]
        xs = [jnp.dot(tb, c, preferred_element_type=F32) for tb, c in zip(tbs, cs)]
        ts = [t - _bdot(x, tb) for t, x, tb in zip(ts, xs, tbs)]
    return ts


def _chunk_body(qc_ref, kc_ref, vc_ref, gcc_ref, grc_ref, qx_ref, kx_ref, vx_ref, gcx_ref, grx_ref,
                o_ref, s_scr, o_scr, u_scr, wq_scr, ak_scr, dl_scr):
    j = pl.program_id(1)
    lane_shift = (LANES - 8 * j) % LANES
    s_scr[...] = jnp.zeros_like(s_scr)
    o_scr[...] = jnp.zeros_like(o_scr)
    row = lax.broadcasted_iota(jnp.int32, (CHUNK, CHUNK), 0)
    col = lax.broadcasted_iota(jnp.int32, (CHUNK, CHUNK), 1)
    eye = jnp.where(row == col, 1.0, 0.0)
    level = jnp.zeros((CHUNK, CHUNK), jnp.int32)
    for size in MERGE_LEVELS:
        level = level + jnp.where((row // size) == (col // size), 0, 1)
    incl = (row >= col, row <= col)
    strict = (row > col, row < col)
    nt = (((1,), (1,)), ((), ()))

    def run(q_ref, k_ref, v_ref, gc_ref, gr_ref, write_out):
        n_chunks = q_ref.shape[1] // CHUNK

        def chunk_of(i, d):
            n = i if d == 0 else n_chunks - 1 - i
            return n, pl.multiple_of(n * CHUNK, CHUNK)

        chains = [(vh, d) for d in range(2) for vh in range(2)]

        def prepare(i, slot):
            ks, qs, kks, qks, gcols, grows, r0s = {}, {}, {}, {}, {}, {}, {}
            for d in range(2):
                n, r0s[d] = chunk_of(i, d)
                ks[d] = k_ref[0, pl.ds(r0s[d], CHUNK), :]
                kks[d] = lax.dot_general(ks[d], ks[d], nt, preferred_element_type=F32)
                if write_out:
                    qs[d] = q_ref[0, pl.ds(r0s[d], CHUNK), :]
                    qks[d] = lax.dot_general(qs[d], ks[d], nt, preferred_element_type=F32)
                gcols[d] = pltpu.roll(gc_ref[0, pl.ds(r0s[d], CHUNK), :], lane_shift, axis=1)
                grows[d] = gr_ref[0, 0, n]
            beta_c = [gcols[d][:, 4 * vh + d:4 * vh + d + 1] for vh, d in chains]
            g_c = [gcols[d][:, 4 * vh + 2 + d:4 * vh + 3 + d] for vh, d in chains]
            beta_r = [grows[d][4 * vh + d:4 * vh + d + 1, :] for vh, d in chains]
            g_r = [grows[d][4 * vh + 2 + d:4 * vh + 3 + d, :] for vh, d in chains]
            g_last = [g[:, CHUNK - 1:CHUNK] if d == 0 else g[:, 0:1] for g, (vh, d) in zip(g_r, chains)]
            dec = [jnp.where(incl[d], jnp.exp(jnp.where(incl[d], gc - gr, 0.0)), 0.0)
                   for gc, gr, (vh, d) in zip(g_c, g_r, chains)]
            lms = [jnp.where(strict[d], kks[d] * dc, 0.0) * bc for dc, bc, (vh, d) in zip(dec, beta_c, chains)]
            tbs = [t * br for t, br in zip(_unit_tri_inverses(lms, eye, level), beta_r)]
            us = [jnp.dot(tb.astype(BF16), v_ref[0, pl.ds(r0s[d], CHUNK), vh * HEAD_DIM:(vh + 1) * HEAD_DIM],
                          preferred_element_type=F32) for tb, (vh, d) in zip(tbs, chains)]
            ws = [jnp.dot((tb * jnp.exp(gr)).astype(BF16), ks[d], preferred_element_type=F32)
                  for tb, gr, (vh, d) in zip(tbs, g_r, chains)]
            for ci, (vh, d) in enumerate(chains):
                u_scr[slot, ci] = us[ci]
                wq_scr[slot, ci, 0:CHUNK, :] = ws[ci].astype(BF16)
                kg = ks[d].astype(F32) * jnp.exp(g_last[ci] - g_c[ci])
                ak_scr[slot, ci, CHUNK:CHUNK + HEAD_DIM, :] = kg.T.astype(BF16)
                dl_scr[slot, ci] = jnp.broadcast_to(jnp.exp(g_last[ci]), (1, HEAD_DIM))
                if write_out:
                    wq_scr[slot, ci, CHUNK:2 * CHUNK, :] = (qs[d].astype(F32) * jnp.exp(g_c[ci])).astype(BF16)
                    ak_scr[slot, ci, 0:CHUNK, :] = (qks[d] * dec[ci]).astype(BF16)

        def advance(i, slot):
            r0s = [chunk_of(i, d)[1] for d in range(2)]
            s_old = [s_scr[ci] for ci in range(N_CHAINS)]
            sbs = [s.astype(BF16) for s in s_old]
            rows = slice(0, 2 * CHUNK if write_out else CHUNK)
            r1 = [jnp.dot(wq_scr[slot, ci, rows, :], sbs[ci], preferred_element_type=F32) for ci in range(N_CHAINS)]
            vns = [(u_scr[slot, ci] - r1[ci][0:CHUNK]).astype(BF16) for ci in range(N_CHAINS)]
            rows = slice(0 if write_out else CHUNK, CHUNK + HEAD_DIM)
            r2 = [jnp.dot(ak_scr[slot, ci, rows, :], vns[ci], preferred_element_type=F32) for ci in range(N_CHAINS)]
            for ci, (vh, d) in enumerate(chains):
                if write_out:
                    o_scr[pl.ds(r0s[d], CHUNK), vh * HEAD_DIM:(vh + 1) * HEAD_DIM] += r1[ci][CHUNK:2 * CHUNK] + r2[ci][0:CHUNK]
                s_scr[ci] = s_old[ci] * dl_scr[slot, ci] + r2[ci][-HEAD_DIM:]

        prepare(0, 0)

        def pair_of_steps(m, carry):
            i = 2 * m
            advance(i, 0)
            prepare(i + 1, 1)
            advance(i + 1, 1)
            prepare(jnp.minimum(i + 2, n_chunks - 1), 0)
            return carry

        lax.fori_loop(0, n_chunks // 2, pair_of_steps, 0)

    run(qc_ref, kc_ref, vc_ref, gcc_ref, grc_ref, False)
    run(qx_ref, kx_ref, vx_ref, gcx_ref, grx_ref, True)
    o_ref[0] = o_scr[...].astype(o_ref.dtype)


def _gate_rows(gcol):
    b, t, n = gcol.shape
    g = gcol.reshape(b, t // CHUNK, CHUNK, n // 8, 8)
    return g.transpose(0, 3, 1, 4, 2)


def _dn_chunk(qc, kc, vc, gcc, qx, kx, vx, gcx):
    b, s, _ = qx.shape
    l = qc.shape[1]
    grc, grx = _gate_rows(gcc), _gate_rows(gcx)
    pair = 2 * HEAD_DIM

    def seq_specs(t):
        return [pl.BlockSpec((1, t, HEAD_DIM), lambda bi, j: (bi, 0, j)),
                pl.BlockSpec((1, t, HEAD_DIM), lambda bi, j: (bi, 0, j)),
                pl.BlockSpec((1, t, pair), lambda bi, j: (bi, 0, j)),
                pl.BlockSpec((1, t, LANES), lambda bi, j: (bi, 0, 0)),
                pl.BlockSpec((1, 1, t // CHUNK, 8, CHUNK), lambda bi, j: (bi, j, 0, 0, 0))]

    return pl.pallas_call(
        _chunk_body,
        grid=(b, DN_QK_HEADS),
        in_specs=seq_specs(l) + seq_specs(s),
        out_specs=pl.BlockSpec((1, s, pair), lambda bi, j: (bi, 0, j)),
        out_shape=jax.ShapeDtypeStruct((b, s, DN_V_HEADS * HEAD_DIM), BF16),
        scratch_shapes=[pltpu.VMEM((N_CHAINS, HEAD_DIM, HEAD_DIM), F32),
                        pltpu.VMEM((s, pair), F32),
                        pltpu.VMEM((2, N_CHAINS, CHUNK, HEAD_DIM), F32),
                        pltpu.VMEM((2, N_CHAINS, 2 * CHUNK, HEAD_DIM), BF16),
                        pltpu.VMEM((2, N_CHAINS, CHUNK + HEAD_DIM, CHUNK), BF16),
                        pltpu.VMEM((2, N_CHAINS, 1, HEAD_DIM), F32)],
        compiler_params=_cparams("parallel", "parallel"),
        name="dn_chunk",
    )(qc, kc, vc, gcc, grc, qx, kx, vx, gcx, grx)


def _rot_cols(w):
    q = MLA_ROPE_DIM // 4
    return jnp.concatenate([-w[..., q:2 * q], w[..., 0:q], -w[..., 3 * q:4 * q], w[..., 2 * q:3 * q]], axis=-1)


def _rope_cos_sin(t_len):
    q = MLA_ROPE_DIM // 4
    pos = np.arange(t_len)
    inv = ROPE_BASE ** (-np.arange(q, dtype=np.float32) / q)
    ang_r = (pos // GRID_W).astype(np.float32)[:, None] * inv[None, :]
    ang_c = (pos % GRID_W).astype(np.float32)[:, None] * inv[None, :]
    ang = jnp.asarray(np.concatenate([ang_r, ang_r, ang_c, ang_c], axis=1))
    return jnp.cos(ang), jnp.sin(ang)


def _q_tables(cos, sin):
    t = cos.shape[0]
    one, zero = jnp.ones((t, HEAD_DIM), F32), jnp.zeros((t, MLA_ROPE_DIM), F32)
    c1 = jnp.concatenate([one, cos, zero], axis=1) * MLA_SCALE
    c2 = jnp.concatenate([jnp.zeros((t, HEAD_DIM), F32), sin, zero], axis=1) * MLA_SCALE
    return c1, c2


def _k_tables(cos, sin):
    zero = jnp.zeros_like(cos)
    return jnp.concatenate([cos, zero], axis=1), jnp.concatenate([sin, zero], axis=1)


def _mod_params(c, c_ctx, w_mod, b_mod):
    b, d = c.shape
    rows = ((b + 1 + 7) // 8) * 8
    cond = jnp.concatenate([c, c_ctx[None], jnp.zeros((rows - b - 1, d), F32)], axis=0)
    mod = _adaln(cond, w_mod, b_mod)
    sh, sc, gt = (mod[:b, i * d:(i + 1) * d].reshape(b, 1, d) for i in range(3))
    sh_c, sc_c, gt_c = (jnp.broadcast_to(mod[b, i * d:(i + 1) * d].reshape(1, 1, d), (b, 1, d)) for i in range(3))
    return (sh, sc, gt), (sh_c, sc_c, gt_c)


def _na_mla_layer(x, ctx, c, c_ctx, w_mod, b_mod, norm, w_in, rpb, q_norm, w_qb, kv_norm, w_kvb, w_out):
    b, s, d = x.shape
    l = ctx.shape[1]
    (sh, sc, gt), (sh_c, sc_c, gt_c) = _mod_params(c, c_ctx, w_mod, b_mod)
    na_w = NA_HEADS * HEAD_DIM
    n_main = 3 * na_w + 2 * MLA_RANK
    kr_w = w_in[:, n_main:n_main + MLA_ROPE_DIM]
    w_main = jnp.concatenate([w_in[:, :n_main], w_in[:, n_main + MLA_ROPE_DIM:]], axis=1).astype(BF16)
    w_small = jnp.concatenate([kr_w, _rot_cols(kr_w)], axis=1).astype(BF16)
    cos, sin = _rope_cos_sin(s)
    ident = (jnp.ones((l, MLA_ROPE_DIM), F32), jnp.zeros((l, MLA_ROPE_DIM), F32))
    hx, krx = _norm_proj(x, 0, d, norm, w_main, tm=512, tn=1024, shift=sh, scale=sc, w_small=w_small,
                         rope_small=_k_tables(cos, sin), small_dtype=BF16, name="in_proj_ab_x")
    hc, krc = _norm_proj(ctx, 0, d, norm, w_main, tm=512, tn=1024, shift=sh_c, scale=sc_c, w_small=w_small,
                         rope_small=_k_tables(*ident), small_dtype=BF16, name="in_proj_ab_ctx")

    wq = w_qb.reshape(MLA_RANK, MLA_HEADS, HEAD_DIM + MLA_ROPE_DIM)
    wq_r = wq[..., HEAD_DIM:]
    wq_p = jnp.concatenate([wq[..., :HEAD_DIM], wq_r, _rot_cols(wq_r)], axis=-1)
    wq_p = wq_p.reshape(MLA_RANK, MLA_HEADS * MLA_QK_PAD).astype(BF16)
    wkv = w_kvb.reshape(MLA_RANK, MLA_HEADS, 2 * HEAD_DIM)
    wkv_p = jnp.concatenate([wkv[..., :HEAD_DIM].reshape(MLA_RANK, -1), wkv[..., HEAD_DIM:].reshape(MLA_RANK, -1)],
                            axis=1).astype(BF16)
    cq_blk = 3 * na_w // MLA_RANK
    nq = MLA_HEADS * MLA_QK_PAD
    qx = _norm_proj(hx, cq_blk, MLA_RANK, q_norm, wq_p, tm=1024, tn=nq, rope_main=_q_tables(cos, sin), name="mla_q_x")
    qc = _norm_proj(hc, cq_blk, MLA_RANK, q_norm, wq_p, tm=1024, tn=nq, rope_main=_q_tables(*ident), name="mla_q_ctx")
    kvx = _norm_proj(hx, cq_blk + 1, MLA_RANK, kv_norm, wkv_p, tm=1024, tn=nq, name="mla_kv_x")
    kvc = _norm_proj(hc, cq_blk + 1, MLA_RANK, kv_norm, wkv_p, tm=1024, tn=nq, name="mla_kv_ctx")

    o_na = _na_attention(hx, hc, _na_bias_table(rpb, s // GRID_W))
    seg_x = (kvx, 0, krx, kvx, MLA_HEADS)
    seg_c = (kvc, 0, krc, kvc, MLA_HEADS)
    o_mla = _attention(qx, 0, MLA_QK_PAD, [seg_x, seg_c], MLA_HEADS, tq=512, scale=1.0, name="mla_attention")
    z_blk0 = n_main // na_w
    w_out_b = w_out.astype(BF16)
    x_new = _out_proj_ab(x, gt, o_na, o_mla, hx, z_blk0, w_out_b, tm=512)

    o_na_c = _attention(hc, 0, HEAD_DIM, [(hc, NA_HEADS, None, hc, 2 * NA_HEADS)], NA_HEADS, tq=l,
                        scale=HEAD_DIM ** -0.5, name="na_ctx_attention")
    o_mla_c = _attention(qc, 0, MLA_QK_PAD, [seg_c], MLA_HEADS, tq=l, scale=1.0, name="mla_ctx_attention")
    ctx_new = _out_proj_ab(ctx, gt_c, o_na_c, o_mla_c, hc, z_blk0, w_out_b, tm=512)
    return x_new, ctx_new


def _deltanet_final_layer(x, ctx, c, c_ctx, w_mod, b_mod, norm, w_in, conv_w, a_log, dt_bias, o_norm, w_out, final_norm):
    b, s, d = x.shape
    (sh, sc, gt), (sh_c, sc_c, _) = _mod_params(c, c_ctx, w_mod, b_mod)
    qk_w = DN_QK_HEADS * HEAD_DIM
    v_w = DN_V_HEADS * HEAD_DIM
    n_main = 2 * qk_w + 2 * v_w
    lane = np.arange(4 * DN_V_HEADS)
    perm = (lane % 4) * DN_V_HEADS + lane // 4
    w_main = w_in[:, :n_main].astype(BF16)
    w_small = w_in[:, n_main:][:, perm].astype(BF16)
    zeros = jnp.zeros((2 * DN_V_HEADS,), F32)
    alog = jnp.concatenate([zeros, a_log.reshape(-1).astype(F32)])[perm].reshape(1, -1)
    dtb = jnp.concatenate([zeros, dt_bias.reshape(-1).astype(F32)])[perm].reshape(1, -1)
    conv_w = conv_w.astype(F32)

    def inputs(t, shift, scale, name):
        h, hs = _norm_proj(t, 0, d, norm, w_main, tm=512, tn=1024, shift=shift, scale=scale, w_small=w_small, name=name)
        q = _dn_conv(h, conv_w, 0, qk_w, HEAD_DIM ** -0.5, tt=512)
        k = _dn_conv(h, conv_w, qk_w, qk_w, 1.0, tt=512)
        v = _dn_conv(h, conv_w, 2 * qk_w, v_w, None, tt=512)
        return h, q, k, v, _dn_gates(hs, alog, dtb)

    hx, qx, kx, vx, gx = inputs(x, sh, sc, "in_proj_dn_x")
    _, qc, kc, vc, gc = inputs(ctx, sh_c, sc_c, "in_proj_dn_ctx")
    o = _dn_chunk(qc, kc, vc, gc, qx, kx, vx, gx)
    return _out_proj_dn(x, gt, o, hx, (2 * qk_w + v_w) // v_w, o_norm, w_out.astype(BF16), final_norm, tm=256)


def kernel(x, c, ctx, c_ctx, ab_w_mod, ab_b_mod, ab_norm, ab_w_in, ab_rpb, ab_q_norm, ab_w_qb, ab_kv_norm, ab_w_kvb, ab_w_out, dn_w_mod, dn_b_mod, dn_norm, dn_w_in, dn_conv, dn_a_log, dn_dt_bias, dn_o_norm, dn_w_out, final_norm):
    assert ab_w_mod.shape[0] == 1 and dn_w_mod.shape[0] == 1, "depth-2 trunk: one even and one odd layer"
    x, ctx = _na_mla_layer(x, ctx, c, c_ctx, ab_w_mod[0], ab_b_mod[0], ab_norm[0], ab_w_in[0], ab_rpb[0],
                           ab_q_norm[0], ab_w_qb[0], ab_kv_norm[0], ab_w_kvb[0], ab_w_out[0])
    return _deltanet_final_layer(x, ctx, c, c_ctx, dn_w_mod[0], dn_b_mod[0], dn_norm[0], dn_w_in[0], dn_conv[0],
                                 dn_a_log[0], dn_dt_bias[0], dn_o_norm[0], dn_w_out[0], final_norm)
```

```python
import functools

import numpy as np
import jax
import jax.numpy as jnp
from jax import lax
from jax.experimental import pallas as pl
from jax.experimental.pallas import tpu as pltpu

F32 = jnp.float32
BF16 = jnp.bfloat16

EPS = 1e-6
GRID_W = 64
WIN_R = 8
WIN_C = 16
NA_HEADS = 8
HEAD_DIM = 128
MLA_HEADS = 8
MLA_RANK = 512
MLA_ROPE_DIM = 64
MLA_SCALE = (HEAD_DIM + MLA_ROPE_DIM) ** -0.5
ROPE_BASE = 10000.0
DN_QK_HEADS = 16
DN_V_HEADS = 32
CONV_K = 5
CHUNK = 64
NEG_BIG = -1e30

LANES = 128
MLA_QK_PAD = 256
NA_Q_ROWS = 8
NA_KEY_ROWS = 16
VMEM_LIMIT = 56 * 1024 * 1024


def _cparams(*sem):
    return pltpu.CompilerParams(dimension_semantics=sem, vmem_limit_bytes=VMEM_LIMIT)


def _adaln_body(c_ref, w_ref, b_ref, o_ref):
    cf = c_ref[...]
    a = (cf * jax.nn.sigmoid(cf)).astype(BF16)
    o_ref[...] = jnp.dot(a, w_ref[...].astype(BF16), preferred_element_type=F32) + b_ref[...]


def _adaln(cond, w_mod, b_mod):
    r, d = cond.shape
    n = w_mod.shape[1]
    tn = 512
    return pl.pallas_call(
        _adaln_body,
        grid=(n // tn,),
        in_specs=[pl.BlockSpec((r, d), lambda j: (0, 0)),
                  pl.BlockSpec((d, tn), lambda j: (0, j)),
                  pl.BlockSpec((1, tn), lambda j: (0, j))],
        out_specs=pl.BlockSpec((r, tn), lambda j: (0, j)),
        out_shape=jax.ShapeDtypeStruct((r, n), F32),
        compiler_params=_cparams("parallel"),
        name="adaln",
    )(cond, w_mod, b_mod.reshape(1, n))


def _proj_body(*refs, modulate, has_small, rope_main, rope_small):
    it = iter(refs)
    x_ref, gain_ref = next(it), next(it)
    shift_ref = scale_ref = ws_ref = c1_ref = c2_ref = s1_ref = s2_ref = os_ref = None
    if modulate:
        shift_ref, scale_ref = next(it), next(it)
    w_ref = next(it)
    if has_small:
        ws_ref = next(it)
    if rope_main:
        c1_ref, c2_ref = next(it), next(it)
    if rope_small:
        s1_ref, s2_ref = next(it), next(it)
    o_ref = next(it)
    if has_small:
        os_ref = next(it)
    xm_ref = next(it)

    @pl.when(pl.program_id(2) == 0)
    def _():
        xf = x_ref[0].astype(F32)
        ms = jnp.mean(xf * xf, axis=-1, keepdims=True)
        y = xf * lax.rsqrt(ms + EPS) * gain_ref[...]
        if modulate:
            y = y * (1.0 + scale_ref[0]) + shift_ref[0]
        xm = y.astype(BF16)
        xm_ref[...] = xm
        if has_small:
            s = jnp.dot(xm, ws_ref[...], preferred_element_type=F32)
            if rope_small:
                s = s * s1_ref[...] + pltpu.roll(s, LANES // 2, axis=1) * s2_ref[...]
            os_ref[0] = s.astype(os_ref.dtype)

    acc = jnp.dot(xm_ref[...], w_ref[...], preferred_element_type=F32)
    if rope_main:
        for h in range(acc.shape[1] // MLA_QK_PAD):
            a = acc[:, h * MLA_QK_PAD:(h + 1) * MLA_QK_PAD]
            a = a * c1_ref[...] + pltpu.roll(a, MLA_QK_PAD - MLA_ROPE_DIM, axis=1) * c2_ref[...]
            o_ref[0, :, h * MLA_QK_PAD:(h + 1) * MLA_QK_PAD] = a.astype(o_ref.dtype)
    else:
        o_ref[0] = acc.astype(o_ref.dtype)


def _norm_proj(x, kblk, k, gain, w, *, tm, tn, shift=None, scale=None, w_small=None,
               rope_main=None, rope_small=None, small_dtype=F32, name="proj"):
    b, t, _ = x.shape
    n = w.shape[1]
    modulate = shift is not None
    has_small = w_small is not None
    tm = min(tm, t)
    args = [x, gain.reshape(1, k).astype(F32)]
    in_specs = [pl.BlockSpec((1, tm, k), lambda bi, i, j: (bi, i, kblk)),
                pl.BlockSpec((1, k), lambda bi, i, j: (0, 0))]
    if modulate:
        args += [shift, scale]
        in_specs += [pl.BlockSpec((1, 1, k), lambda bi, i, j: (bi, 0, 0))] * 2
    args.append(w)
    in_specs.append(pl.BlockSpec((k, tn), lambda bi, i, j: (0, j)))
    if has_small:
        ns = w_small.shape[1]
        args.append(w_small)
        in_specs.append(pl.BlockSpec((k, ns), lambda bi, i, j: (0, 0)))
    if rope_main is not None:
        args += list(rope_main)
        in_specs += [pl.BlockSpec((tm, MLA_QK_PAD), lambda bi, i, j: (i, 0))] * 2
    if rope_small is not None:
        args += list(rope_small)
        in_specs += [pl.BlockSpec((tm, LANES), lambda bi, i, j: (i, 0))] * 2
    out_shape = [jax.ShapeDtypeStruct((b, t, n), BF16)]
    out_specs = [pl.BlockSpec((1, tm, tn), lambda bi, i, j: (bi, i, j))]
    if has_small:
        out_shape.append(jax.ShapeDtypeStruct((b, t, ns), small_dtype))
        out_specs.append(pl.BlockSpec((1, tm, ns), lambda bi, i, j: (bi, i, 0)))
    body = functools.partial(_proj_body, modulate=modulate, has_small=has_small,
                             rope_main=rope_main is not None, rope_small=rope_small is not None)
    out = pl.pallas_call(
        body,
        grid=(b, t // tm, n // tn),
        in_specs=in_specs,
        out_specs=out_specs,
        out_shape=out_shape,
        scratch_shapes=[pltpu.VMEM((tm, k), BF16)],
        compiler_params=_cparams("parallel", "parallel", "arbitrary"),
        name=name,
    )(*args)
    return out if has_small else out[0]


def _attn_body(*refs, seg_lens, has_extra, scale):
    it = iter(refs)
    q_ref = next(it)
    segs = []
    for _ in seg_lens:
        km = next(it)
        ke = next(it) if has_extra else None
        segs.append((km, ke, next(it)))
    o_ref, k_scr, v_scr = next(it), next(it), next(it)

    @pl.when(pl.program_id(2) == 0)
    def _():
        off = 0
        for (km, ke, v), ln in zip(segs, seg_lens):
            k_scr[off:off + ln, 0:HEAD_DIM] = km[0]
            if has_extra:
                k_scr[off:off + ln, HEAD_DIM:2 * HEAD_DIM] = ke[0]
            v_scr[off:off + ln, :] = v[0]
            off += ln

    s = lax.dot_general(q_ref[0], k_scr[...], (((1,), (1,)), ((), ())), preferred_element_type=F32)
    if scale != 1.0:
        s = s * scale
    m = jnp.max(s, axis=-1, keepdims=True)
    e = jnp.exp(s - m)
    l = jnp.sum(e, axis=-1, keepdims=True)
    o = jnp.dot(e.astype(BF16), v_scr[...], preferred_element_type=F32)
    o_ref[0] = (o / l).astype(o_ref.dtype)


def _attention(q, q_blk0, dq, segs, heads, *, tq, scale, name):
    b, t, _ = q.shape
    tq = min(tq, t)
    has_extra = segs[0][2] is not None
    seg_lens = tuple(s[0].shape[1] for s in segs)
    tk = sum(seg_lens)
    args = [q]
    in_specs = [pl.BlockSpec((1, tq, dq), lambda bi, h, i: (bi, i, q_blk0 + h))]
    for km, kb0, ke, v, vb0 in segs:
        ln = km.shape[1]
        args.append(km)
        in_specs.append(pl.BlockSpec((1, ln, HEAD_DIM), lambda bi, h, i, kb0=kb0: (bi, 0, kb0 + h)))
        if has_extra:
            args.append(ke)
            in_specs.append(pl.BlockSpec((1, ln, HEAD_DIM), lambda bi, h, i: (bi, 0, 0)))
        args.append(v)
        in_specs.append(pl.BlockSpec((1, ln, HEAD_DIM), lambda bi, h, i, vb0=vb0: (bi, 0, vb0 + h)))
    body = functools.partial(_attn_body, seg_lens=seg_lens, has_extra=has_extra, scale=scale)
    return pl.pallas_call(
        body,
        grid=(b, heads, t // tq),
        in_specs=in_specs,
        out_specs=pl.BlockSpec((1, tq, HEAD_DIM), lambda bi, h, i: (bi, i, h)),
        out_shape=jax.ShapeDtypeStruct((b, t, heads * HEAD_DIM), BF16),
        scratch_shapes=[pltpu.VMEM((tk, dq), BF16), pltpu.VMEM((tk, HEAD_DIM), BF16)],
        compiler_params=_cparams("parallel", "parallel", "arbitrary"),
        name=name,
    )(*args)


def _na_geometry(rows):
    groups = rows // NA_Q_ROWS
    key_rows = min(NA_KEY_ROWS, rows)
    return groups, key_rows


def _na_bias_table(rpb, rows):
    heads = rpb.shape[0]
    groups, key_rows = _na_geometry(rows)
    g = np.arange(groups)
    slab0 = np.clip(NA_Q_ROWS * g - WIN_R // 2, 0, rows - key_rows)
    r = NA_Q_ROWS * g[:, None] + np.arange(NA_Q_ROWS)[None]
    k_abs = slab0[:, None] + np.arange(key_rows)[None]
    win0 = np.clip(r - WIN_R // 2, 0, rows - WIN_R)
    d_row = k_abs[:, None, :] - r[:, :, None] + (WIN_R - 1)
    ok_row = (k_abs[:, None, :] >= win0[:, :, None]) & (k_abs[:, None, :] < win0[:, :, None] + WIN_R)
    col = np.arange(GRID_W)
    c0 = np.clip(col - WIN_C // 2, 0, GRID_W - WIN_C)
    d_col = col[None, :] - col[:, None] + (WIN_C - 1)
    ok_col = (col[None, :] >= c0[:, None]) & (col[None, :] < c0[:, None] + WIN_C)
    t = rpb[:, np.clip(d_row, 0, 2 * WIN_R - 2)]
    t = t[..., np.clip(d_col, 0, 2 * WIN_C - 2)]
    ok = ok_row[:, :, :, None, None] & ok_col[None, None, None, :, :]
    t = jnp.where(jnp.asarray(ok)[None], t.astype(F32), NEG_BIG)
    t = t.transpose(0, 1, 2, 4, 3, 5)
    return t.reshape(heads, groups, NA_Q_ROWS * GRID_W, key_rows * GRID_W)


def _na_body(q_ref, k_ref, v_ref, kc_ref, vc_ref, bias_ref, o_ref, *, rows, scale):
    groups, key_rows = _na_geometry(rows)
    g = pl.program_id(2)
    slab0 = jnp.clip(NA_Q_ROWS * g - WIN_R // 2, 0, rows - key_rows)
    start = pl.multiple_of(slab0 * GRID_W, 4 * GRID_W)
    nk = key_rows * GRID_W
    q = q_ref[0]
    nt = (((1,), (1,)), ((), ()))
    s_nb = lax.dot_general(q, k_ref[0, pl.ds(start, nk), :], nt, preferred_element_type=F32) * scale + bias_ref[0, g]
    s_cx = lax.dot_general(q, kc_ref[0], nt, preferred_element_type=F32) * scale
    m = jnp.maximum(jnp.max(s_nb, axis=-1, keepdims=True), jnp.max(s_cx, axis=-1, keepdims=True))
    e_nb = jnp.exp(s_nb - m)
    e_cx = jnp.exp(s_cx - m)
    l = jnp.sum(e_nb, axis=-1, keepdims=True) + jnp.sum(e_cx, axis=-1, keepdims=True)
    o = jnp.dot(e_nb.astype(BF16), v_ref[0, pl.ds(start, nk), :], preferred_element_type=F32)
    o = o + jnp.dot(e_cx.astype(BF16), vc_ref[0], preferred_element_type=F32)
    o_ref[0] = (o / l).astype(o_ref.dtype)


def _na_attention(hx, hc, bias):
    b, s, _ = hx.shape
    l = hc.shape[1]
    rows = s // GRID_W
    groups, key_rows = _na_geometry(rows)
    tq = NA_Q_ROWS * GRID_W
    body = functools.partial(_na_body, rows=rows, scale=HEAD_DIM ** -0.5)
    return pl.pallas_call(
        body,
        grid=(NA_HEADS, b, groups),
        in_specs=[pl.BlockSpec((1, tq, HEAD_DIM), lambda h, bi, g: (bi, g, h)),
                  pl.BlockSpec((1, s, HEAD_DIM), lambda h, bi, g: (bi, 0, NA_HEADS + h)),
                  pl.BlockSpec((1, s, HEAD_DIM), lambda h, bi, g: (bi, 0, 2 * NA_HEADS + h)),
                  pl.BlockSpec((1, l, HEAD_DIM), lambda h, bi, g: (bi, 0, NA_HEADS + h)),
                  pl.BlockSpec((1, l, HEAD_DIM), lambda h, bi, g: (bi, 0, 2 * NA_HEADS + h)),
                  pl.BlockSpec((1, groups, tq, key_rows * GRID_W), lambda h, bi, g: (h, 0, 0, 0))],
        out_specs=pl.BlockSpec((1, tq, HEAD_DIM), lambda h, bi, g: (bi, g, h)),
        out_shape=jax.ShapeDtypeStruct((b, s, NA_HEADS * HEAD_DIM), BF16),
        compiler_params=_cparams("parallel", "parallel", "parallel"),
        name="na_attention",
    )(hx, hx, hx, hc, hc, bias)


def _silu(x):
    return x * jax.nn.sigmoid(x)


def _out_ab_body(x_ref, gt_ref, o1_ref, o2_ref, z1_ref, z2_ref, w_ref, y_ref):
    half = o1_ref.shape[2]
    a1 = (o1_ref[0].astype(F32) * _silu(z1_ref[0].astype(F32))).astype(BF16)
    a2 = (o2_ref[0].astype(F32) * _silu(z2_ref[0].astype(F32))).astype(BF16)
    acc = jnp.dot(a1, w_ref[0:half, :], preferred_element_type=F32)
    acc = acc + jnp.dot(a2, w_ref[half:2 * half, :], preferred_element_type=F32)
    y_ref[0] = x_ref[0] + gt_ref[0] * acc


def _out_proj_ab(x, gt, o1, o2, h, z_blk0, w, *, tm):
    b, t, d = x.shape
    half = o1.shape[2]
    tm = min(tm, t)
    return pl.pallas_call(
        _out_ab_body,
        grid=(b, t // tm),
        in_specs=[pl.BlockSpec((1, tm, d), lambda bi, i: (bi, i, 0)),
                  pl.BlockSpec((1, 1, d), lambda bi, i: (bi, 0, 0)),
                  pl.BlockSpec((1, tm, half), lambda bi, i: (bi, i, 0)),
                  pl.BlockSpec((1, tm, half), lambda bi, i: (bi, i, 0)),
                  pl.BlockSpec((1, tm, half), lambda bi, i: (bi, i, z_blk0)),
                  pl.BlockSpec((1, tm, half), lambda bi, i: (bi, i, z_blk0 + 1)),
                  pl.BlockSpec((2 * half, d), lambda bi, i: (0, 0))],
        out_specs=pl.BlockSpec((1, tm, d), lambda bi, i: (bi, i, 0)),
        out_shape=jax.ShapeDtypeStruct((b, t, d), F32),
        compiler_params=_cparams("parallel", "parallel"),
        name="out_proj_ab",
    )(x, gt, o1, o2, h, h, w)


def _out_dn_body(x_ref, gt_ref, o_ref, z_ref, on_ref, w_ref, fn_ref, y_ref, a_scr):
    for h in range(o_ref.shape[2] // HEAD_DIM):
        sl = slice(h * HEAD_DIM, (h + 1) * HEAD_DIM)
        oh = o_ref[0, :, sl].astype(F32)
        ms = jnp.mean(oh * oh, axis=-1, keepdims=True)
        a = oh * lax.rsqrt(ms + EPS) * on_ref[...] * _silu(z_ref[0, :, sl].astype(F32))
        a_scr[:, sl] = a.astype(BF16)
    acc = jnp.dot(a_scr[...], w_ref[...], preferred_element_type=F32)
    y = x_ref[0] + gt_ref[0] * acc
    ms = jnp.mean(y * y, axis=-1, keepdims=True)
    y_ref[0] = y * lax.rsqrt(ms + EPS) * fn_ref[...]


def _out_proj_dn(x, gt, o, h, z_blk, o_norm, w, final_norm, *, tm):
    b, t, d = x.shape
    km = o.shape[2]
    tm = min(tm, t)
    return pl.pallas_call(
        _out_dn_body,
        grid=(b, t // tm),
        in_specs=[pl.BlockSpec((1, tm, d), lambda bi, i: (bi, i, 0)),
                  pl.BlockSpec((1, 1, d), lambda bi, i: (bi, 0, 0)),
                  pl.BlockSpec((1, tm, km), lambda bi, i: (bi, i, 0)),
                  pl.BlockSpec((1, tm, km), lambda bi, i: (bi, i, z_blk)),
                  pl.BlockSpec((1, HEAD_DIM), lambda bi, i: (0, 0)),
                  pl.BlockSpec((km, d), lambda bi, i: (0, 0)),
                  pl.BlockSpec((1, d), lambda bi, i: (0, 0))],
        out_specs=pl.BlockSpec((1, tm, d), lambda bi, i: (bi, i, 0)),
        out_shape=jax.ShapeDtypeStruct((b, t, d), F32),
        scratch_shapes=[pltpu.VMEM((tm, km), BF16)],
        compiler_params=_cparams("parallel", "parallel"),
        name="out_proj_dn",
    )(x, gt, o, h, o_norm.reshape(1, HEAD_DIM).astype(F32), w, final_norm.reshape(1, d).astype(F32))


CONV_HALO = 16


def _conv_body(xp_ref, x_ref, xn_ref, w_ref, o_ref, pad_scr, *, l2_scale):
    i = pl.program_id(1)
    nt = pl.num_programs(1)
    tt = x_ref.shape[1]
    prev = xp_ref[0].astype(F32)
    nxt = xn_ref[0].astype(F32)
    pad_scr[0:CONV_HALO, :] = jnp.where(i > 0, prev, 0.0)
    pad_scr[CONV_HALO:CONV_HALO + tt, :] = x_ref[0].astype(F32)
    pad_scr[CONV_HALO + tt:2 * CONV_HALO + tt, :] = jnp.where(i < nt - 1, nxt, 0.0)
    acc = None
    for j in range(CONV_K):
        lo = CONV_HALO + j - CONV_K // 2
        term = pad_scr[lo:lo + tt, :] * w_ref[j:j + 1, :]
        acc = term if acc is None else acc + term
    y = _silu(acc)
    if l2_scale is None:
        o_ref[0] = y.astype(o_ref.dtype)
    else:
        for h in range(y.shape[1] // HEAD_DIM):
            yh = y[:, h * HEAD_DIM:(h + 1) * HEAD_DIM]
            ss = jnp.sum(yh * yh, axis=-1, keepdims=True)
            o_ref[0, :, h * HEAD_DIM:(h + 1) * HEAD_DIM] = (yh * lax.rsqrt(ss + EPS) * l2_scale).astype(o_ref.dtype)


def _dn_conv(h, conv_w, col0, width, l2_scale, *, tt, cb=512):
    b, t, _ = h.shape
    tt = min(tt, t)
    blk0 = col0 // cb
    hb = tt // CONV_HALO
    nhalo = t // CONV_HALO
    body = functools.partial(_conv_body, l2_scale=l2_scale)
    return pl.pallas_call(
        body,
        grid=(b, t // tt, width // cb),
        in_specs=[pl.BlockSpec((1, CONV_HALO, cb), lambda bi, i, c: (bi, jnp.maximum(i * hb - 1, 0), blk0 + c)),
                  pl.BlockSpec((1, tt, cb), lambda bi, i, c: (bi, i, blk0 + c)),
                  pl.BlockSpec((1, CONV_HALO, cb), lambda bi, i, c: (bi, jnp.minimum((i + 1) * hb, nhalo - 1), blk0 + c)),
                  pl.BlockSpec((CONV_K, cb), lambda bi, i, c: (0, blk0 + c))],
        out_specs=pl.BlockSpec((1, tt, cb), lambda bi, i, c: (bi, i, c)),
        out_shape=jax.ShapeDtypeStruct((b, t, width), BF16),
        scratch_shapes=[pltpu.VMEM((tt + 2 * CONV_HALO, cb), F32)],
        compiler_params=_cparams("parallel", "parallel", "parallel"),
        name="dn_conv",
    )(h, h, h, conv_w)


GATE_TILE = 256


def _split3(x):
    hi = x.astype(BF16)
    r1 = x - hi.astype(F32)
    mid = r1.astype(BF16)
    lo = (r1 - mid.astype(F32)).astype(BF16)
    return hi, mid, lo


def _gates_body(h_ref, alog_ref, dtb_ref, o_ref):
    t = h_ref[0]
    lane = lax.broadcasted_iota(jnp.int32, t.shape, 1) % 4
    beta = jax.nn.sigmoid(t)
    z = t + dtb_ref[...]
    softplus = jnp.maximum(z, 0.0) + jnp.log1p(jnp.exp(-jnp.abs(z)))
    g = -jnp.exp(alog_ref[...]) * softplus
    n = t.shape[0]
    r = lax.broadcasted_iota(jnp.int32, (n, n), 0)
    c = lax.broadcasted_iota(jnp.int32, (n, n), 1)
    same = (r // CHUNK) == (c // CHUNK)
    tri_f = jnp.where(same & (r >= c), 1.0, 0.0).astype(BF16)
    tri_b = jnp.where(same & (r <= c), 1.0, 0.0).astype(BF16)
    pf = pb = None
    for part in _split3(g):
        df = jnp.dot(tri_f, part, preferred_element_type=F32)
        db = jnp.dot(tri_b, part, preferred_element_type=F32)
        pf = df if pf is None else pf + df
        pb = db if pb is None else pb + db
    o_ref[0] = jnp.where(lane < 2, beta, jnp.where(lane == 2, pf, pb))


def _dn_gates(h_small, alog, dtb):
    b, t, n = h_small.shape
    tt = min(GATE_TILE, t)
    return pl.pallas_call(
        _gates_body,
        grid=(b, t // tt),
        in_specs=[pl.BlockSpec((1, tt, n), lambda bi, i: (bi, i, 0)),
                  pl.BlockSpec((1, n), lambda bi, i: (0, 0)),
                  pl.BlockSpec((1, n), lambda bi, i: (0, 0))],
        out_specs=pl.BlockSpec((1, tt, n), lambda bi, i: (bi, i, 0)),
        out_shape=jax.ShapeDtypeStruct((b, t, n), F32),
        compiler_params=_cparams("parallel", "parallel"),
        name="dn_gates",
    )(h_small, alog, dtb)


HEADS_PER_CALL = 4
N_CHAINS = 4 * HEADS_PER_CALL
GATE_LANES = 8
MERGE_LEVELS = (4, 8, 16, 32)


def _bdot(a, b):
    return jnp.dot(a.astype(BF16), b.astype(BF16), preferred_element_type=F32)


def _unit_tri_inverses(lms, eye, level):
    l4s = [jnp.where(level == 0, lm, 0.0) for lm in lms]
    ts = [eye - l4 for l4 in l4s]
    sq = [_bdot(l4, l4) for l4 in l4s]
    ts = [t + _bdot(t, p) for t, p in zip(ts, sq)]
    for m in range(1, len(MERGE_LEVELS) + 1):
        cs = [jnp.where(level == m, lm, 0.0).astype(BF16) for lm in lms]
        tbs = [t.astype(BF16) for t in ts]
        xs = [jnp.dot(tb, c, preferred_element_type=F32) for tb, c in zip(tbs, cs)]
        ts = [t - _bdot(x, tb) for t, x, tb in zip(ts, xs, tbs)]
    return ts


def _chunk_body(qc_ref, kc_ref, vc_ref, gcc_ref, grc_ref, qx_ref, kx_ref, vx_ref, gcx_ref, grx_ref,
                o_ref, s_scr, o_scr, u_scr, wq_scr, ak_scr, dl_scr):
    jg = pl.program_id(1)
    group_lanes = GATE_LANES * HEADS_PER_CALL
    lane_shift = (LANES - group_lanes * jg) % LANES
    s_scr[...] = jnp.zeros_like(s_scr)
    o_scr[...] = jnp.zeros_like(o_scr)
    row = lax.broadcasted_iota(jnp.int32, (CHUNK, CHUNK), 0)
    col = lax.broadcasted_iota(jnp.int32, (CHUNK, CHUNK), 1)
    eye = jnp.where(row == col, 1.0, 0.0)
    level = jnp.zeros((CHUNK, CHUNK), jnp.int32)
    for size in MERGE_LEVELS:
        level = level + jnp.where((row // size) == (col // size), 0, 1)
    incl = (row >= col, row <= col)
    strict = (row > col, row < col)
    nt = (((1,), (1,)), ((), ()))
    chains = [(jh, vh, d) for d in range(2) for jh in range(HEADS_PER_CALL) for vh in range(2)]
    streams = [(jh, d) for d in range(2) for jh in range(HEADS_PER_CALL)]

    def head_cols(jh):
        return slice(jh * HEAD_DIM, (jh + 1) * HEAD_DIM)

    def value_cols(jh, vh):
        return slice((2 * jh + vh) * HEAD_DIM, (2 * jh + vh + 1) * HEAD_DIM)

    def run(q_ref, k_ref, v_ref, gc_ref, gr_ref, write_out):
        n_chunks = q_ref.shape[1] // CHUNK

        def chunk_of(i, d):
            n = i if d == 0 else n_chunks - 1 - i
            return n, pl.multiple_of(n * CHUNK, CHUNK)

        def prepare(i, slot):
            ns, r0s, gcols = {}, {}, {}
            for d in range(2):
                ns[d], r0s[d] = chunk_of(i, d)
                gcols[d] = pltpu.roll(gc_ref[0, pl.ds(r0s[d], CHUNK), :], lane_shift, axis=1)
            ks = {(jh, d): k_ref[0, pl.ds(r0s[d], CHUNK), head_cols(jh)] for jh, d in streams}
            kks = {sd: lax.dot_general(ks[sd], ks[sd], nt, preferred_element_type=F32) for sd in streams}
            if write_out:
                qs = {(jh, d): q_ref[0, pl.ds(r0s[d], CHUNK), head_cols(jh)] for jh, d in streams}
                qks = {sd: lax.dot_general(qs[sd], ks[sd], nt, preferred_element_type=F32) for sd in streams}
            grows = {(jh, d): gr_ref[0, jh, ns[d]] for jh, d in streams}
            lane0 = [GATE_LANES * jh + 4 * vh + d for jh, vh, d in chains]
            beta_c = [gcols[d][:, l:l + 1] for l, (jh, vh, d) in zip(lane0, chains)]
            g_c = [gcols[d][:, l + 2:l + 3] for l, (jh, vh, d) in zip(lane0, chains)]
            beta_r = [grows[jh, d][4 * vh + d:4 * vh + d + 1, :] for jh, vh, d in chains]
            g_r = [grows[jh, d][4 * vh + 2 + d:4 * vh + 3 + d, :] for jh, vh, d in chains]
            g_last = [g[:, CHUNK - 1:CHUNK] if d == 0 else g[:, 0:1] for g, (jh, vh, d) in zip(g_r, chains)]
            dec = [jnp.where(incl[d], jnp.exp(jnp.where(incl[d], gc - gr, 0.0)), 0.0)
                   for gc, gr, (jh, vh, d) in zip(g_c, g_r, chains)]
            lms = [jnp.where(strict[d], kks[jh, d] * dc, 0.0) * bc
                   for dc, bc, (jh, vh, d) in zip(dec, beta_c, chains)]
            tbs = [t * br for t, br in zip(_unit_tri_inverses(lms, eye, level), beta_r)]
            us = [jnp.dot(tb.astype(BF16), v_ref[0, pl.ds(r0s[d], CHUNK), value_cols(jh, vh)],
                          preferred_element_type=F32) for tb, (jh, vh, d) in zip(tbs, chains)]
            ws = [jnp.dot((tb * jnp.exp(gr)).astype(BF16), ks[jh, d], preferred_element_type=F32)
                  for tb, gr, (jh, vh, d) in zip(tbs, g_r, chains)]
            for ci, (jh, vh, d) in enumerate(chains):
                u_scr[slot, ci] = us[ci]
                wq_scr[slot, ci, 0:CHUNK, :] = ws[ci].astype(BF16)
                kg = ks[jh, d].astype(F32) * jnp.exp(g_last[ci] - g_c[ci])
                ak_scr[slot, ci, CHUNK:CHUNK + HEAD_DIM, :] = kg.T.astype(BF16)
                dl_scr[slot, ci] = jnp.broadcast_to(jnp.exp(g_last[ci]), (1, HEAD_DIM))
                if write_out:
                    wq_scr[slot, ci, CHUNK:2 * CHUNK, :] = (qs[jh, d].astype(F32) * jnp.exp(g_c[ci])).astype(BF16)
                    ak_scr[slot, ci, 0:CHUNK, :] = (qks[jh, d] * dec[ci]).astype(BF16)

        def advance(i, slot):
            r0s = [chunk_of(i, d)[1] for d in range(2)]
            s_old = [s_scr[ci] for ci in range(N_CHAINS)]
            sbs = [s.astype(BF16) for s in s_old]
            rows = slice(0, 2 * CHUNK if write_out else CHUNK)
            r1 = [jnp.dot(wq_scr[slot, ci, rows, :], sbs[ci], preferred_element_type=F32) for ci in range(N_CHAINS)]
            vns = [(u_scr[slot, ci] - r1[ci][0:CHUNK]).astype(BF16) for ci in range(N_CHAINS)]
            rows = slice(0 if write_out else CHUNK, CHUNK + HEAD_DIM)
            r2 = [jnp.dot(ak_scr[slot, ci, rows, :], vns[ci], preferred_element_type=F32) for ci in range(N_CHAINS)]
            for ci, (jh, vh, d) in enumerate(chains):
                if write_out:
                    o_scr[pl.ds(r0s[d], CHUNK), value_cols(jh, vh)] += r1[ci][CHUNK:2 * CHUNK] + r2[ci][0:CHUNK]
                s_scr[ci] = s_old[ci] * dl_scr[slot, ci] + r2[ci][-HEAD_DIM:]

        prepare(0, 0)

        def pair_of_steps(m, carry):
            i = 2 * m
            advance(i, 0)
            prepare(i + 1, 1)
            advance(i + 1, 1)
            prepare(jnp.minimum(i + 2, n_chunks - 1), 0)
            return carry

        lax.fori_loop(0, n_chunks // 2, pair_of_steps, 0)

    run(qc_ref, kc_ref, vc_ref, gcc_ref, grc_ref, False)
    run(qx_ref, kx_ref, vx_ref, gcx_ref, grx_ref, True)
    o_ref[0] = o_scr[...].astype(o_ref.dtype)


def _gate_rows(gcol):
    b, t, n = gcol.shape
    g = gcol.reshape(b, t // CHUNK, CHUNK, n // GATE_LANES, GATE_LANES)
    return g.transpose(0, 3, 1, 4, 2)


def _dn_chunk(qc, kc, vc, gcc, qx, kx, vx, gcx):
    b, s, _ = qx.shape
    l = qc.shape[1]
    grc, grx = _gate_rows(gcc), _gate_rows(gcx)
    qk_cols = HEADS_PER_CALL * HEAD_DIM
    v_cols = 2 * qk_cols

    def seq_specs(t):
        return [pl.BlockSpec((1, t, qk_cols), lambda bi, j: (bi, 0, j)),
                pl.BlockSpec((1, t, qk_cols), lambda bi, j: (bi, 0, j)),
                pl.BlockSpec((1, t, v_cols), lambda bi, j: (bi, 0, j)),
                pl.BlockSpec((1, t, LANES), lambda bi, j: (bi, 0, 0)),
                pl.BlockSpec((1, HEADS_PER_CALL, t // CHUNK, GATE_LANES, CHUNK), lambda bi, j: (bi, j, 0, 0, 0))]

    return pl.pallas_call(
        _chunk_body,
        grid=(b, DN_QK_HEADS // HEADS_PER_CALL),
        in_specs=seq_specs(l) + seq_specs(s),
        out_specs=pl.BlockSpec((1, s, v_cols), lambda bi, j: (bi, 0, j)),
        out_shape=jax.ShapeDtypeStruct((b, s, DN_V_HEADS * HEAD_DIM), BF16),
        scratch_shapes=[pltpu.VMEM((N_CHAINS, HEAD_DIM, HEAD_DIM), F32),
                        pltpu.VMEM((s, v_cols), F32),
                        pltpu.VMEM((2, N_CHAINS, CHUNK, HEAD_DIM), F32),
                        pltpu.VMEM((2, N_CHAINS, 2 * CHUNK, HEAD_DIM), BF16),
                        pltpu.VMEM((2, N_CHAINS, CHUNK + HEAD_DIM, CHUNK), BF16),
                        pltpu.VMEM((2, N_CHAINS, 1, HEAD_DIM), F32)],
        compiler_params=_cparams("parallel", "parallel"),
        name="dn_chunk",
    )(qc, kc, vc, gcc, grc, qx, kx, vx, gcx, grx)


def _rot_cols(w):
    q = MLA_ROPE_DIM // 4
    return jnp.concatenate([-w[..., q:2 * q], w[..., 0:q], -w[..., 3 * q:4 * q], w[..., 2 * q:3 * q]], axis=-1)


def _rope_cos_sin(t_len):
    q = MLA_ROPE_DIM // 4
    pos = np.arange(t_len)
    inv = ROPE_BASE ** (-np.arange(q, dtype=np.float32) / q)
    ang_r = (pos // GRID_W).astype(np.float32)[:, None] * inv[None, :]
    ang_c = (pos % GRID_W).astype(np.float32)[:, None] * inv[None, :]
    ang = jnp.asarray(np.concatenate([ang_r, ang_r, ang_c, ang_c], axis=1))
    return jnp.cos(ang), jnp.sin(ang)


def _q_tables(cos, sin):
    t = cos.shape[0]
    one, zero = jnp.ones((t, HEAD_DIM), F32), jnp.zeros((t, MLA_ROPE_DIM), F32)
    c1 = jnp.concatenate([one, cos, zero], axis=1) * MLA_SCALE
    c2 = jnp.concatenate([jnp.zeros((t, HEAD_DIM), F32), sin, zero], axis=1) * MLA_SCALE
    return c1, c2


def _k_tables(cos, sin):
    zero = jnp.zeros_like(cos)
    return jnp.concatenate([cos, zero], axis=1), jnp.concatenate([sin, zero], axis=1)


def _mod_params(c, c_ctx, w_mod, b_mod):
    b, d = c.shape
    rows = ((b + 1 + 7) // 8) * 8
    cond = jnp.concatenate([c, c_ctx[None], jnp.zeros((rows - b - 1, d), F32)], axis=0)
    mod = _adaln(cond, w_mod, b_mod)
    sh, sc, gt = (mod[:b, i * d:(i + 1) * d].reshape(b, 1, d) for i in range(3))
    sh_c, sc_c, gt_c = (jnp.broadcast_to(mod[b, i * d:(i + 1) * d].reshape(1, 1, d), (b, 1, d)) for i in range(3))
    return (sh, sc, gt), (sh_c, sc_c, gt_c)


def _na_mla_layer(x, ctx, c, c_ctx, w_mod, b_mod, norm, w_in, rpb, q_norm, w_qb, kv_norm, w_kvb, w_out):
    b, s, d = x.shape
    l = ctx.shape[1]
    (sh, sc, gt), (sh_c, sc_c, gt_c) = _mod_params(c, c_ctx, w_mod, b_mod)
    na_w = NA_HEADS * HEAD_DIM
    n_main = 3 * na_w + 2 * MLA_RANK
    kr_w = w_in[:, n_main:n_main + MLA_ROPE_DIM]
    w_main = jnp.concatenate([w_in[:, :n_main], w_in[:, n_main + MLA_ROPE_DIM:]], axis=1).astype(BF16)
    w_small = jnp.concatenate([kr_w, _rot_cols(kr_w)], axis=1).astype(BF16)
    cos, sin = _rope_cos_sin(s)
    ident = (jnp.ones((l, MLA_ROPE_DIM), F32), jnp.zeros((l, MLA_ROPE_DIM), F32))
    hx, krx = _norm_proj(x, 0, d, norm, w_main, tm=512, tn=1024, shift=sh, scale=sc, w_small=w_small,
                         rope_small=_k_tables(cos, sin), small_dtype=BF16, name="in_proj_ab_x")
    hc, krc = _norm_proj(ctx, 0, d, norm, w_main, tm=512, tn=1024, shift=sh_c, scale=sc_c, w_small=w_small,
                         rope_small=_k_tables(*ident), small_dtype=BF16, name="in_proj_ab_ctx")

    wq = w_qb.reshape(MLA_RANK, MLA_HEADS, HEAD_DIM + MLA_ROPE_DIM)
    wq_r = wq[..., HEAD_DIM:]
    wq_p = jnp.concatenate([wq[..., :HEAD_DIM], wq_r, _rot_cols(wq_r)], axis=-1)
    wq_p = wq_p.reshape(MLA_RANK, MLA_HEADS * MLA_QK_PAD).astype(BF16)
    wkv = w_kvb.reshape(MLA_RANK, MLA_HEADS, 2 * HEAD_DIM)
    wkv_p = jnp.concatenate([wkv[..., :HEAD_DIM].reshape(MLA_RANK, -1), wkv[..., HEAD_DIM:].reshape(MLA_RANK, -1)],
                            axis=1).astype(BF16)
    cq_blk = 3 * na_w // MLA_RANK
    nq = MLA_HEADS * MLA_QK_PAD
    qx = _norm_proj(hx, cq_blk, MLA_RANK, q_norm, wq_p, tm=1024, tn=nq, rope_main=_q_tables(cos, sin), name="mla_q_x")
    qc = _norm_proj(hc, cq_blk, MLA_RANK, q_norm, wq_p, tm=1024, tn=nq, rope_main=_q_tables(*ident), name="mla_q_ctx")
    kvx = _norm_proj(hx, cq_blk + 1, MLA_RANK, kv_norm, wkv_p, tm=1024, tn=nq, name="mla_kv_x")
    kvc = _norm_proj(hc, cq_blk + 1, MLA_RANK, kv_norm, wkv_p, tm=1024, tn=nq, name="mla_kv_ctx")

    o_na = _na_attention(hx, hc, _na_bias_table(rpb, s // GRID_W))
    seg_x = (kvx, 0, krx, kvx, MLA_HEADS)
    seg_c = (kvc, 0, krc, kvc, MLA_HEADS)
    o_mla = _attention(qx, 0, MLA_QK_PAD, [seg_x, seg_c], MLA_HEADS, tq=512, scale=1.0, name="mla_attention")
    z_blk0 = n_main // na_w
    w_out_b = w_out.astype(BF16)
    x_new = _out_proj_ab(x, gt, o_na, o_mla, hx, z_blk0, w_out_b, tm=512)

    o_na_c = _attention(hc, 0, HEAD_DIM, [(hc, NA_HEADS, None, hc, 2 * NA_HEADS)], NA_HEADS, tq=l,
                        scale=HEAD_DIM ** -0.5, name="na_ctx_attention")
    o_mla_c = _attention(qc, 0, MLA_QK_PAD, [seg_c], MLA_HEADS, tq=l, scale=1.0, name="mla_ctx_attention")
    ctx_new = _out_proj_ab(ctx, gt_c, o_na_c, o_mla_c, hc, z_blk0, w_out_b, tm=512)
    return x_new, ctx_new


def _deltanet_final_layer(x, ctx, c, c_ctx, w_mod, b_mod, norm, w_in, conv_w, a_log, dt_bias, o_norm, w_out, final_norm):
    b, s, d = x.shape
    (sh, sc, gt), (sh_c, sc_c, _) = _mod_params(c, c_ctx, w_mod, b_mod)
    qk_w = DN_QK_HEADS * HEAD_DIM
    v_w = DN_V_HEADS * HEAD_DIM
    n_main = 2 * qk_w + 2 * v_w
    lane = np.arange(4 * DN_V_HEADS)
    perm = (lane % 4) * DN_V_HEADS + lane // 4
    w_main = w_in[:, :n_main].astype(BF16)
    w_small = w_in[:, n_main:][:, perm].astype(BF16)
    zeros = jnp.zeros((2 * DN_V_HEADS,), F32)
    alog = jnp.concatenate([zeros, a_log.reshape(-1).astype(F32)])[perm].reshape(1, -1)
    dtb = jnp.concatenate([zeros, dt_bias.reshape(-1).astype(F32)])[perm].reshape(1, -1)
    conv_w = conv_w.astype(F32)

    def inputs(t, shift, scale, name):
        h, hs = _norm_proj(t, 0, d, norm, w_main, tm=512, tn=1024, shift=shift, scale=scale, w_small=w_small, name=name)
        q = _dn_conv(h, conv_w, 0, qk_w, HEAD_DIM ** -0.5, tt=512)
        k = _dn_conv(h, conv_w, qk_w, qk_w, 1.0, tt=512)
        v = _dn_conv(h, conv_w, 2 * qk_w, v_w, None, tt=512)
        return h, q, k, v, _dn_gates(hs, alog, dtb)

    hx, qx, kx, vx, gx = inputs(x, sh, sc, "in_proj_dn_x")
    _, qc, kc, vc, gc = inputs(ctx, sh_c, sc_c, "in_proj_dn_ctx")
    o = _dn_chunk(qc, kc, vc, gc, qx, kx, vx, gx)
    return _out_proj_dn(x, gt, o, hx, (2 * qk_w + v_w) // v_w, o_norm, w_out.astype(BF16), final_norm, tm=256)


def kernel(x, c, ctx, c_ctx, ab_w_mod, ab_b_mod, ab_norm, ab_w_in, ab_rpb, ab_q_norm, ab_w_qb, ab_kv_norm, ab_w_kvb, ab_w_out, dn_w_mod, dn_b_mod, dn_norm, dn_w_in, dn_conv, dn_a_log, dn_dt_bias, dn_o_norm, dn_w_out, final_norm):
    assert ab_w_mod.shape[0] == 1 and dn_w_mod.shape[0] == 1, "depth-2 trunk: one even and one odd layer"
    x, ctx = _na_mla_layer(x, ctx, c, c_ctx, ab_w_mod[0], ab_b_mod[0], ab_norm[0], ab_w_in[0], ab_rpb[0],
                           ab_q_norm[0], ab_w_qb[0], ab_kv_norm[0], ab_w_kvb[0], ab_w_out[0])
    return _deltanet_final_layer(x, ctx, c, c_ctx, dn_w_mod[0], dn_b_mod[0], dn_norm[0], dn_w_in[0], dn_conv[0],
                                 dn_a_log[0], dn_dt_bias[0], dn_o_norm[0], dn_w_out[0], final_norm)
```

```python
import functools

import numpy as np
import jax
import jax.numpy as jnp
from jax import lax
from jax.experimental import pallas as pl
from jax.experimental.pallas import tpu as pltpu

F32 = jnp.float32
BF16 = jnp.bfloat16

EPS = 1e-6
GRID_W = 64
WIN_R = 8
WIN_C = 16
NA_HEADS = 8
HEAD_DIM = 128
MLA_HEADS = 8
MLA_RANK = 512
MLA_ROPE_DIM = 64
MLA_SCALE = (HEAD_DIM + MLA_ROPE_DIM) ** -0.5
ROPE_BASE = 10000.0
DN_QK_HEADS = 16
DN_V_HEADS = 32
CONV_K = 5
CHUNK = 64
NEG_BIG = -1e30

LANES = 128
MLA_QK_PAD = 256
NA_Q_ROWS = 8
NA_KEY_ROWS = 16
ATTN_SUB_ROWS = 256
IN_PROJ_ROWS = 1024
VMEM_LIMIT = 56 * 1024 * 1024


def _cparams(*sem):
    return pltpu.CompilerParams(dimension_semantics=sem, vmem_limit_bytes=VMEM_LIMIT)


def _adaln_body(c_ref, w_ref, b_ref, o_ref):
    cf = c_ref[...]
    a = (cf * jax.nn.sigmoid(cf)).astype(BF16)
    o_ref[...] = jnp.dot(a, w_ref[...].astype(BF16), preferred_element_type=F32) + b_ref[...]


def _adaln(cond, w_mod, b_mod):
    r, d = cond.shape
    n = w_mod.shape[1]
    tn = 512
    return pl.pallas_call(
        _adaln_body,
        grid=(n // tn,),
        in_specs=[pl.BlockSpec((r, d), lambda j: (0, 0)),
                  pl.BlockSpec((d, tn), lambda j: (0, j)),
                  pl.BlockSpec((1, tn), lambda j: (0, j))],
        out_specs=pl.BlockSpec((r, tn), lambda j: (0, j)),
        out_shape=jax.ShapeDtypeStruct((r, n), F32),
        compiler_params=_cparams("parallel"),
        name="adaln",
    )(cond, w_mod, b_mod.reshape(1, n))


def _proj_body(*refs, modulate, has_small, rope_main, rope_small):
    it = iter(refs)
    x_ref, gain_ref = next(it), next(it)
    shift_ref = scale_ref = ws_ref = c1_ref = c2_ref = s1_ref = s2_ref = os_ref = None
    if modulate:
        shift_ref, scale_ref = next(it), next(it)
    w_ref = next(it)
    if has_small:
        ws_ref = next(it)
    if rope_main:
        c1_ref, c2_ref = next(it), next(it)
    if rope_small:
        s1_ref, s2_ref = next(it), next(it)
    o_ref = next(it)
    if has_small:
        os_ref = next(it)
    xm_ref = next(it)

    @pl.when(pl.program_id(2) == 0)
    def _():
        xf = x_ref[0].astype(F32)
        ms = jnp.mean(xf * xf, axis=-1, keepdims=True)
        y = xf * lax.rsqrt(ms + EPS) * gain_ref[...]
        if modulate:
            y = y * (1.0 + scale_ref[0]) + shift_ref[0]
        xm = y.astype(BF16)
        xm_ref[...] = xm
        if has_small:
            s = jnp.dot(xm, ws_ref[...], preferred_element_type=F32)
            if rope_small:
                s = s * s1_ref[...] + pltpu.roll(s, LANES // 2, axis=1) * s2_ref[...]
            os_ref[0] = s.astype(os_ref.dtype)

    acc = jnp.dot(xm_ref[...], w_ref[...], preferred_element_type=F32)
    if rope_main:
        for h in range(acc.shape[1] // MLA_QK_PAD):
            a = acc[:, h * MLA_QK_PAD:(h + 1) * MLA_QK_PAD]
            a = a * c1_ref[...] + pltpu.roll(a, MLA_QK_PAD - MLA_ROPE_DIM, axis=1) * c2_ref[...]
            o_ref[0, :, h * MLA_QK_PAD:(h + 1) * MLA_QK_PAD] = a.astype(o_ref.dtype)
    else:
        o_ref[0] = acc.astype(o_ref.dtype)


def _norm_proj(x, kblk, k, gain, w, *, tm, tn, shift=None, scale=None, w_small=None,
               rope_main=None, rope_small=None, small_dtype=F32, name="proj"):
    b, t, _ = x.shape
    n = w.shape[1]
    modulate = shift is not None
    has_small = w_small is not None
    tm = min(tm, t)
    args = [x, gain.reshape(1, k).astype(F32)]
    in_specs = [pl.BlockSpec((1, tm, k), lambda bi, i, j: (bi, i, kblk)),
                pl.BlockSpec((1, k), lambda bi, i, j: (0, 0))]
    if modulate:
        args += [shift, scale]
        in_specs += [pl.BlockSpec((1, 1, k), lambda bi, i, j: (bi, 0, 0))] * 2
    args.append(w)
    in_specs.append(pl.BlockSpec((k, tn), lambda bi, i, j: (0, j)))
    if has_small:
        ns = w_small.shape[1]
        args.append(w_small)
        in_specs.append(pl.BlockSpec((k, ns), lambda bi, i, j: (0, 0)))
    if rope_main is not None:
        args += list(rope_main)
        in_specs += [pl.BlockSpec((tm, MLA_QK_PAD), lambda bi, i, j: (i, 0))] * 2
    if rope_small is not None:
        args += list(rope_small)
        in_specs += [pl.BlockSpec((tm, LANES), lambda bi, i, j: (i, 0))] * 2
    out_shape = [jax.ShapeDtypeStruct((b, t, n), BF16)]
    out_specs = [pl.BlockSpec((1, tm, tn), lambda bi, i, j: (bi, i, j))]
    if has_small:
        out_shape.append(jax.ShapeDtypeStruct((b, t, ns), small_dtype))
        out_specs.append(pl.BlockSpec((1, tm, ns), lambda bi, i, j: (bi, i, 0)))
    body = functools.partial(_proj_body, modulate=modulate, has_small=has_small,
                             rope_main=rope_main is not None, rope_small=rope_small is not None)
    out = pl.pallas_call(
        body,
        grid=(b, t // tm, n // tn),
        in_specs=in_specs,
        out_specs=out_specs,
        out_shape=out_shape,
        scratch_shapes=[pltpu.VMEM((tm, k), BF16)],
        compiler_params=_cparams("parallel", "parallel", "arbitrary"),
        name=name,
    )(*args)
    return out if has_small else out[0]


def _attn_body(*refs, seg_lens, has_extra, scale):
    it = iter(refs)
    q_ref = next(it)
    segs = []
    for _ in seg_lens:
        km = next(it)
        ke = next(it) if has_extra else None
        segs.append((km, ke, next(it)))
    o_ref, k_scr, v_scr = next(it), next(it), next(it)

    @pl.when(pl.program_id(2) == 0)
    def _():
        off = 0
        for (km, ke, v), ln in zip(segs, seg_lens):
            k_scr[off:off + ln, 0:HEAD_DIM] = km[0]
            if has_extra:
                k_scr[off:off + ln, HEAD_DIM:2 * HEAD_DIM] = ke[0]
            v_scr[off:off + ln, :] = v[0]
            off += ln

    sub = min(ATTN_SUB_ROWS, q_ref.shape[1])
    n_sub = q_ref.shape[1] // sub

    def scores(j):
        s = lax.dot_general(q_ref[0, j * sub:(j + 1) * sub, :], k_scr[...], (((1,), (1,)), ((), ())),
                            preferred_element_type=F32)
        return s if scale == 1.0 else s * scale

    s_next = scores(0)
    for j in range(n_sub):
        s = s_next
        if j + 1 < n_sub:
            s_next = scores(j + 1)
        m = jnp.max(s, axis=-1, keepdims=True)
        e = jnp.exp(s - m)
        l = jnp.sum(e, axis=-1, keepdims=True)
        o = jnp.dot(e.astype(BF16), v_scr[...], preferred_element_type=F32)
        o_ref[0, j * sub:(j + 1) * sub, :] = (o / l).astype(o_ref.dtype)


def _attention(q, q_blk0, dq, segs, heads, *, tq, scale, name):
    b, t, _ = q.shape
    tq = min(tq, t)
    has_extra = segs[0][2] is not None
    seg_lens = tuple(s[0].shape[1] for s in segs)
    tk = sum(seg_lens)
    args = [q]
    in_specs = [pl.BlockSpec((1, tq, dq), lambda bi, h, i: (bi, i, q_blk0 + h))]
    for km, kb0, ke, v, vb0 in segs:
        ln = km.shape[1]
        args.append(km)
        in_specs.append(pl.BlockSpec((1, ln, HEAD_DIM), lambda bi, h, i, kb0=kb0: (bi, 0, kb0 + h)))
        if has_extra:
            args.append(ke)
            in_specs.append(pl.BlockSpec((1, ln, HEAD_DIM), lambda bi, h, i: (bi, 0, 0)))
        args.append(v)
        in_specs.append(pl.BlockSpec((1, ln, HEAD_DIM), lambda bi, h, i, vb0=vb0: (bi, 0, vb0 + h)))
    body = functools.partial(_attn_body, seg_lens=seg_lens, has_extra=has_extra, scale=scale)
    return pl.pallas_call(
        body,
        grid=(b, heads, t // tq),
        in_specs=in_specs,
        out_specs=pl.BlockSpec((1, tq, HEAD_DIM), lambda bi, h, i: (bi, i, h)),
        out_shape=jax.ShapeDtypeStruct((b, t, heads * HEAD_DIM), BF16),
        scratch_shapes=[pltpu.VMEM((tk, dq), BF16), pltpu.VMEM((tk, HEAD_DIM), BF16)],
        compiler_params=_cparams("parallel", "parallel", "arbitrary"),
        name=name,
    )(*args)


def _na_geometry(rows):
    groups = rows // NA_Q_ROWS
    key_rows = min(NA_KEY_ROWS, rows)
    return groups, key_rows


def _na_bias_table(rpb, rows):
    heads = rpb.shape[0]
    groups, key_rows = _na_geometry(rows)
    g = np.arange(groups)
    slab0 = np.clip(NA_Q_ROWS * g - WIN_R // 2, 0, rows - key_rows)
    r = NA_Q_ROWS * g[:, None] + np.arange(NA_Q_ROWS)[None]
    k_abs = slab0[:, None] + np.arange(key_rows)[None]
    win0 = np.clip(r - WIN_R // 2, 0, rows - WIN_R)
    d_row = k_abs[:, None, :] - r[:, :, None] + (WIN_R - 1)
    ok_row = (k_abs[:, None, :] >= win0[:, :, None]) & (k_abs[:, None, :] < win0[:, :, None] + WIN_R)
    col = np.arange(GRID_W)
    c0 = np.clip(col - WIN_C // 2, 0, GRID_W - WIN_C)
    d_col = col[None, :] - col[:, None] + (WIN_C - 1)
    ok_col = (col[None, :] >= c0[:, None]) & (col[None, :] < c0[:, None] + WIN_C)
    t = rpb[:, np.clip(d_row, 0, 2 * WIN_R - 2)]
    t = t[..., np.clip(d_col, 0, 2 * WIN_C - 2)]
    ok = ok_row[:, :, :, None, None] & ok_col[None, None, None, :, :]
    t = jnp.where(jnp.asarray(ok)[None], t.astype(F32), NEG_BIG)
    t = t.transpose(0, 1, 2, 4, 3, 5)
    return t.reshape(heads, groups, NA_Q_ROWS * GRID_W, key_rows * GRID_W)


def _na_body(q_ref, k_ref, v_ref, kc_ref, vc_ref, bias_ref, o_ref, *, rows, scale):
    groups, key_rows = _na_geometry(rows)
    g = pl.program_id(2)
    slab0 = jnp.clip(NA_Q_ROWS * g - WIN_R // 2, 0, rows - key_rows)
    start = pl.multiple_of(slab0 * GRID_W, 4 * GRID_W)
    nk = key_rows * GRID_W
    nt = (((1,), (1,)), ((), ()))
    sub = min(ATTN_SUB_ROWS, q_ref.shape[1])
    n_sub = q_ref.shape[1] // sub

    def scores(j):
        q = q_ref[0, j * sub:(j + 1) * sub, :]
        s_nb = lax.dot_general(q, k_ref[0, pl.ds(start, nk), :], nt, preferred_element_type=F32)
        s_cx = lax.dot_general(q, kc_ref[0], nt, preferred_element_type=F32)
        return s_nb, s_cx

    s_next = scores(0)
    for j in range(n_sub):
        s_nb, s_cx = s_next
        if j + 1 < n_sub:
            s_next = scores(j + 1)
        s_nb = s_nb * scale + bias_ref[0, g, j * sub:(j + 1) * sub, :]
        s_cx = s_cx * scale
        m = jnp.maximum(jnp.max(s_nb, axis=-1, keepdims=True), jnp.max(s_cx, axis=-1, keepdims=True))
        e_nb = jnp.exp(s_nb - m)
        e_cx = jnp.exp(s_cx - m)
        l = jnp.sum(e_nb, axis=-1, keepdims=True) + jnp.sum(e_cx, axis=-1, keepdims=True)
        o = jnp.dot(e_nb.astype(BF16), v_ref[0, pl.ds(start, nk), :], preferred_element_type=F32)
        o = o + jnp.dot(e_cx.astype(BF16), vc_ref[0], preferred_element_type=F32)
        o_ref[0, j * sub:(j + 1) * sub, :] = (o / l).astype(o_ref.dtype)


def _na_attention(hx, hc, bias):
    b, s, _ = hx.shape
    l = hc.shape[1]
    rows = s // GRID_W
    groups, key_rows = _na_geometry(rows)
    tq = NA_Q_ROWS * GRID_W
    body = functools.partial(_na_body, rows=rows, scale=HEAD_DIM ** -0.5)
    return pl.pallas_call(
        body,
        grid=(NA_HEADS, b, groups),
        in_specs=[pl.BlockSpec((1, tq, HEAD_DIM), lambda h, bi, g: (bi, g, h)),
                  pl.BlockSpec((1, s, HEAD_DIM), lambda h, bi, g: (bi, 0, NA_HEADS + h)),
                  pl.BlockSpec((1, s, HEAD_DIM), lambda h, bi, g: (bi, 0, 2 * NA_HEADS + h)),
                  pl.BlockSpec((1, l, HEAD_DIM), lambda h, bi, g: (bi, 0, NA_HEADS + h)),
                  pl.BlockSpec((1, l, HEAD_DIM), lambda h, bi, g: (bi, 0, 2 * NA_HEADS + h)),
                  pl.BlockSpec((1, groups, tq, key_rows * GRID_W), lambda h, bi, g: (h, 0, 0, 0))],
        out_specs=pl.BlockSpec((1, tq, HEAD_DIM), lambda h, bi, g: (bi, g, h)),
        out_shape=jax.ShapeDtypeStruct((b, s, NA_HEADS * HEAD_DIM), BF16),
        compiler_params=_cparams("parallel", "parallel", "parallel"),
        name="na_attention",
    )(hx, hx, hx, hc, hc, bias)


def _silu(x):
    return x * jax.nn.sigmoid(x)


def _out_ab_body(x_ref, gt_ref, o1_ref, o2_ref, z1_ref, z2_ref, w_ref, y_ref):
    half = o1_ref.shape[2]
    a1 = (o1_ref[0].astype(F32) * _silu(z1_ref[0].astype(F32))).astype(BF16)
    a2 = (o2_ref[0].astype(F32) * _silu(z2_ref[0].astype(F32))).astype(BF16)
    acc = jnp.dot(a1, w_ref[0:half, :], preferred_element_type=F32)
    acc = acc + jnp.dot(a2, w_ref[half:2 * half, :], preferred_element_type=F32)
    y_ref[0] = x_ref[0] + gt_ref[0] * acc


def _out_proj_ab(x, gt, o1, o2, h, z_blk0, w, *, tm):
    b, t, d = x.shape
    half = o1.shape[2]
    tm = min(tm, t)
    return pl.pallas_call(
        _out_ab_body,
        grid=(b, t // tm),
        in_specs=[pl.BlockSpec((1, tm, d), lambda bi, i: (bi, i, 0)),
                  pl.BlockSpec((1, 1, d), lambda bi, i: (bi, 0, 0)),
                  pl.BlockSpec((1, tm, half), lambda bi, i: (bi, i, 0)),
                  pl.BlockSpec((1, tm, half), lambda bi, i: (bi, i, 0)),
                  pl.BlockSpec((1, tm, half), lambda bi, i: (bi, i, z_blk0)),
                  pl.BlockSpec((1, tm, half), lambda bi, i: (bi, i, z_blk0 + 1)),
                  pl.BlockSpec((2 * half, d), lambda bi, i: (0, 0))],
        out_specs=pl.BlockSpec((1, tm, d), lambda bi, i: (bi, i, 0)),
        out_shape=jax.ShapeDtypeStruct((b, t, d), F32),
        compiler_params=_cparams("parallel", "parallel"),
        name="out_proj_ab",
    )(x, gt, o1, o2, h, h, w)


def _out_dn_body(x_ref, gt_ref, o_ref, z_ref, on_ref, w_ref, fn_ref, y_ref, a_scr):
    for h in range(o_ref.shape[2] // HEAD_DIM):
        sl = slice(h * HEAD_DIM, (h + 1) * HEAD_DIM)
        oh = o_ref[0, :, sl].astype(F32)
        ms = jnp.mean(oh * oh, axis=-1, keepdims=True)
        a = oh * lax.rsqrt(ms + EPS) * on_ref[...] * _silu(z_ref[0, :, sl].astype(F32))
        a_scr[:, sl] = a.astype(BF16)
    acc = jnp.dot(a_scr[...], w_ref[...], preferred_element_type=F32)
    y = x_ref[0] + gt_ref[0] * acc
    ms = jnp.mean(y * y, axis=-1, keepdims=True)
    y_ref[0] = y * lax.rsqrt(ms + EPS) * fn_ref[...]


def _out_proj_dn(x, gt, o, h, z_blk, o_norm, w, final_norm, *, tm):
    b, t, d = x.shape
    km = o.shape[2]
    tm = min(tm, t)
    return pl.pallas_call(
        _out_dn_body,
        grid=(b, t // tm),
        in_specs=[pl.BlockSpec((1, tm, d), lambda bi, i: (bi, i, 0)),
                  pl.BlockSpec((1, 1, d), lambda bi, i: (bi, 0, 0)),
                  pl.BlockSpec((1, tm, km), lambda bi, i: (bi, i, 0)),
                  pl.BlockSpec((1, tm, km), lambda bi, i: (bi, i, z_blk)),
                  pl.BlockSpec((1, HEAD_DIM), lambda bi, i: (0, 0)),
                  pl.BlockSpec((km, d), lambda bi, i: (0, 0)),
                  pl.BlockSpec((1, d), lambda bi, i: (0, 0))],
        out_specs=pl.BlockSpec((1, tm, d), lambda bi, i: (bi, i, 0)),
        out_shape=jax.ShapeDtypeStruct((b, t, d), F32),
        scratch_shapes=[pltpu.VMEM((tm, km), BF16)],
        compiler_params=_cparams("parallel", "parallel"),
        name="out_proj_dn",
    )(x, gt, o, h, o_norm.reshape(1, HEAD_DIM).astype(F32), w, final_norm.reshape(1, d).astype(F32))


CONV_HALO = 16


def _conv_body(xp_ref, x_ref, xn_ref, w_ref, o_ref, pad_scr, *, l2_scale):
    i = pl.program_id(1)
    nt = pl.num_programs(1)
    tt = x_ref.shape[1]
    prev = xp_ref[0].astype(F32)
    nxt = xn_ref[0].astype(F32)
    pad_scr[0:CONV_HALO, :] = jnp.where(i > 0, prev, 0.0)
    pad_scr[CONV_HALO:CONV_HALO + tt, :] = x_ref[0].astype(F32)
    pad_scr[CONV_HALO + tt:2 * CONV_HALO + tt, :] = jnp.where(i < nt - 1, nxt, 0.0)
    acc = None
    for j in range(CONV_K):
        lo = CONV_HALO + j - CONV_K // 2
        term = pad_scr[lo:lo + tt, :] * w_ref[j:j + 1, :]
        acc = term if acc is None else acc + term
    y = _silu(acc)
    if l2_scale is None:
        o_ref[0] = y.astype(o_ref.dtype)
    else:
        for h in range(y.shape[1] // HEAD_DIM):
            yh = y[:, h * HEAD_DIM:(h + 1) * HEAD_DIM]
            ss = jnp.sum(yh * yh, axis=-1, keepdims=True)
            o_ref[0, :, h * HEAD_DIM:(h + 1) * HEAD_DIM] = (yh * lax.rsqrt(ss + EPS) * l2_scale).astype(o_ref.dtype)


def _dn_conv(h, conv_w, col0, width, l2_scale, *, tt, cb=512):
    b, t, _ = h.shape
    tt = min(tt, t)
    blk0 = col0 // cb
    hb = tt // CONV_HALO
    nhalo = t // CONV_HALO
    body = functools.partial(_conv_body, l2_scale=l2_scale)
    return pl.pallas_call(
        body,
        grid=(b, t // tt, width // cb),
        in_specs=[pl.BlockSpec((1, CONV_HALO, cb), lambda bi, i, c: (bi, jnp.maximum(i * hb - 1, 0), blk0 + c)),
                  pl.BlockSpec((1, tt, cb), lambda bi, i, c: (bi, i, blk0 + c)),
                  pl.BlockSpec((1, CONV_HALO, cb), lambda bi, i, c: (bi, jnp.minimum((i + 1) * hb, nhalo - 1), blk0 + c)),
                  pl.BlockSpec((CONV_K, cb), lambda bi, i, c: (0, blk0 + c))],
        out_specs=pl.BlockSpec((1, tt, cb), lambda bi, i, c: (bi, i, c)),
        out_shape=jax.ShapeDtypeStruct((b, t, width), BF16),
        scratch_shapes=[pltpu.VMEM((tt + 2 * CONV_HALO, cb), F32)],
        compiler_params=_cparams("parallel", "parallel", "parallel"),
        name="dn_conv",
    )(h, h, h, conv_w)


GATE_TILE = 256


def _split3(x):
    hi = x.astype(BF16)
    r1 = x - hi.astype(F32)
    mid = r1.astype(BF16)
    lo = (r1 - mid.astype(F32)).astype(BF16)
    return hi, mid, lo


def _gates_body(h_ref, alog_ref, dtb_ref, o_ref):
    t = h_ref[0]
    lane = lax.broadcasted_iota(jnp.int32, t.shape, 1) % 4
    beta = jax.nn.sigmoid(t)
    z = t + dtb_ref[...]
    softplus = jnp.maximum(z, 0.0) + jnp.log1p(jnp.exp(-jnp.abs(z)))
    g = -jnp.exp(alog_ref[...]) * softplus
    n = t.shape[0]
    r = lax.broadcasted_iota(jnp.int32, (n, n), 0)
    c = lax.broadcasted_iota(jnp.int32, (n, n), 1)
    same = (r // CHUNK) == (c // CHUNK)
    tri_f = jnp.where(same & (r >= c), 1.0, 0.0).astype(BF16)
    tri_b = jnp.where(same & (r <= c), 1.0, 0.0).astype(BF16)
    pf = pb = None
    for part in _split3(g):
        df = jnp.dot(tri_f, part, preferred_element_type=F32)
        db = jnp.dot(tri_b, part, preferred_element_type=F32)
        pf = df if pf is None else pf + df
        pb = db if pb is None else pb + db
    o_ref[0] = jnp.where(lane < 2, beta, jnp.where(lane == 2, pf, pb))


def _dn_gates(h_small, alog, dtb):
    b, t, n = h_small.shape
    tt = min(GATE_TILE, t)
    return pl.pallas_call(
        _gates_body,
        grid=(b, t // tt),
        in_specs=[pl.BlockSpec((1, tt, n), lambda bi, i: (bi, i, 0)),
                  pl.BlockSpec((1, n), lambda bi, i: (0, 0)),
                  pl.BlockSpec((1, n), lambda bi, i: (0, 0))],
        out_specs=pl.BlockSpec((1, tt, n), lambda bi, i: (bi, i, 0)),
        out_shape=jax.ShapeDtypeStruct((b, t, n), F32),
        compiler_params=_cparams("parallel", "parallel"),
        name="dn_gates",
    )(h_small, alog, dtb)


HEADS_PER_CALL = 4
N_CHAINS = 4 * HEADS_PER_CALL
GATE_LANES = 8
MERGE_LEVELS = (4, 8, 16, 32)


def _bdot(a, b):
    return jnp.dot(a.astype(BF16), b.astype(BF16), preferred_element_type=F32)


def _unit_tri_inverses(lms, eye, level):
    l4s = [jnp.where(level == 0, lm, 0.0) for lm in lms]
    ts = [eye - l4 for l4 in l4s]
    sq = [_bdot(l4, l4) for l4 in l4s]
    ts = [t + _bdot(t, p) for t, p in zip(ts, sq)]
    for m in range(1, len(MERGE_LEVELS) + 1):
        cs = [jnp.where(level == m, lm, 0.0).astype(BF16) for lm in lms]
        tbs = [t.astype(BF16) for t in ts]
        xs = [jnp.dot(tb, c, preferred_element_type=F32) for tb, c in zip(tbs, cs)]
        ts = [t - _bdot(x, tb) for t, x, tb in zip(ts, xs, tbs)]
    return ts


def _chunk_body(qc_ref, kc_ref, vc_ref, gcc_ref, grc_ref, qx_ref, kx_ref, vx_ref, gcx_ref, grx_ref,
                o_ref, s_scr, o_scr, u_scr, wq_scr, ak_scr, dl_scr):
    jg = pl.program_id(1)
    group_lanes = GATE_LANES * HEADS_PER_CALL
    lane_shift = (LANES - group_lanes * jg) % LANES
    s_scr[...] = jnp.zeros_like(s_scr)
    o_scr[...] = jnp.zeros_like(o_scr)
    row = lax.broadcasted_iota(jnp.int32, (CHUNK, CHUNK), 0)
    col = lax.broadcasted_iota(jnp.int32, (CHUNK, CHUNK), 1)
    eye = jnp.where(row == col, 1.0, 0.0)
    level = jnp.zeros((CHUNK, CHUNK), jnp.int32)
    for size in MERGE_LEVELS:
        level = level + jnp.where((row // size) == (col // size), 0, 1)
    incl = (row >= col, row <= col)
    strict = (row > col, row < col)
    nt = (((1,), (1,)), ((), ()))
    chains = [(jh, vh, d) for d in range(2) for jh in range(HEADS_PER_CALL) for vh in range(2)]
    streams = [(jh, d) for d in range(2) for jh in range(HEADS_PER_CALL)]

    def head_cols(jh):
        return slice(jh * HEAD_DIM, (jh + 1) * HEAD_DIM)

    def value_cols(jh, vh):
        return slice((2 * jh + vh) * HEAD_DIM, (2 * jh + vh + 1) * HEAD_DIM)

    def run(q_ref, k_ref, v_ref, gc_ref, gr_ref, write_out):
        n_chunks = q_ref.shape[1] // CHUNK

        def chunk_of(i, d):
            n = i if d == 0 else n_chunks - 1 - i
            return n, pl.multiple_of(n * CHUNK, CHUNK)

        def prepare(i, slot):
            ns, r0s, gcols = {}, {}, {}
            for d in range(2):
                ns[d], r0s[d] = chunk_of(i, d)
                gcols[d] = pltpu.roll(gc_ref[0, pl.ds(r0s[d], CHUNK), :], lane_shift, axis=1)
            ks = {(jh, d): k_ref[0, pl.ds(r0s[d], CHUNK), head_cols(jh)] for jh, d in streams}
            kks = {sd: lax.dot_general(ks[sd], ks[sd], nt, preferred_element_type=F32) for sd in streams}
            if write_out:
                qs = {(jh, d): q_ref[0, pl.ds(r0s[d], CHUNK), head_cols(jh)] for jh, d in streams}
                qks = {sd: lax.dot_general(qs[sd], ks[sd], nt, preferred_element_type=F32) for sd in streams}
            grows = {(jh, d): gr_ref[0, jh, ns[d]] for jh, d in streams}
            lane0 = [GATE_LANES * jh + 4 * vh + d for jh, vh, d in chains]
            beta_c = [gcols[d][:, l:l + 1] for l, (jh, vh, d) in zip(lane0, chains)]
            g_c = [gcols[d][:, l + 2:l + 3] for l, (jh, vh, d) in zip(lane0, chains)]
            beta_r = [grows[jh, d][4 * vh + d:4 * vh + d + 1, :] for jh, vh, d in chains]
            g_r = [grows[jh, d][4 * vh + 2 + d:4 * vh + 3 + d, :] for jh, vh, d in chains]
            g_last = [g[:, CHUNK - 1:CHUNK] if d == 0 else g[:, 0:1] for g, (jh, vh, d) in zip(g_r, chains)]
            dec = [jnp.where(incl[d], jnp.exp(jnp.where(incl[d], gc - gr, 0.0)), 0.0)
                   for gc, gr, (jh, vh, d) in zip(g_c, g_r, chains)]
            lms = [jnp.where(strict[d], kks[jh, d] * dc, 0.0) * bc
                   for dc, bc, (jh, vh, d) in zip(dec, beta_c, chains)]
            tbs = [t * br for t, br in zip(_unit_tri_inverses(lms, eye, level), beta_r)]
            us = [jnp.dot(tb.astype(BF16), v_ref[0, pl.ds(r0s[d], CHUNK), value_cols(jh, vh)],
                          preferred_element_type=F32) for tb, (jh, vh, d) in zip(tbs, chains)]
            ws = [jnp.dot((tb * jnp.exp(gr)).astype(BF16), ks[jh, d], preferred_element_type=F32)
                  for tb, gr, (jh, vh, d) in zip(tbs, g_r, chains)]
            for ci, (jh, vh, d) in enumerate(chains):
                u_scr[slot, ci] = us[ci]
                wq_scr[slot, ci, 0:CHUNK, :] = ws[ci].astype(BF16)
                kg = ks[jh, d].astype(F32) * jnp.exp(g_last[ci] - g_c[ci])
                ak_scr[slot, ci, CHUNK:CHUNK + HEAD_DIM, :] = kg.T.astype(BF16)
                dl_scr[slot, ci] = jnp.broadcast_to(jnp.exp(g_last[ci]), (1, HEAD_DIM))
                if write_out:
                    wq_scr[slot, ci, CHUNK:2 * CHUNK, :] = (qs[jh, d].astype(F32) * jnp.exp(g_c[ci])).astype(BF16)
                    ak_scr[slot, ci, 0:CHUNK, :] = (qks[jh, d] * dec[ci]).astype(BF16)

        def advance(i, slot):
            r0s = [chunk_of(i, d)[1] for d in range(2)]
            s_old = [s_scr[ci] for ci in range(N_CHAINS)]
            sbs = [s.astype(BF16) for s in s_old]
            rows = slice(0, 2 * CHUNK if write_out else CHUNK)
            r1 = [jnp.dot(wq_scr[slot, ci, rows, :], sbs[ci], preferred_element_type=F32) for ci in range(N_CHAINS)]
            vns = [(u_scr[slot, ci] - r1[ci][0:CHUNK]).astype(BF16) for ci in range(N_CHAINS)]
            rows = slice(0 if write_out else CHUNK, CHUNK + HEAD_DIM)
            r2 = [jnp.dot(ak_scr[slot, ci, rows, :], vns[ci], preferred_element_type=F32) for ci in range(N_CHAINS)]
            for ci, (jh, vh, d) in enumerate(chains):
                if write_out:
                    o_scr[pl.ds(r0s[d], CHUNK), value_cols(jh, vh)] += r1[ci][CHUNK:2 * CHUNK] + r2[ci][0:CHUNK]
                s_scr[ci] = s_old[ci] * dl_scr[slot, ci] + r2[ci][-HEAD_DIM:]

        prepare(0, 0)

        def pair_of_steps(m, carry):
            i = 2 * m
            advance(i, 0)
            prepare(i + 1, 1)
            advance(i + 1, 1)
            prepare(jnp.minimum(i + 2, n_chunks - 1), 0)
            return carry

        lax.fori_loop(0, n_chunks // 2, pair_of_steps, 0)

    run(qc_ref, kc_ref, vc_ref, gcc_ref, grc_ref, False)
    run(qx_ref, kx_ref, vx_ref, gcx_ref, grx_ref, True)
    o_ref[0] = o_scr[...].astype(o_ref.dtype)


def _gate_rows(gcol):
    b, t, n = gcol.shape
    g = gcol.reshape(b, t // CHUNK, CHUNK, n // GATE_LANES, GATE_LANES)
    return g.transpose(0, 3, 1, 4, 2)


def _dn_chunk(qc, kc, vc, gcc, qx, kx, vx, gcx):
    b, s, _ = qx.shape
    l = qc.shape[1]
    grc, grx = _gate_rows(gcc), _gate_rows(gcx)
    qk_cols = HEADS_PER_CALL * HEAD_DIM
    v_cols = 2 * qk_cols

    def seq_specs(t):
        return [pl.BlockSpec((1, t, qk_cols), lambda bi, j: (bi, 0, j)),
                pl.BlockSpec((1, t, qk_cols), lambda bi, j: (bi, 0, j)),
                pl.BlockSpec((1, t, v_cols), lambda bi, j: (bi, 0, j)),
                pl.BlockSpec((1, t, LANES), lambda bi, j: (bi, 0, 0)),
                pl.BlockSpec((1, HEADS_PER_CALL, t // CHUNK, GATE_LANES, CHUNK), lambda bi, j: (bi, j, 0, 0, 0))]

    return pl.pallas_call(
        _chunk_body,
        grid=(b, DN_QK_HEADS // HEADS_PER_CALL),
        in_specs=seq_specs(l) + seq_specs(s),
        out_specs=pl.BlockSpec((1, s, v_cols), lambda bi, j: (bi, 0, j)),
        out_shape=jax.ShapeDtypeStruct((b, s, DN_V_HEADS * HEAD_DIM), BF16),
        scratch_shapes=[pltpu.VMEM((N_CHAINS, HEAD_DIM, HEAD_DIM), F32),
                        pltpu.VMEM((s, v_cols), F32),
                        pltpu.VMEM((2, N_CHAINS, CHUNK, HEAD_DIM), F32),
                        pltpu.VMEM((2, N_CHAINS, 2 * CHUNK, HEAD_DIM), BF16),
                        pltpu.VMEM((2, N_CHAINS, CHUNK + HEAD_DIM, CHUNK), BF16),
                        pltpu.VMEM((2, N_CHAINS, 1, HEAD_DIM), F32)],
        compiler_params=_cparams("parallel", "parallel"),
        name="dn_chunk",
    )(qc, kc, vc, gcc, grc, qx, kx, vx, gcx, grx)


def _rot_cols(w):
    q = MLA_ROPE_DIM // 4
    return jnp.concatenate([-w[..., q:2 * q], w[..., 0:q], -w[..., 3 * q:4 * q], w[..., 2 * q:3 * q]], axis=-1)


def _rope_cos_sin(t_len):
    q = MLA_ROPE_DIM // 4
    pos = np.arange(t_len)
    inv = ROPE_BASE ** (-np.arange(q, dtype=np.float32) / q)
    ang_r = (pos // GRID_W).astype(np.float32)[:, None] * inv[None, :]
    ang_c = (pos % GRID_W).astype(np.float32)[:, None] * inv[None, :]
    ang = jnp.asarray(np.concatenate([ang_r, ang_r, ang_c, ang_c], axis=1))
    return jnp.cos(ang), jnp.sin(ang)


def _q_tables(cos, sin):
    t = cos.shape[0]
    one, zero = jnp.ones((t, HEAD_DIM), F32), jnp.zeros((t, MLA_ROPE_DIM), F32)
    c1 = jnp.concatenate([one, cos, zero], axis=1) * MLA_SCALE
    c2 = jnp.concatenate([jnp.zeros((t, HEAD_DIM), F32), sin, zero], axis=1) * MLA_SCALE
    return c1, c2


def _k_tables(cos, sin):
    zero = jnp.zeros_like(cos)
    return jnp.concatenate([cos, zero], axis=1), jnp.concatenate([sin, zero], axis=1)


def _mod_params(c, c_ctx, w_mod, b_mod):
    b, d = c.shape
    rows = ((b + 1 + 7) // 8) * 8
    cond = jnp.concatenate([c, c_ctx[None], jnp.zeros((rows - b - 1, d), F32)], axis=0)
    mod = _adaln(cond, w_mod, b_mod)
    sh, sc, gt = (mod[:b, i * d:(i + 1) * d].reshape(b, 1, d) for i in range(3))
    sh_c, sc_c, gt_c = (jnp.broadcast_to(mod[b, i * d:(i + 1) * d].reshape(1, 1, d), (b, 1, d)) for i in range(3))
    return (sh, sc, gt), (sh_c, sc_c, gt_c)


def _na_mla_layer(x, ctx, c, c_ctx, w_mod, b_mod, norm, w_in, rpb, q_norm, w_qb, kv_norm, w_kvb, w_out):
    b, s, d = x.shape
    l = ctx.shape[1]
    (sh, sc, gt), (sh_c, sc_c, gt_c) = _mod_params(c, c_ctx, w_mod, b_mod)
    na_w = NA_HEADS * HEAD_DIM
    n_main = 3 * na_w + 2 * MLA_RANK
    kr_w = w_in[:, n_main:n_main + MLA_ROPE_DIM]
    w_main = jnp.concatenate([w_in[:, :n_main], w_in[:, n_main + MLA_ROPE_DIM:]], axis=1).astype(BF16)
    w_small = jnp.concatenate([kr_w, _rot_cols(kr_w)], axis=1).astype(BF16)
    cos, sin = _rope_cos_sin(s)
    ident = (jnp.ones((l, MLA_ROPE_DIM), F32), jnp.zeros((l, MLA_ROPE_DIM), F32))
    hx, krx = _norm_proj(x, 0, d, norm, w_main, tm=IN_PROJ_ROWS, tn=1024, shift=sh, scale=sc, w_small=w_small,
                         rope_small=_k_tables(cos, sin), small_dtype=BF16, name="in_proj_ab_x")
    hc, krc = _norm_proj(ctx, 0, d, norm, w_main, tm=IN_PROJ_ROWS, tn=1024, shift=sh_c, scale=sc_c, w_small=w_small,
                         rope_small=_k_tables(*ident), small_dtype=BF16, name="in_proj_ab_ctx")

    wq = w_qb.reshape(MLA_RANK, MLA_HEADS, HEAD_DIM + MLA_ROPE_DIM)
    wq_r = wq[..., HEAD_DIM:]
    wq_p = jnp.concatenate([wq[..., :HEAD_DIM], wq_r, _rot_cols(wq_r)], axis=-1)
    wq_p = wq_p.reshape(MLA_RANK, MLA_HEADS * MLA_QK_PAD).astype(BF16)
    wkv = w_kvb.reshape(MLA_RANK, MLA_HEADS, 2 * HEAD_DIM)
    wkv_p = jnp.concatenate([wkv[..., :HEAD_DIM].reshape(MLA_RANK, -1), wkv[..., HEAD_DIM:].reshape(MLA_RANK, -1)],
                            axis=1).astype(BF16)
    cq_blk = 3 * na_w // MLA_RANK
    nq = MLA_HEADS * MLA_QK_PAD
    qx = _norm_proj(hx, cq_blk, MLA_RANK, q_norm, wq_p, tm=1024, tn=nq, rope_main=_q_tables(cos, sin), name="mla_q_x")
    qc = _norm_proj(hc, cq_blk, MLA_RANK, q_norm, wq_p, tm=1024, tn=nq, rope_main=_q_tables(*ident), name="mla_q_ctx")
    kvx = _norm_proj(hx, cq_blk + 1, MLA_RANK, kv_norm, wkv_p, tm=1024, tn=nq, name="mla_kv_x")
    kvc = _norm_proj(hc, cq_blk + 1, MLA_RANK, kv_norm, wkv_p, tm=1024, tn=nq, name="mla_kv_ctx")

    o_na = _na_attention(hx, hc, _na_bias_table(rpb, s // GRID_W))
    seg_x = (kvx, 0, krx, kvx, MLA_HEADS)
    seg_c = (kvc, 0, krc, kvc, MLA_HEADS)
    o_mla = _attention(qx, 0, MLA_QK_PAD, [seg_x, seg_c], MLA_HEADS, tq=512, scale=1.0, name="mla_attention")
    z_blk0 = n_main // na_w
    w_out_b = w_out.astype(BF16)
    x_new = _out_proj_ab(x, gt, o_na, o_mla, hx, z_blk0, w_out_b, tm=512)

    o_na_c = _attention(hc, 0, HEAD_DIM, [(hc, NA_HEADS, None, hc, 2 * NA_HEADS)], NA_HEADS, tq=l,
                        scale=HEAD_DIM ** -0.5, name="na_ctx_attention")
    o_mla_c = _attention(qc, 0, MLA_QK_PAD, [seg_c], MLA_HEADS, tq=l, scale=1.0, name="mla_ctx_attention")
    ctx_new = _out_proj_ab(ctx, gt_c, o_na_c, o_mla_c, hc, z_blk0, w_out_b, tm=512)
    return x_new, ctx_new


def _deltanet_final_layer(x, ctx, c, c_ctx, w_mod, b_mod, norm, w_in, conv_w, a_log, dt_bias, o_norm, w_out, final_norm):
    b, s, d = x.shape
    (sh, sc, gt), (sh_c, sc_c, _) = _mod_params(c, c_ctx, w_mod, b_mod)
    qk_w = DN_QK_HEADS * HEAD_DIM
    v_w = DN_V_HEADS * HEAD_DIM
    n_main = 2 * qk_w + 2 * v_w
    lane = np.arange(4 * DN_V_HEADS)
    perm = (lane % 4) * DN_V_HEADS + lane // 4
    w_main = w_in[:, :n_main].astype(BF16)
    w_small = w_in[:, n_main:][:, perm].astype(BF16)
    zeros = jnp.zeros((2 * DN_V_HEADS,), F32)
    alog = jnp.concatenate([zeros, a_log.reshape(-1).astype(F32)])[perm].reshape(1, -1)
    dtb = jnp.concatenate([zeros, dt_bias.reshape(-1).astype(F32)])[perm].reshape(1, -1)
    conv_w = conv_w.astype(F32)

    def inputs(t, shift, scale, name):
        h, hs = _norm_proj(t, 0, d, norm, w_main, tm=IN_PROJ_ROWS, tn=1024, shift=shift, scale=scale, w_small=w_small,
                           name=name)
        q = _dn_conv(h, conv_w, 0, qk_w, HEAD_DIM ** -0.5, tt=512)
        k = _dn_conv(h, conv_w, qk_w, qk_w, 1.0, tt=512)
        v = _dn_conv(h, conv_w, 2 * qk_w, v_w, None, tt=512)
        return h, q, k, v, _dn_gates(hs, alog, dtb)

    hx, qx, kx, vx, gx = inputs(x, sh, sc, "in_proj_dn_x")
    _, qc, kc, vc, gc = inputs(ctx, sh_c, sc_c, "in_proj_dn_ctx")
    o = _dn_chunk(qc, kc, vc, gc, qx, kx, vx, gx)
    return _out_proj_dn(x, gt, o, hx, (2 * qk_w + v_w) // v_w, o_norm, w_out.astype(BF16), final_norm, tm=256)


def kernel(x, c, ctx, c_ctx, ab_w_mod, ab_b_mod, ab_norm, ab_w_in, ab_rpb, ab_q_norm, ab_w_qb, ab_kv_norm, ab_w_kvb, ab_w_out, dn_w_mod, dn_b_mod, dn_norm, dn_w_in, dn_conv, dn_a_log, dn_dt_bias, dn_o_norm, dn_w_out, final_norm):
    assert ab_w_mod.shape[0] == 1 and dn_w_mod.shape[0] == 1, "depth-2 trunk: one even and one odd layer"
    x, ctx = _na_mla_layer(x, ctx, c, c_ctx, ab_w_mod[0], ab_b_mod[0], ab_norm[0], ab_w_in[0], ab_rpb[0],
                           ab_q_norm[0], ab_w_qb[0], ab_kv_norm[0], ab_w_kvb[0], ab_w_out[0])
    return _deltanet_final_layer(x, ctx, c, c_ctx, dn_w_mod[0], dn_b_mod[0], dn_norm[0], dn_w_in[0], dn_conv[0],
                                 dn_a_log[0], dn_dt_bias[0], dn_o_norm[0], dn_w_out[0], final_norm)
```

```python
import functools

import numpy as np
import jax
import jax.numpy as jnp
from jax import lax
from jax.experimental import pallas as pl
from jax.experimental.pallas import tpu as pltpu

F32 = jnp.float32
BF16 = jnp.bfloat16

EPS = 1e-6
GRID_W = 64
WIN_R = 8
WIN_C = 16
NA_HEADS = 8
HEAD_DIM = 128
MLA_HEADS = 8
MLA_RANK = 512
MLA_ROPE_DIM = 64
MLA_SCALE = (HEAD_DIM + MLA_ROPE_DIM) ** -0.5
ROPE_BASE = 10000.0
DN_QK_HEADS = 16
DN_V_HEADS = 32
CONV_K = 5
CHUNK = 128
NEG_BIG = -1e30

LANES = 128
MLA_QK_PAD = 256
NA_Q_ROWS = 8
NA_KEY_ROWS = 16
ATTN_SUB_ROWS = 256
IN_PROJ_ROWS = 1024
VMEM_LIMIT = 56 * 1024 * 1024


def _cparams(*sem):
    return pltpu.CompilerParams(dimension_semantics=sem, vmem_limit_bytes=VMEM_LIMIT)


def _adaln_body(c_ref, w_ref, b_ref, o_ref):
    cf = c_ref[...]
    a = (cf * jax.nn.sigmoid(cf)).astype(BF16)
    o_ref[...] = jnp.dot(a, w_ref[...].astype(BF16), preferred_element_type=F32) + b_ref[...]


def _adaln(cond, w_mod, b_mod):
    r, d = cond.shape
    n = w_mod.shape[1]
    tn = 512
    return pl.pallas_call(
        _adaln_body,
        grid=(n // tn,),
        in_specs=[pl.BlockSpec((r, d), lambda j: (0, 0)),
                  pl.BlockSpec((d, tn), lambda j: (0, j)),
                  pl.BlockSpec((1, tn), lambda j: (0, j))],
        out_specs=pl.BlockSpec((r, tn), lambda j: (0, j)),
        out_shape=jax.ShapeDtypeStruct((r, n), F32),
        compiler_params=_cparams("parallel"),
        name="adaln",
    )(cond, w_mod, b_mod.reshape(1, n))


def _proj_body(*refs, modulate, has_small, rope_main, rope_small):
    it = iter(refs)
    x_ref, gain_ref = next(it), next(it)
    shift_ref = scale_ref = ws_ref = c1_ref = c2_ref = s1_ref = s2_ref = os_ref = None
    if modulate:
        shift_ref, scale_ref = next(it), next(it)
    w_ref = next(it)
    if has_small:
        ws_ref = next(it)
    if rope_main:
        c1_ref, c2_ref = next(it), next(it)
    if rope_small:
        s1_ref, s2_ref = next(it), next(it)
    o_ref = next(it)
    if has_small:
        os_ref = next(it)
    xm_ref = next(it)

    @pl.when(pl.program_id(2) == 0)
    def _():
        xf = x_ref[0].astype(F32)
        ms = jnp.mean(xf * xf, axis=-1, keepdims=True)
        y = xf * lax.rsqrt(ms + EPS) * gain_ref[...]
        if modulate:
            y = y * (1.0 + scale_ref[0]) + shift_ref[0]
        xm = y.astype(BF16)
        xm_ref[...] = xm
        if has_small:
            s = jnp.dot(xm, ws_ref[...], preferred_element_type=F32)
            if rope_small:
                s = s * s1_ref[...] + pltpu.roll(s, LANES // 2, axis=1) * s2_ref[...]
            os_ref[0] = s.astype(os_ref.dtype)

    acc = jnp.dot(xm_ref[...], w_ref[...], preferred_element_type=F32)
    if rope_main:
        for h in range(acc.shape[1] // MLA_QK_PAD):
            a = acc[:, h * MLA_QK_PAD:(h + 1) * MLA_QK_PAD]
            a = a * c1_ref[...] + pltpu.roll(a, MLA_QK_PAD - MLA_ROPE_DIM, axis=1) * c2_ref[...]
            o_ref[0, :, h * MLA_QK_PAD:(h + 1) * MLA_QK_PAD] = a.astype(o_ref.dtype)
    else:
        o_ref[0] = acc.astype(o_ref.dtype)


def _norm_proj(x, kblk, k, gain, w, *, tm, tn, shift=None, scale=None, w_small=None,
               rope_main=None, rope_small=None, small_dtype=F32, name="proj"):
    b, t, _ = x.shape
    n = w.shape[1]
    modulate = shift is not None
    has_small = w_small is not None
    tm = min(tm, t)
    args = [x, gain.reshape(1, k).astype(F32)]
    in_specs = [pl.BlockSpec((1, tm, k), lambda bi, i, j: (bi, i, kblk)),
                pl.BlockSpec((1, k), lambda bi, i, j: (0, 0))]
    if modulate:
        args += [shift, scale]
        in_specs += [pl.BlockSpec((1, 1, k), lambda bi, i, j: (bi, 0, 0))] * 2
    args.append(w)
    in_specs.append(pl.BlockSpec((k, tn), lambda bi, i, j: (0, j)))
    if has_small:
        ns = w_small.shape[1]
        args.append(w_small)
        in_specs.append(pl.BlockSpec((k, ns), lambda bi, i, j: (0, 0)))
    if rope_main is not None:
        args += list(rope_main)
        in_specs += [pl.BlockSpec((tm, MLA_QK_PAD), lambda bi, i, j: (i, 0))] * 2
    if rope_small is not None:
        args += list(rope_small)
        in_specs += [pl.BlockSpec((tm, LANES), lambda bi, i, j: (i, 0))] * 2
    out_shape = [jax.ShapeDtypeStruct((b, t, n), BF16)]
    out_specs = [pl.BlockSpec((1, tm, tn), lambda bi, i, j: (bi, i, j))]
    if has_small:
        out_shape.append(jax.ShapeDtypeStruct((b, t, ns), small_dtype))
        out_specs.append(pl.BlockSpec((1, tm, ns), lambda bi, i, j: (bi, i, 0)))
    body = functools.partial(_proj_body, modulate=modulate, has_small=has_small,
                             rope_main=rope_main is not None, rope_small=rope_small is not None)
    out = pl.pallas_call(
        body,
        grid=(b, t // tm, n // tn),
        in_specs=in_specs,
        out_specs=out_specs,
        out_shape=out_shape,
        scratch_shapes=[pltpu.VMEM((tm, k), BF16)],
        compiler_params=_cparams("parallel", "parallel", "arbitrary"),
        name=name,
    )(*args)
    return out if has_small else out[0]


def _attn_body(*refs, seg_lens, has_extra, scale):
    it = iter(refs)
    q_ref = next(it)
    segs = []
    for _ in seg_lens:
        km = next(it)
        ke = next(it) if has_extra else None
        segs.append((km, ke, next(it)))
    o_ref, k_scr, v_scr = next(it), next(it), next(it)

    @pl.when(pl.program_id(2) == 0)
    def _():
        off = 0
        for (km, ke, v), ln in zip(segs, seg_lens):
            k_scr[off:off + ln, 0:HEAD_DIM] = km[0]
            if has_extra:
                k_scr[off:off + ln, HEAD_DIM:2 * HEAD_DIM] = ke[0]
            v_scr[off:off + ln, :] = v[0]
            off += ln

    sub = min(ATTN_SUB_ROWS, q_ref.shape[1])
    n_sub = q_ref.shape[1] // sub

    def scores(j):
        s = lax.dot_general(q_ref[0, j * sub:(j + 1) * sub, :], k_scr[...], (((1,), (1,)), ((), ())),
                            preferred_element_type=F32)
        return s if scale == 1.0 else s * scale

    s_next = scores(0)
    for j in range(n_sub):
        s = s_next
        if j + 1 < n_sub:
            s_next = scores(j + 1)
        m = jnp.max(s, axis=-1, keepdims=True)
        e = jnp.exp(s - m)
        l = jnp.sum(e, axis=-1, keepdims=True)
        o = jnp.dot(e.astype(BF16), v_scr[...], preferred_element_type=F32)
        o_ref[0, j * sub:(j + 1) * sub, :] = (o / l).astype(o_ref.dtype)


def _attention(q, q_blk0, dq, segs, heads, *, tq, scale, name):
    b, t, _ = q.shape
    tq = min(tq, t)
    has_extra = segs[0][2] is not None
    seg_lens = tuple(s[0].shape[1] for s in segs)
    tk = sum(seg_lens)
    args = [q]
    in_specs = [pl.BlockSpec((1, tq, dq), lambda bi, h, i: (bi, i, q_blk0 + h))]
    for km, kb0, ke, v, vb0 in segs:
        ln = km.shape[1]
        args.append(km)
        in_specs.append(pl.BlockSpec((1, ln, HEAD_DIM), lambda bi, h, i, kb0=kb0: (bi, 0, kb0 + h)))
        if has_extra:
            args.append(ke)
            in_specs.append(pl.BlockSpec((1, ln, HEAD_DIM), lambda bi, h, i: (bi, 0, 0)))
        args.append(v)
        in_specs.append(pl.BlockSpec((1, ln, HEAD_DIM), lambda bi, h, i, vb0=vb0: (bi, 0, vb0 + h)))
    body = functools.partial(_attn_body, seg_lens=seg_lens, has_extra=has_extra, scale=scale)
    return pl.pallas_call(
        body,
        grid=(b, heads, t // tq),
        in_specs=in_specs,
        out_specs=pl.BlockSpec((1, tq, HEAD_DIM), lambda bi, h, i: (bi, i, h)),
        out_shape=jax.ShapeDtypeStruct((b, t, heads * HEAD_DIM), BF16),
        scratch_shapes=[pltpu.VMEM((tk, dq), BF16), pltpu.VMEM((tk, HEAD_DIM), BF16)],
        compiler_params=_cparams("parallel", "parallel", "arbitrary"),
        name=name,
    )(*args)


def _na_geometry(rows):
    groups = rows // NA_Q_ROWS
    key_rows = min(NA_KEY_ROWS, rows)
    return groups, key_rows


def _na_bias_table(rpb, rows):
    heads = rpb.shape[0]
    groups, key_rows = _na_geometry(rows)
    g = np.arange(groups)
    slab0 = np.clip(NA_Q_ROWS * g - WIN_R // 2, 0, rows - key_rows)
    r = NA_Q_ROWS * g[:, None] + np.arange(NA_Q_ROWS)[None]
    k_abs = slab0[:, None] + np.arange(key_rows)[None]
    win0 = np.clip(r - WIN_R // 2, 0, rows - WIN_R)
    d_row = k_abs[:, None, :] - r[:, :, None] + (WIN_R - 1)
    ok_row = (k_abs[:, None, :] >= win0[:, :, None]) & (k_abs[:, None, :] < win0[:, :, None] + WIN_R)
    col = np.arange(GRID_W)
    c0 = np.clip(col - WIN_C // 2, 0, GRID_W - WIN_C)
    d_col = col[None, :] - col[:, None] + (WIN_C - 1)
    ok_col = (col[None, :] >= c0[:, None]) & (col[None, :] < c0[:, None] + WIN_C)
    sel_row = (np.clip(d_row, 0, 2 * WIN_R - 2)[..., None] == np.arange(2 * WIN_R - 1)).astype(np.float32)
    sel_col = (np.clip(d_col, 0, 2 * WIN_C - 2)[..., None] == np.arange(2 * WIN_C - 1)).astype(np.float32)
    ok = jnp.asarray(ok_row)[:, :, None, :, None] & jnp.asarray(ok_col)[None, None, :, None, :]
    t = jnp.einsum("grka,hab,qcb->hgrqkc", jnp.asarray(sel_row), rpb.astype(F32), jnp.asarray(sel_col),
                   precision=lax.Precision.HIGHEST)
    t = jnp.where(ok[None], t, NEG_BIG)
    return t.reshape(heads, groups, NA_Q_ROWS * GRID_W, key_rows * GRID_W)


def _na_body(q_ref, k_ref, v_ref, kc_ref, vc_ref, bias_ref, o_ref, *, rows, scale):
    groups, key_rows = _na_geometry(rows)
    g = pl.program_id(2)
    slab0 = jnp.clip(NA_Q_ROWS * g - WIN_R // 2, 0, rows - key_rows)
    start = pl.multiple_of(slab0 * GRID_W, 4 * GRID_W)
    nk = key_rows * GRID_W
    nt = (((1,), (1,)), ((), ()))
    sub = min(ATTN_SUB_ROWS, q_ref.shape[1])
    n_sub = q_ref.shape[1] // sub

    def scores(j):
        q = q_ref[0, j * sub:(j + 1) * sub, :]
        s_nb = lax.dot_general(q, k_ref[0, pl.ds(start, nk), :], nt, preferred_element_type=F32)
        s_cx = lax.dot_general(q, kc_ref[0], nt, preferred_element_type=F32)
        return s_nb, s_cx

    s_next = scores(0)
    for j in range(n_sub):
        s_nb, s_cx = s_next
        if j + 1 < n_sub:
            s_next = scores(j + 1)
        s_nb = s_nb * scale + bias_ref[0, g, j * sub:(j + 1) * sub, :]
        s_cx = s_cx * scale
        m = jnp.maximum(jnp.max(s_nb, axis=-1, keepdims=True), jnp.max(s_cx, axis=-1, keepdims=True))
        e_nb = jnp.exp(s_nb - m)
        e_cx = jnp.exp(s_cx - m)
        l = jnp.sum(e_nb, axis=-1, keepdims=True) + jnp.sum(e_cx, axis=-1, keepdims=True)
        o = jnp.dot(e_nb.astype(BF16), v_ref[0, pl.ds(start, nk), :], preferred_element_type=F32)
        o = o + jnp.dot(e_cx.astype(BF16), vc_ref[0], preferred_element_type=F32)
        o_ref[0, j * sub:(j + 1) * sub, :] = (o / l).astype(o_ref.dtype)


def _na_attention(hx, hc, bias):
    b, s, _ = hx.shape
    l = hc.shape[1]
    rows = s // GRID_W
    groups, key_rows = _na_geometry(rows)
    tq = NA_Q_ROWS * GRID_W
    body = functools.partial(_na_body, rows=rows, scale=HEAD_DIM ** -0.5)
    return pl.pallas_call(
        body,
        grid=(NA_HEADS, b, groups),
        in_specs=[pl.BlockSpec((1, tq, HEAD_DIM), lambda h, bi, g: (bi, g, h)),
                  pl.BlockSpec((1, s, HEAD_DIM), lambda h, bi, g: (bi, 0, NA_HEADS + h)),
                  pl.BlockSpec((1, s, HEAD_DIM), lambda h, bi, g: (bi, 0, 2 * NA_HEADS + h)),
                  pl.BlockSpec((1, l, HEAD_DIM), lambda h, bi, g: (bi, 0, NA_HEADS + h)),
                  pl.BlockSpec((1, l, HEAD_DIM), lambda h, bi, g: (bi, 0, 2 * NA_HEADS + h)),
                  pl.BlockSpec((1, groups, tq, key_rows * GRID_W), lambda h, bi, g: (h, 0, 0, 0))],
        out_specs=pl.BlockSpec((1, tq, HEAD_DIM), lambda h, bi, g: (bi, g, h)),
        out_shape=jax.ShapeDtypeStruct((b, s, NA_HEADS * HEAD_DIM), BF16),
        compiler_params=_cparams("parallel", "parallel", "parallel"),
        name="na_attention",
    )(hx, hx, hx, hc, hc, bias)


def _silu(x):
    return x * jax.nn.sigmoid(x)


def _out_ab_body(x_ref, gt_ref, o1_ref, o2_ref, z1_ref, z2_ref, w_ref, y_ref):
    half = o1_ref.shape[2]
    a1 = (o1_ref[0].astype(F32) * _silu(z1_ref[0].astype(F32))).astype(BF16)
    a2 = (o2_ref[0].astype(F32) * _silu(z2_ref[0].astype(F32))).astype(BF16)
    acc = jnp.dot(a1, w_ref[0:half, :], preferred_element_type=F32)
    acc = acc + jnp.dot(a2, w_ref[half:2 * half, :], preferred_element_type=F32)
    y_ref[0] = x_ref[0] + gt_ref[0] * acc


def _out_proj_ab(x, gt, o1, o2, h, z_blk0, w, *, tm):
    b, t, d = x.shape
    half = o1.shape[2]
    tm = min(tm, t)
    return pl.pallas_call(
        _out_ab_body,
        grid=(b, t // tm),
        in_specs=[pl.BlockSpec((1, tm, d), lambda bi, i: (bi, i, 0)),
                  pl.BlockSpec((1, 1, d), lambda bi, i: (bi, 0, 0)),
                  pl.BlockSpec((1, tm, half), lambda bi, i: (bi, i, 0)),
                  pl.BlockSpec((1, tm, half), lambda bi, i: (bi, i, 0)),
                  pl.BlockSpec((1, tm, half), lambda bi, i: (bi, i, z_blk0)),
                  pl.BlockSpec((1, tm, half), lambda bi, i: (bi, i, z_blk0 + 1)),
                  pl.BlockSpec((2 * half, d), lambda bi, i: (0, 0))],
        out_specs=pl.BlockSpec((1, tm, d), lambda bi, i: (bi, i, 0)),
        out_shape=jax.ShapeDtypeStruct((b, t, d), F32),
        compiler_params=_cparams("parallel", "parallel"),
        name="out_proj_ab",
    )(x, gt, o1, o2, h, h, w)


def _out_dn_body(x_ref, gt_ref, o_ref, z_ref, on_ref, w_ref, fn_ref, y_ref, a_scr):
    for h in range(o_ref.shape[2] // HEAD_DIM):
        sl = slice(h * HEAD_DIM, (h + 1) * HEAD_DIM)
        oh = o_ref[0, :, sl].astype(F32)
        ms = jnp.mean(oh * oh, axis=-1, keepdims=True)
        a = oh * lax.rsqrt(ms + EPS) * on_ref[...] * _silu(z_ref[0, :, sl].astype(F32))
        a_scr[:, sl] = a.astype(BF16)
    acc = jnp.dot(a_scr[...], w_ref[...], preferred_element_type=F32)
    y = x_ref[0] + gt_ref[0] * acc
    ms = jnp.mean(y * y, axis=-1, keepdims=True)
    y_ref[0] = y * lax.rsqrt(ms + EPS) * fn_ref[...]


def _out_proj_dn(x, gt, o, h, z_blk, o_norm, w, final_norm, *, tm):
    b, t, d = x.shape
    km = o.shape[2]
    tm = min(tm, t)
    return pl.pallas_call(
        _out_dn_body,
        grid=(b, t // tm),
        in_specs=[pl.BlockSpec((1, tm, d), lambda bi, i: (bi, i, 0)),
                  pl.BlockSpec((1, 1, d), lambda bi, i: (bi, 0, 0)),
                  pl.BlockSpec((1, tm, km), lambda bi, i: (bi, i, 0)),
                  pl.BlockSpec((1, tm, km), lambda bi, i: (bi, i, z_blk)),
                  pl.BlockSpec((1, HEAD_DIM), lambda bi, i: (0, 0)),
                  pl.BlockSpec((km, d), lambda bi, i: (0, 0)),
                  pl.BlockSpec((1, d), lambda bi, i: (0, 0))],
        out_specs=pl.BlockSpec((1, tm, d), lambda bi, i: (bi, i, 0)),
        out_shape=jax.ShapeDtypeStruct((b, t, d), F32),
        scratch_shapes=[pltpu.VMEM((tm, km), BF16)],
        compiler_params=_cparams("parallel", "parallel"),
        name="out_proj_dn",
    )(x, gt, o, h, o_norm.reshape(1, HEAD_DIM).astype(F32), w, final_norm.reshape(1, d).astype(F32))


CONV_HALO = 16


def _conv_body(xp_ref, x_ref, xn_ref, w_ref, o_ref, pad_scr, *, l2_scale):
    i = pl.program_id(1)
    nt = pl.num_programs(1)
    tt = x_ref.shape[1]
    prev = xp_ref[0].astype(F32)
    nxt = xn_ref[0].astype(F32)
    pad_scr[0:CONV_HALO, :] = jnp.where(i > 0, prev, 0.0)
    pad_scr[CONV_HALO:CONV_HALO + tt, :] = x_ref[0].astype(F32)
    pad_scr[CONV_HALO + tt:2 * CONV_HALO + tt, :] = jnp.where(i < nt - 1, nxt, 0.0)
    acc = None
    for j in range(CONV_K):
        lo = CONV_HALO + j - CONV_K // 2
        term = pad_scr[lo:lo + tt, :] * w_ref[j:j + 1, :]
        acc = term if acc is None else acc + term
    y = _silu(acc)
    if l2_scale is None:
        o_ref[0] = y.astype(o_ref.dtype)
    else:
        for h in range(y.shape[1] // HEAD_DIM):
            yh = y[:, h * HEAD_DIM:(h + 1) * HEAD_DIM]
            ss = jnp.sum(yh * yh, axis=-1, keepdims=True)
            o_ref[0, :, h * HEAD_DIM:(h + 1) * HEAD_DIM] = (yh * lax.rsqrt(ss + EPS) * l2_scale).astype(o_ref.dtype)


def _dn_conv(h, conv_w, col0, width, l2_scale, *, tt, cb=512):
    b, t, _ = h.shape
    tt = min(tt, t)
    blk0 = col0 // cb
    hb = tt // CONV_HALO
    nhalo = t // CONV_HALO
    body = functools.partial(_conv_body, l2_scale=l2_scale)
    return pl.pallas_call(
        body,
        grid=(b, t // tt, width // cb),
        in_specs=[pl.BlockSpec((1, CONV_HALO, cb), lambda bi, i, c: (bi, jnp.maximum(i * hb - 1, 0), blk0 + c)),
                  pl.BlockSpec((1, tt, cb), lambda bi, i, c: (bi, i, blk0 + c)),
                  pl.BlockSpec((1, CONV_HALO, cb), lambda bi, i, c: (bi, jnp.minimum((i + 1) * hb, nhalo - 1), blk0 + c)),
                  pl.BlockSpec((CONV_K, cb), lambda bi, i, c: (0, blk0 + c))],
        out_specs=pl.BlockSpec((1, tt, cb), lambda bi, i, c: (bi, i, c)),
        out_shape=jax.ShapeDtypeStruct((b, t, width), BF16),
        scratch_shapes=[pltpu.VMEM((tt + 2 * CONV_HALO, cb), F32)],
        compiler_params=_cparams("parallel", "parallel", "parallel"),
        name="dn_conv",
    )(h, h, h, conv_w)


GATE_TILE = 256


def _split3(x):
    hi = x.astype(BF16)
    r1 = x - hi.astype(F32)
    mid = r1.astype(BF16)
    lo = (r1 - mid.astype(F32)).astype(BF16)
    return hi, mid, lo


def _gates_body(h_ref, alog_ref, dtb_ref, o_ref):
    t = h_ref[0]
    lane = lax.broadcasted_iota(jnp.int32, t.shape, 1) % 4
    beta = jax.nn.sigmoid(t)
    z = t + dtb_ref[...]
    softplus = jnp.maximum(z, 0.0) + jnp.log1p(jnp.exp(-jnp.abs(z)))
    g = -jnp.exp(alog_ref[...]) * softplus
    n = t.shape[0]
    r = lax.broadcasted_iota(jnp.int32, (n, n), 0)
    c = lax.broadcasted_iota(jnp.int32, (n, n), 1)
    same = (r // CHUNK) == (c // CHUNK)
    tri_f = jnp.where(same & (r >= c), 1.0, 0.0).astype(BF16)
    tri_b = jnp.where(same & (r <= c), 1.0, 0.0).astype(BF16)
    pf = pb = None
    for part in _split3(g):
        df = jnp.dot(tri_f, part, preferred_element_type=F32)
        db = jnp.dot(tri_b, part, preferred_element_type=F32)
        pf = df if pf is None else pf + df
        pb = db if pb is None else pb + db
    o_ref[0] = jnp.where(lane < 2, beta, jnp.where(lane == 2, pf, pb))


def _dn_gates(h_small, alog, dtb):
    b, t, n = h_small.shape
    tt = min(GATE_TILE, t)
    return pl.pallas_call(
        _gates_body,
        grid=(b, t // tt),
        in_specs=[pl.BlockSpec((1, tt, n), lambda bi, i: (bi, i, 0)),
                  pl.BlockSpec((1, n), lambda bi, i: (0, 0)),
                  pl.BlockSpec((1, n), lambda bi, i: (0, 0))],
        out_specs=pl.BlockSpec((1, tt, n), lambda bi, i: (bi, i, 0)),
        out_shape=jax.ShapeDtypeStruct((b, t, n), F32),
        compiler_params=_cparams("parallel", "parallel"),
        name="dn_gates",
    )(h_small, alog, dtb)


HEADS_PER_CALL = 4
N_CHAINS = 4 * HEADS_PER_CALL
GATE_LANES = 8
MERGE_LEVELS = tuple(4 << i for i in range((CHUNK // 4).bit_length() - 1))


def _bdot(a, b):
    return jnp.dot(a.astype(BF16), b.astype(BF16), preferred_element_type=F32)


def _unit_tri_inverses(lms, eye, level):
    l4s = [jnp.where(level == 0, lm, 0.0) for lm in lms]
    ts = [eye - l4 for l4 in l4s]
    sq = [_bdot(l4, l4) for l4 in l4s]
    ts = [t + _bdot(t, p) for t, p in zip(ts, sq)]
    for m in range(1, len(MERGE_LEVELS) + 1):
        cs = [jnp.where(level == m, lm, 0.0).astype(BF16) for lm in lms]
        tbs = [t.astype(BF16) for t in ts]
        xs = [jnp.dot(tb, c, preferred_element_type=F32) for tb, c in zip(tbs, cs)]
        ts = [t - _bdot(x, tb) for t, x, tb in zip(ts, xs, tbs)]
    return ts


def _chunk_body(qc_ref, kc_ref, vc_ref, gcc_ref, grc_ref, qx_ref, kx_ref, vx_ref, gcx_ref, grx_ref,
                o_ref, s_scr, o_scr, u_scr, wq_scr, ak_scr, dl_scr):
    jg = pl.program_id(1)
    group_lanes = GATE_LANES * HEADS_PER_CALL
    lane_shift = (LANES - group_lanes * jg) % LANES
    s_scr[...] = jnp.zeros_like(s_scr)
    o_scr[...] = jnp.zeros_like(o_scr)
    row = lax.broadcasted_iota(jnp.int32, (CHUNK, CHUNK), 0)
    col = lax.broadcasted_iota(jnp.int32, (CHUNK, CHUNK), 1)
    eye = jnp.where(row == col, 1.0, 0.0)
    level = jnp.zeros((CHUNK, CHUNK), jnp.int32)
    for size in MERGE_LEVELS:
        level = level + jnp.where((row // size) == (col // size), 0, 1)
    incl = (row >= col, row <= col)
    strict = (row > col, row < col)
    nt = (((1,), (1,)), ((), ()))
    chains = [(jh, vh, d) for d in range(2) for jh in range(HEADS_PER_CALL) for vh in range(2)]
    streams = [(jh, d) for d in range(2) for jh in range(HEADS_PER_CALL)]

    def head_cols(jh):
        return slice(jh * HEAD_DIM, (jh + 1) * HEAD_DIM)

    def value_cols(jh, vh):
        return slice((2 * jh + vh) * HEAD_DIM, (2 * jh + vh + 1) * HEAD_DIM)

    def run(q_ref, k_ref, v_ref, gc_ref, gr_ref, write_out):
        n_chunks = q_ref.shape[1] // CHUNK

        def chunk_of(i, d):
            n = i if d == 0 else n_chunks - 1 - i
            return n, pl.multiple_of(n * CHUNK, CHUNK)

        def prepare(i, slot):
            ns, r0s, gcols = {}, {}, {}
            for d in range(2):
                ns[d], r0s[d] = chunk_of(i, d)
                gcols[d] = pltpu.roll(gc_ref[0, pl.ds(r0s[d], CHUNK), :], lane_shift, axis=1)
            ks = {(jh, d): k_ref[0, pl.ds(r0s[d], CHUNK), head_cols(jh)] for jh, d in streams}
            kks = {sd: lax.dot_general(ks[sd], ks[sd], nt, preferred_element_type=F32) for sd in streams}
            if write_out:
                qs = {(jh, d): q_ref[0, pl.ds(r0s[d], CHUNK), head_cols(jh)] for jh, d in streams}
                qks = {sd: lax.dot_general(qs[sd], ks[sd], nt, preferred_element_type=F32) for sd in streams}
            grows = {(jh, d): gr_ref[0, jh, ns[d]] for jh, d in streams}
            lane0 = [GATE_LANES * jh + 4 * vh + d for jh, vh, d in chains]
            beta_c = [gcols[d][:, l:l + 1] for l, (jh, vh, d) in zip(lane0, chains)]
            g_c = [gcols[d][:, l + 2:l + 3] for l, (jh, vh, d) in zip(lane0, chains)]
            beta_r = [grows[jh, d][4 * vh + d:4 * vh + d + 1, :] for jh, vh, d in chains]
            g_r = [grows[jh, d][4 * vh + 2 + d:4 * vh + 3 + d, :] for jh, vh, d in chains]
            g_last = [g[:, CHUNK - 1:CHUNK] if d == 0 else g[:, 0:1] for g, (jh, vh, d) in zip(g_r, chains)]
            dec = [jnp.where(incl[d], jnp.exp(jnp.where(incl[d], gc - gr, 0.0)), 0.0)
                   for gc, gr, (jh, vh, d) in zip(g_c, g_r, chains)]
            lms = [jnp.where(strict[d], kks[jh, d] * dc, 0.0) * bc
                   for dc, bc, (jh, vh, d) in zip(dec, beta_c, chains)]
            tbs = [t * br for t, br in zip(_unit_tri_inverses(lms, eye, level), beta_r)]
            us = [jnp.dot(tb.astype(BF16), v_ref[0, pl.ds(r0s[d], CHUNK), value_cols(jh, vh)],
                          preferred_element_type=F32) for tb, (jh, vh, d) in zip(tbs, chains)]
            ws = [jnp.dot((tb * jnp.exp(gr)).astype(BF16), ks[jh, d], preferred_element_type=F32)
                  for tb, gr, (jh, vh, d) in zip(tbs, g_r, chains)]
            for ci, (jh, vh, d) in enumerate(chains):
                u_scr[slot, ci] = us[ci]
                wq_scr[slot, ci, 0:CHUNK, :] = ws[ci].astype(BF16)
                kg = ks[jh, d].astype(F32) * jnp.exp(g_last[ci] - g_c[ci])
                ak_scr[slot, ci, CHUNK:CHUNK + HEAD_DIM, :] = kg.T.astype(BF16)
                dl_scr[slot, ci] = jnp.broadcast_to(jnp.exp(g_last[ci]), (1, HEAD_DIM))
                if write_out:
                    wq_scr[slot, ci, CHUNK:2 * CHUNK, :] = (qs[jh, d].astype(F32) * jnp.exp(g_c[ci])).astype(BF16)
                    ak_scr[slot, ci, 0:CHUNK, :] = (qks[jh, d] * dec[ci]).astype(BF16)

        def advance(i, slot):
            r0s = [chunk_of(i, d)[1] for d in range(2)]
            s_old = [s_scr[ci] for ci in range(N_CHAINS)]
            sbs = [s.astype(BF16) for s in s_old]
            rows = slice(0, 2 * CHUNK if write_out else CHUNK)
            r1 = [jnp.dot(wq_scr[slot, ci, rows, :], sbs[ci], preferred_element_type=F32) for ci in range(N_CHAINS)]
            vns = [(u_scr[slot, ci] - r1[ci][0:CHUNK]).astype(BF16) for ci in range(N_CHAINS)]
            rows = slice(0 if write_out else CHUNK, CHUNK + HEAD_DIM)
            r2 = [jnp.dot(ak_scr[slot, ci, rows, :], vns[ci], preferred_element_type=F32) for ci in range(N_CHAINS)]
            for ci, (jh, vh, d) in enumerate(chains):
                if write_out:
                    o_scr[pl.ds(r0s[d], CHUNK), value_cols(jh, vh)] += r1[ci][CHUNK:2 * CHUNK] + r2[ci][0:CHUNK]
                s_scr[ci] = s_old[ci] * dl_scr[slot, ci] + r2[ci][-HEAD_DIM:]

        prepare(0, 0)

        def pair_of_steps(m, carry):
            i = 2 * m
            advance(i, 0)
            prepare(i + 1, 1)
            advance(i + 1, 1)
            prepare(jnp.minimum(i + 2, n_chunks - 1), 0)
            return carry

        lax.fori_loop(0, n_chunks // 2, pair_of_steps, 0)

    run(qc_ref, kc_ref, vc_ref, gcc_ref, grc_ref, False)
    run(qx_ref, kx_ref, vx_ref, gcx_ref, grx_ref, True)
    o_ref[0] = o_scr[...].astype(o_ref.dtype)


def _gate_rows(gcol):
    b, t, n = gcol.shape
    g = gcol.reshape(b, t // CHUNK, CHUNK, n // GATE_LANES, GATE_LANES)
    return g.transpose(0, 3, 1, 4, 2)


def _dn_chunk(qc, kc, vc, gcc, qx, kx, vx, gcx):
    b, s, _ = qx.shape
    l = qc.shape[1]
    grc, grx = _gate_rows(gcc), _gate_rows(gcx)
    qk_cols = HEADS_PER_CALL * HEAD_DIM
    v_cols = 2 * qk_cols

    def seq_specs(t):
        return [pl.BlockSpec((1, t, qk_cols), lambda bi, j: (bi, 0, j)),
                pl.BlockSpec((1, t, qk_cols), lambda bi, j: (bi, 0, j)),
                pl.BlockSpec((1, t, v_cols), lambda bi, j: (bi, 0, j)),
                pl.BlockSpec((1, t, LANES), lambda bi, j: (bi, 0, 0)),
                pl.BlockSpec((1, HEADS_PER_CALL, t // CHUNK, GATE_LANES, CHUNK), lambda bi, j: (bi, j, 0, 0, 0))]

    return pl.pallas_call(
        _chunk_body,
        grid=(b, DN_QK_HEADS // HEADS_PER_CALL),
        in_specs=seq_specs(l) + seq_specs(s),
        out_specs=pl.BlockSpec((1, s, v_cols), lambda bi, j: (bi, 0, j)),
        out_shape=jax.ShapeDtypeStruct((b, s, DN_V_HEADS * HEAD_DIM), BF16),
        scratch_shapes=[pltpu.VMEM((N_CHAINS, HEAD_DIM, HEAD_DIM), F32),
                        pltpu.VMEM((s, v_cols), F32),
                        pltpu.VMEM((2, N_CHAINS, CHUNK, HEAD_DIM), F32),
                        pltpu.VMEM((2, N_CHAINS, 2 * CHUNK, HEAD_DIM), BF16),
                        pltpu.VMEM((2, N_CHAINS, CHUNK + HEAD_DIM, CHUNK), BF16),
                        pltpu.VMEM((2, N_CHAINS, 1, HEAD_DIM), F32)],
        compiler_params=_cparams("parallel", "parallel"),
        name="dn_chunk",
    )(qc, kc, vc, gcc, grc, qx, kx, vx, gcx, grx)


def _rot_cols(w):
    q = MLA_ROPE_DIM // 4
    return jnp.concatenate([-w[..., q:2 * q], w[..., 0:q], -w[..., 3 * q:4 * q], w[..., 2 * q:3 * q]], axis=-1)


def _rope_cos_sin(t_len):
    q = MLA_ROPE_DIM // 4
    pos = np.arange(t_len)
    inv = ROPE_BASE ** (-np.arange(q, dtype=np.float32) / q)
    ang_r = (pos // GRID_W).astype(np.float32)[:, None] * inv[None, :]
    ang_c = (pos % GRID_W).astype(np.float32)[:, None] * inv[None, :]
    ang = jnp.asarray(np.concatenate([ang_r, ang_r, ang_c, ang_c], axis=1))
    return jnp.cos(ang), jnp.sin(ang)


def _q_tables(cos, sin):
    t = cos.shape[0]
    one, zero = jnp.ones((t, HEAD_DIM), F32), jnp.zeros((t, MLA_ROPE_DIM), F32)
    c1 = jnp.concatenate([one, cos, zero], axis=1) * MLA_SCALE
    c2 = jnp.concatenate([jnp.zeros((t, HEAD_DIM), F32), sin, zero], axis=1) * MLA_SCALE
    return c1, c2


def _k_tables(cos, sin):
    zero = jnp.zeros_like(cos)
    return jnp.concatenate([cos, zero], axis=1), jnp.concatenate([sin, zero], axis=1)


def _mod_params(c, c_ctx, w_mod, b_mod):
    b, d = c.shape
    rows = ((b + 1 + 7) // 8) * 8
    cond = jnp.concatenate([c, c_ctx[None], jnp.zeros((rows - b - 1, d), F32)], axis=0)
    mod = _adaln(cond, w_mod, b_mod)
    sh, sc, gt = (mod[:b, i * d:(i + 1) * d].reshape(b, 1, d) for i in range(3))
    sh_c, sc_c, gt_c = (jnp.broadcast_to(mod[b, i * d:(i + 1) * d].reshape(1, 1, d), (b, 1, d)) for i in range(3))
    return (sh, sc, gt), (sh_c, sc_c, gt_c)


def _na_mla_layer(x, ctx, c, c_ctx, w_mod, b_mod, norm, w_in, rpb, q_norm, w_qb, kv_norm, w_kvb, w_out):
    b, s, d = x.shape
    l = ctx.shape[1]
    (sh, sc, gt), (sh_c, sc_c, gt_c) = _mod_params(c, c_ctx, w_mod, b_mod)
    na_w = NA_HEADS * HEAD_DIM
    n_main = 3 * na_w + 2 * MLA_RANK
    kr_w = w_in[:, n_main:n_main + MLA_ROPE_DIM]
    w_main = jnp.concatenate([w_in[:, :n_main], w_in[:, n_main + MLA_ROPE_DIM:]], axis=1).astype(BF16)
    w_small = jnp.concatenate([kr_w, _rot_cols(kr_w)], axis=1).astype(BF16)
    cos, sin = _rope_cos_sin(s)
    ident = (jnp.ones((l, MLA_ROPE_DIM), F32), jnp.zeros((l, MLA_ROPE_DIM), F32))
    hx, krx = _norm_proj(x, 0, d, norm, w_main, tm=IN_PROJ_ROWS, tn=1024, shift=sh, scale=sc, w_small=w_small,
                         rope_small=_k_tables(cos, sin), small_dtype=BF16, name="in_proj_ab_x")
    hc, krc = _norm_proj(ctx, 0, d, norm, w_main, tm=IN_PROJ_ROWS, tn=1024, shift=sh_c, scale=sc_c, w_small=w_small,
                         rope_small=_k_tables(*ident), small_dtype=BF16, name="in_proj_ab_ctx")

    wq = w_qb.reshape(MLA_RANK, MLA_HEADS, HEAD_DIM + MLA_ROPE_DIM)
    wq_r = wq[..., HEAD_DIM:]
    wq_p = jnp.concatenate([wq[..., :HEAD_DIM], wq_r, _rot_cols(wq_r)], axis=-1)
    wq_p = wq_p.reshape(MLA_RANK, MLA_HEADS * MLA_QK_PAD).astype(BF16)
    wkv = w_kvb.reshape(MLA_RANK, MLA_HEADS, 2 * HEAD_DIM)
    wkv_p = jnp.concatenate([wkv[..., :HEAD_DIM].reshape(MLA_RANK, -1), wkv[..., HEAD_DIM:].reshape(MLA_RANK, -1)],
                            axis=1).astype(BF16)
    cq_blk = 3 * na_w // MLA_RANK
    nq = MLA_HEADS * MLA_QK_PAD
    qx = _norm_proj(hx, cq_blk, MLA_RANK, q_norm, wq_p, tm=1024, tn=nq, rope_main=_q_tables(cos, sin), name="mla_q_x")
    qc = _norm_proj(hc, cq_blk, MLA_RANK, q_norm, wq_p, tm=1024, tn=nq, rope_main=_q_tables(*ident), name="mla_q_ctx")
    kvx = _norm_proj(hx, cq_blk + 1, MLA_RANK, kv_norm, wkv_p, tm=1024, tn=nq, name="mla_kv_x")
    kvc = _norm_proj(hc, cq_blk + 1, MLA_RANK, kv_norm, wkv_p, tm=1024, tn=nq, name="mla_kv_ctx")

    o_na = _na_attention(hx, hc, _na_bias_table(rpb, s // GRID_W))
    seg_x = (kvx, 0, krx, kvx, MLA_HEADS)
    seg_c = (kvc, 0, krc, kvc, MLA_HEADS)
    o_mla = _attention(qx, 0, MLA_QK_PAD, [seg_x, seg_c], MLA_HEADS, tq=512, scale=1.0, name="mla_attention")
    z_blk0 = n_main // na_w
    w_out_b = w_out.astype(BF16)
    x_new = _out_proj_ab(x, gt, o_na, o_mla, hx, z_blk0, w_out_b, tm=512)

    o_na_c = _attention(hc, 0, HEAD_DIM, [(hc, NA_HEADS, None, hc, 2 * NA_HEADS)], NA_HEADS, tq=l,
                        scale=HEAD_DIM ** -0.5, name="na_ctx_attention")
    o_mla_c = _attention(qc, 0, MLA_QK_PAD, [seg_c], MLA_HEADS, tq=l, scale=1.0, name="mla_ctx_attention")
    ctx_new = _out_proj_ab(ctx, gt_c, o_na_c, o_mla_c, hc, z_blk0, w_out_b, tm=512)
    return x_new, ctx_new


def _deltanet_final_layer(x, ctx, c, c_ctx, w_mod, b_mod, norm, w_in, conv_w, a_log, dt_bias, o_norm, w_out, final_norm):
    b, s, d = x.shape
    (sh, sc, gt), (sh_c, sc_c, _) = _mod_params(c, c_ctx, w_mod, b_mod)
    qk_w = DN_QK_HEADS * HEAD_DIM
    v_w = DN_V_HEADS * HEAD_DIM
    n_main = 2 * qk_w + 2 * v_w
    lane = np.arange(4 * DN_V_HEADS)
    perm = (lane % 4) * DN_V_HEADS + lane // 4
    w_main = w_in[:, :n_main].astype(BF16)
    w_small = w_in[:, n_main:][:, perm].astype(BF16)
    zeros = jnp.zeros((2 * DN_V_HEADS,), F32)
    alog = jnp.concatenate([zeros, a_log.reshape(-1).astype(F32)])[perm].reshape(1, -1)
    dtb = jnp.concatenate([zeros, dt_bias.reshape(-1).astype(F32)])[perm].reshape(1, -1)
    conv_w = conv_w.astype(F32)

    def inputs(t, shift, scale, name):
        h, hs = _norm_proj(t, 0, d, norm, w_main, tm=IN_PROJ_ROWS, tn=1024, shift=shift, scale=scale, w_small=w_small,
                           name=name)
        q = _dn_conv(h, conv_w, 0, qk_w, HEAD_DIM ** -0.5, tt=512)
        k = _dn_conv(h, conv_w, qk_w, qk_w, 1.0, tt=512)
        v = _dn_conv(h, conv_w, 2 * qk_w, v_w, None, tt=512)
        return h, q, k, v, _dn_gates(hs, alog, dtb)

    hx, qx, kx, vx, gx = inputs(x, sh, sc, "in_proj_dn_x")
    _, qc, kc, vc, gc = inputs(ctx, sh_c, sc_c, "in_proj_dn_ctx")
    o = _dn_chunk(qc, kc, vc, gc, qx, kx, vx, gx)
    return _out_proj_dn(x, gt, o, hx, (2 * qk_w + v_w) // v_w, o_norm, w_out.astype(BF16), final_norm, tm=256)


def kernel(x, c, ctx, c_ctx, ab_w_mod, ab_b_mod, ab_norm, ab_w_in, ab_rpb, ab_q_norm, ab_w_qb, ab_kv_norm, ab_w_kvb, ab_w_out, dn_w_mod, dn_b_mod, dn_norm, dn_w_in, dn_conv, dn_a_log, dn_dt_bias, dn_o_norm, dn_w_out, final_norm):
    assert ab_w_mod.shape[0] == 1 and dn_w_mod.shape[0] == 1, "depth-2 trunk: one even and one odd layer"
    x, ctx = _na_mla_layer(x, ctx, c, c_ctx, ab_w_mod[0], ab_b_mod[0], ab_norm[0], ab_w_in[0], ab_rpb[0],
                           ab_q_norm[0], ab_w_qb[0], ab_kv_norm[0], ab_w_kvb[0], ab_w_out[0])
    return _deltanet_final_layer(x, ctx, c, c_ctx, dn_w_mod[0], dn_b_mod[0], dn_norm[0], dn_w_in[0], dn_conv[0],
                                 dn_a_log[0], dn_dt_bias[0], dn_o_norm[0], dn_w_out[0], final_norm)
```

```python
import functools

import numpy as np
import jax
import jax.numpy as jnp
from jax import lax
from jax.experimental import pallas as pl
from jax.experimental.pallas import tpu as pltpu

F32 = jnp.float32
BF16 = jnp.bfloat16

EPS = 1e-6
GRID_W = 64
WIN_R = 8
WIN_C = 16
NA_HEADS = 8
HEAD_DIM = 128
MLA_HEADS = 8
MLA_RANK = 512
MLA_ROPE_DIM = 64
MLA_SCALE = (HEAD_DIM + MLA_ROPE_DIM) ** -0.5
ROPE_BASE = 10000.0
DN_QK_HEADS = 16
DN_V_HEADS = 32
CONV_K = 5
CHUNK = 128
NEG_BIG = -1e30

LANES = 128
MLA_QK_PAD = 256
NA_Q_ROWS = 8
NA_KEY_ROWS = 16
ATTN_SUB_ROWS = 256
IN_PROJ_ROWS = 1024
VMEM_LIMIT = 56 * 1024 * 1024


def _cparams(*sem):
    return pltpu.CompilerParams(dimension_semantics=sem, vmem_limit_bytes=VMEM_LIMIT)


def _adaln_body(c_ref, w_ref, b_ref, o_ref):
    cf = c_ref[...]
    a = (cf * jax.nn.sigmoid(cf)).astype(BF16)
    o_ref[...] = jnp.dot(a, w_ref[...].astype(BF16), preferred_element_type=F32) + b_ref[...]


def _adaln(cond, w_mod, b_mod):
    r, d = cond.shape
    n = w_mod.shape[1]
    tn = 512
    return pl.pallas_call(
        _adaln_body,
        grid=(n // tn,),
        in_specs=[pl.BlockSpec((r, d), lambda j: (0, 0)),
                  pl.BlockSpec((d, tn), lambda j: (0, j)),
                  pl.BlockSpec((1, tn), lambda j: (0, j))],
        out_specs=pl.BlockSpec((r, tn), lambda j: (0, j)),
        out_shape=jax.ShapeDtypeStruct((r, n), F32),
        compiler_params=_cparams("parallel"),
        name="adaln",
    )(cond, w_mod, b_mod.reshape(1, n))


def _proj_body(*refs, modulate, has_small, rope_main, rope_small):
    it = iter(refs)
    x_ref, gain_ref = next(it), next(it)
    shift_ref = scale_ref = ws_ref = c1_ref = c2_ref = s1_ref = s2_ref = os_ref = None
    if modulate:
        shift_ref, scale_ref = next(it), next(it)
    w_ref = next(it)
    if has_small:
        ws_ref = next(it)
    if rope_main:
        c1_ref, c2_ref = next(it), next(it)
    if rope_small:
        s1_ref, s2_ref = next(it), next(it)
    o_ref = next(it)
    if has_small:
        os_ref = next(it)
    xm_ref = next(it)

    @pl.when(pl.program_id(2) == 0)
    def _():
        xf = x_ref[0].astype(F32)
        ms = jnp.mean(xf * xf, axis=-1, keepdims=True)
        y = xf * lax.rsqrt(ms + EPS) * gain_ref[...]
        if modulate:
            y = y * (1.0 + scale_ref[0]) + shift_ref[0]
        xm = y.astype(BF16)
        xm_ref[...] = xm
        if has_small:
            s = jnp.dot(xm, ws_ref[...], preferred_element_type=F32)
            if rope_small:
                s = s * s1_ref[...] + pltpu.roll(s, LANES // 2, axis=1) * s2_ref[...]
            os_ref[0] = s.astype(os_ref.dtype)

    acc = jnp.dot(xm_ref[...], w_ref[...], preferred_element_type=F32)
    if rope_main:
        for h in range(acc.shape[1] // MLA_QK_PAD):
            a = acc[:, h * MLA_QK_PAD:(h + 1) * MLA_QK_PAD]
            a = a * c1_ref[...] + pltpu.roll(a, MLA_QK_PAD - MLA_ROPE_DIM, axis=1) * c2_ref[...]
            o_ref[0, :, h * MLA_QK_PAD:(h + 1) * MLA_QK_PAD] = a.astype(o_ref.dtype)
    else:
        o_ref[0] = acc.astype(o_ref.dtype)


def _norm_proj(x, kblk, k, gain, w, *, tm, tn, shift=None, scale=None, w_small=None,
               rope_main=None, rope_small=None, small_dtype=F32, name="proj"):
    b, t, _ = x.shape
    n = w.shape[1]
    modulate = shift is not None
    has_small = w_small is not None
    tm = min(tm, t)
    args = [x, gain.reshape(1, k).astype(F32)]
    in_specs = [pl.BlockSpec((1, tm, k), lambda bi, i, j: (bi, i, kblk)),
                pl.BlockSpec((1, k), lambda bi, i, j: (0, 0))]
    if modulate:
        args += [shift, scale]
        in_specs += [pl.BlockSpec((1, 1, k), lambda bi, i, j: (bi, 0, 0))] * 2
    args.append(w)
    in_specs.append(pl.BlockSpec((k, tn), lambda bi, i, j: (0, j)))
    if has_small:
        ns = w_small.shape[1]
        args.append(w_small)
        in_specs.append(pl.BlockSpec((k, ns), lambda bi, i, j: (0, 0)))
    if rope_main is not None:
        args += list(rope_main)
        in_specs += [pl.BlockSpec((tm, MLA_QK_PAD), lambda bi, i, j: (i, 0))] * 2
    if rope_small is not None:
        args += list(rope_small)
        in_specs += [pl.BlockSpec((tm, LANES), lambda bi, i, j: (i, 0))] * 2
    out_shape = [jax.ShapeDtypeStruct((b, t, n), BF16)]
    out_specs = [pl.BlockSpec((1, tm, tn), lambda bi, i, j: (bi, i, j))]
    if has_small:
        out_shape.append(jax.ShapeDtypeStruct((b, t, ns), small_dtype))
        out_specs.append(pl.BlockSpec((1, tm, ns), lambda bi, i, j: (bi, i, 0)))
    body = functools.partial(_proj_body, modulate=modulate, has_small=has_small,
                             rope_main=rope_main is not None, rope_small=rope_small is not None)
    out = pl.pallas_call(
        body,
        grid=(b, t // tm, n // tn),
        in_specs=in_specs,
        out_specs=out_specs,
        out_shape=out_shape,
        scratch_shapes=[pltpu.VMEM((tm, k), BF16)],
        compiler_params=_cparams("parallel", "parallel", "arbitrary"),
        name=name,
    )(*args)
    return out if has_small else out[0]


def _attn_body(*refs, seg_lens, has_extra, scale):
    it = iter(refs)
    q_ref = next(it)
    segs = []
    for _ in seg_lens:
        km = next(it)
        ke = next(it) if has_extra else None
        segs.append((km, ke, next(it)))
    o_ref, k_scr, v_scr = next(it), next(it), next(it)

    @pl.when(pl.program_id(2) == 0)
    def _():
        off = 0
        for (km, ke, v), ln in zip(segs, seg_lens):
            k_scr[off:off + ln, 0:HEAD_DIM] = km[0]
            if has_extra:
                k_scr[off:off + ln, HEAD_DIM:2 * HEAD_DIM] = ke[0]
            v_scr[off:off + ln, :] = v[0]
            off += ln

    sub = min(ATTN_SUB_ROWS, q_ref.shape[1])
    n_sub = q_ref.shape[1] // sub

    def scores(j):
        s = lax.dot_general(q_ref[0, j * sub:(j + 1) * sub, :], k_scr[...], (((1,), (1,)), ((), ())),
                            preferred_element_type=F32)
        return s if scale == 1.0 else s * scale

    s_next = scores(0)
    for j in range(n_sub):
        s = s_next
        if j + 1 < n_sub:
            s_next = scores(j + 1)
        m = jnp.max(s, axis=-1, keepdims=True)
        e = jnp.exp(s - m)
        l = jnp.sum(e, axis=-1, keepdims=True)
        o = jnp.dot(e.astype(BF16), v_scr[...], preferred_element_type=F32)
        o_ref[0, j * sub:(j + 1) * sub, :] = (o / l).astype(o_ref.dtype)


def _attention(q, q_blk0, dq, segs, heads, *, tq, scale, name):
    b, t, _ = q.shape
    tq = min(tq, t)
    has_extra = segs[0][2] is not None
    seg_lens = tuple(s[0].shape[1] for s in segs)
    tk = sum(seg_lens)
    args = [q]
    in_specs = [pl.BlockSpec((1, tq, dq), lambda bi, h, i: (bi, i, q_blk0 + h))]
    for km, kb0, ke, v, vb0 in segs:
        ln = km.shape[1]
        args.append(km)
        in_specs.append(pl.BlockSpec((1, ln, HEAD_DIM), lambda bi, h, i, kb0=kb0: (bi, 0, kb0 + h)))
        if has_extra:
            args.append(ke)
            in_specs.append(pl.BlockSpec((1, ln, HEAD_DIM), lambda bi, h, i: (bi, 0, 0)))
        args.append(v)
        in_specs.append(pl.BlockSpec((1, ln, HEAD_DIM), lambda bi, h, i, vb0=vb0: (bi, 0, vb0 + h)))
    body = functools.partial(_attn_body, seg_lens=seg_lens, has_extra=has_extra, scale=scale)
    return pl.pallas_call(
        body,
        grid=(b, heads, t // tq),
        in_specs=in_specs,
        out_specs=pl.BlockSpec((1, tq, HEAD_DIM), lambda bi, h, i: (bi, i, h)),
        out_shape=jax.ShapeDtypeStruct((b, t, heads * HEAD_DIM), BF16),
        scratch_shapes=[pltpu.VMEM((tk, dq), BF16), pltpu.VMEM((tk, HEAD_DIM), BF16)],
        compiler_params=_cparams("parallel", "parallel", "arbitrary"),
        name=name,
    )(*args)


def _na_geometry(rows):
    groups = rows // NA_Q_ROWS
    key_rows = min(NA_KEY_ROWS, rows)
    return groups, key_rows


def _na_bias_table(rpb, rows):
    heads = rpb.shape[0]
    groups, key_rows = _na_geometry(rows)
    g = np.arange(groups)
    slab0 = np.clip(NA_Q_ROWS * g - WIN_R // 2, 0, rows - key_rows)
    r = NA_Q_ROWS * g[:, None] + np.arange(NA_Q_ROWS)[None]
    k_abs = slab0[:, None] + np.arange(key_rows)[None]
    win0 = np.clip(r - WIN_R // 2, 0, rows - WIN_R)
    d_row = k_abs[:, None, :] - r[:, :, None] + (WIN_R - 1)
    ok_row = (k_abs[:, None, :] >= win0[:, :, None]) & (k_abs[:, None, :] < win0[:, :, None] + WIN_R)
    col = np.arange(GRID_W)
    c0 = np.clip(col - WIN_C // 2, 0, GRID_W - WIN_C)
    d_col = col[None, :] - col[:, None] + (WIN_C - 1)
    ok_col = (col[None, :] >= c0[:, None]) & (col[None, :] < c0[:, None] + WIN_C)
    sel_row = (np.clip(d_row, 0, 2 * WIN_R - 2)[..., None] == np.arange(2 * WIN_R - 1)).astype(np.float32)
    sel_col = (np.clip(d_col, 0, 2 * WIN_C - 2)[..., None] == np.arange(2 * WIN_C - 1)).astype(np.float32)
    ok = jnp.asarray(ok_row)[:, :, None, :, None] & jnp.asarray(ok_col)[None, None, :, None, :]
    t = jnp.einsum("grka,hab,qcb->hgrqkc", jnp.asarray(sel_row), rpb.astype(F32), jnp.asarray(sel_col),
                   precision=lax.Precision.HIGHEST)
    t = jnp.where(ok[None], t, NEG_BIG)
    return t.reshape(heads, groups, NA_Q_ROWS * GRID_W, key_rows * GRID_W)


def _na_body(q_ref, k_ref, v_ref, kc_ref, vc_ref, bias_ref, o_ref, *, rows, scale):
    groups, key_rows = _na_geometry(rows)
    g = pl.program_id(2)
    slab0 = jnp.clip(NA_Q_ROWS * g - WIN_R // 2, 0, rows - key_rows)
    start = pl.multiple_of(slab0 * GRID_W, 4 * GRID_W)
    nk = key_rows * GRID_W
    nt = (((1,), (1,)), ((), ()))
    sub = min(ATTN_SUB_ROWS, q_ref.shape[1])
    n_sub = q_ref.shape[1] // sub

    def scores(j):
        q = q_ref[0, j * sub:(j + 1) * sub, :]
        s_nb = lax.dot_general(q, k_ref[0, pl.ds(start, nk), :], nt, preferred_element_type=F32)
        s_cx = lax.dot_general(q, kc_ref[0], nt, preferred_element_type=F32)
        return s_nb, s_cx

    s_next = scores(0)
    for j in range(n_sub):
        s_nb, s_cx = s_next
        if j + 1 < n_sub:
            s_next = scores(j + 1)
        s_nb = s_nb * scale + bias_ref[0, g, j * sub:(j + 1) * sub, :]
        s_cx = s_cx * scale
        m = jnp.maximum(jnp.max(s_nb, axis=-1, keepdims=True), jnp.max(s_cx, axis=-1, keepdims=True))
        e_nb = jnp.exp(s_nb - m)
        e_cx = jnp.exp(s_cx - m)
        l = jnp.sum(e_nb, axis=-1, keepdims=True) + jnp.sum(e_cx, axis=-1, keepdims=True)
        o = jnp.dot(e_nb.astype(BF16), v_ref[0, pl.ds(start, nk), :], preferred_element_type=F32)
        o = o + jnp.dot(e_cx.astype(BF16), vc_ref[0], preferred_element_type=F32)
        o_ref[0, j * sub:(j + 1) * sub, :] = (o / l).astype(o_ref.dtype)


def _na_attention(hx, hc, bias):
    b, s, _ = hx.shape
    l = hc.shape[1]
    rows = s // GRID_W
    groups, key_rows = _na_geometry(rows)
    tq = NA_Q_ROWS * GRID_W
    body = functools.partial(_na_body, rows=rows, scale=HEAD_DIM ** -0.5)
    return pl.pallas_call(
        body,
        grid=(NA_HEADS, b, groups),
        in_specs=[pl.BlockSpec((1, tq, HEAD_DIM), lambda h, bi, g: (bi, g, h)),
                  pl.BlockSpec((1, s, HEAD_DIM), lambda h, bi, g: (bi, 0, NA_HEADS + h)),
                  pl.BlockSpec((1, s, HEAD_DIM), lambda h, bi, g: (bi, 0, 2 * NA_HEADS + h)),
                  pl.BlockSpec((1, l, HEAD_DIM), lambda h, bi, g: (bi, 0, NA_HEADS + h)),
                  pl.BlockSpec((1, l, HEAD_DIM), lambda h, bi, g: (bi, 0, 2 * NA_HEADS + h)),
                  pl.BlockSpec((1, groups, tq, key_rows * GRID_W), lambda h, bi, g: (h, 0, 0, 0))],
        out_specs=pl.BlockSpec((1, tq, HEAD_DIM), lambda h, bi, g: (bi, g, h)),
        out_shape=jax.ShapeDtypeStruct((b, s, NA_HEADS * HEAD_DIM), BF16),
        compiler_params=_cparams("parallel", "parallel", "parallel"),
        name="na_attention",
    )(hx, hx, hx, hc, hc, bias)


def _silu(x):
    return x * jax.nn.sigmoid(x)


def _out_ab_body(x_ref, gt_ref, o1_ref, o2_ref, z1_ref, z2_ref, w_ref, y_ref):
    half = o1_ref.shape[2]
    a1 = (o1_ref[0].astype(F32) * _silu(z1_ref[0].astype(F32))).astype(BF16)
    acc = jnp.dot(a1, w_ref[0:half, :], preferred_element_type=F32)
    a2 = (o2_ref[0].astype(F32) * _silu(z2_ref[0].astype(F32))).astype(BF16)
    acc = acc + jnp.dot(a2, w_ref[half:2 * half, :], preferred_element_type=F32)
    y_ref[0] = x_ref[0] + gt_ref[0] * acc


def _out_proj_ab(x, gt, o1, o2, h, z_blk0, w, *, tm):
    b, t, d = x.shape
    half = o1.shape[2]
    tm = min(tm, t)
    return pl.pallas_call(
        _out_ab_body,
        grid=(b, t // tm),
        in_specs=[pl.BlockSpec((1, tm, d), lambda bi, i: (bi, i, 0)),
                  pl.BlockSpec((1, 1, d), lambda bi, i: (bi, 0, 0)),
                  pl.BlockSpec((1, tm, half), lambda bi, i: (bi, i, 0)),
                  pl.BlockSpec((1, tm, half), lambda bi, i: (bi, i, 0)),
                  pl.BlockSpec((1, tm, half), lambda bi, i: (bi, i, z_blk0)),
                  pl.BlockSpec((1, tm, half), lambda bi, i: (bi, i, z_blk0 + 1)),
                  pl.BlockSpec((2 * half, d), lambda bi, i: (0, 0))],
        out_specs=pl.BlockSpec((1, tm, d), lambda bi, i: (bi, i, 0)),
        out_shape=jax.ShapeDtypeStruct((b, t, d), F32),
        compiler_params=_cparams("parallel", "parallel"),
        name="out_proj_ab",
    )(x, gt, o1, o2, h, h, w)


def _out_dn_body(x_ref, gt_ref, o_ref, z_ref, on_ref, w_ref, fn_ref, y_ref, a_scr):
    heads = o_ref.shape[2] // HEAD_DIM
    parts = 2
    acc = None
    for p in range(parts):
        for h in range(p * heads // parts, (p + 1) * heads // parts):
            sl = slice(h * HEAD_DIM, (h + 1) * HEAD_DIM)
            oh = o_ref[0, :, sl].astype(F32)
            ms = jnp.mean(oh * oh, axis=-1, keepdims=True)
            a = oh * lax.rsqrt(ms + EPS) * on_ref[...] * _silu(z_ref[0, :, sl].astype(F32))
            a_scr[:, sl] = a.astype(BF16)
        cols = slice(p * heads // parts * HEAD_DIM, (p + 1) * heads // parts * HEAD_DIM)
        part = jnp.dot(a_scr[:, cols], w_ref[cols, :], preferred_element_type=F32)
        acc = part if acc is None else acc + part
    y = x_ref[0] + gt_ref[0] * acc
    ms = jnp.mean(y * y, axis=-1, keepdims=True)
    y_ref[0] = y * lax.rsqrt(ms + EPS) * fn_ref[...]


def _out_proj_dn(x, gt, o, h, z_blk, o_norm, w, final_norm, *, tm):
    b, t, d = x.shape
    km = o.shape[2]
    tm = min(tm, t)
    return pl.pallas_call(
        _out_dn_body,
        grid=(b, t // tm),
        in_specs=[pl.BlockSpec((1, tm, d), lambda bi, i: (bi, i, 0)),
                  pl.BlockSpec((1, 1, d), lambda bi, i: (bi, 0, 0)),
                  pl.BlockSpec((1, tm, km), lambda bi, i: (bi, i, 0)),
                  pl.BlockSpec((1, tm, km), lambda bi, i: (bi, i, z_blk)),
                  pl.BlockSpec((1, HEAD_DIM), lambda bi, i: (0, 0)),
                  pl.BlockSpec((km, d), lambda bi, i: (0, 0)),
                  pl.BlockSpec((1, d), lambda bi, i: (0, 0))],
        out_specs=pl.BlockSpec((1, tm, d), lambda bi, i: (bi, i, 0)),
        out_shape=jax.ShapeDtypeStruct((b, t, d), F32),
        scratch_shapes=[pltpu.VMEM((tm, km), BF16)],
        compiler_params=_cparams("parallel", "parallel"),
        name="out_proj_dn",
    )(x, gt, o, h, o_norm.reshape(1, HEAD_DIM).astype(F32), w, final_norm.reshape(1, d).astype(F32))


CONV_HALO = 16


def _conv_body(xp_ref, x_ref, xn_ref, w_ref, o_ref, pad_scr, *, l2_scale):
    i = pl.program_id(1)
    nt = pl.num_programs(1)
    tt = x_ref.shape[1]
    prev = xp_ref[0].astype(F32)
    nxt = xn_ref[0].astype(F32)
    pad_scr[0:CONV_HALO, :] = jnp.where(i > 0, prev, 0.0)
    pad_scr[CONV_HALO:CONV_HALO + tt, :] = x_ref[0].astype(F32)
    pad_scr[CONV_HALO + tt:2 * CONV_HALO + tt, :] = jnp.where(i < nt - 1, nxt, 0.0)
    acc = None
    for j in range(CONV_K):
        lo = CONV_HALO + j - CONV_K // 2
        term = pad_scr[lo:lo + tt, :] * w_ref[j:j + 1, :]
        acc = term if acc is None else acc + term
    y = _silu(acc)
    if l2_scale is None:
        o_ref[0] = y.astype(o_ref.dtype)
    else:
        for h in range(y.shape[1] // HEAD_DIM):
            yh = y[:, h * HEAD_DIM:(h + 1) * HEAD_DIM]
            ss = jnp.sum(yh * yh, axis=-1, keepdims=True)
            o_ref[0, :, h * HEAD_DIM:(h + 1) * HEAD_DIM] = (yh * lax.rsqrt(ss + EPS) * l2_scale).astype(o_ref.dtype)


def _dn_conv(h, conv_w, col0, width, l2_scale, *, tt, cb=512):
    b, t, _ = h.shape
    tt = min(tt, t)
    blk0 = col0 // cb
    hb = tt // CONV_HALO
    nhalo = t // CONV_HALO
    body = functools.partial(_conv_body, l2_scale=l2_scale)
    return pl.pallas_call(
        body,
        grid=(b, t // tt, width // cb),
        in_specs=[pl.BlockSpec((1, CONV_HALO, cb), lambda bi, i, c: (bi, jnp.maximum(i * hb - 1, 0), blk0 + c)),
                  pl.BlockSpec((1, tt, cb), lambda bi, i, c: (bi, i, blk0 + c)),
                  pl.BlockSpec((1, CONV_HALO, cb), lambda bi, i, c: (bi, jnp.minimum((i + 1) * hb, nhalo - 1), blk0 + c)),
                  pl.BlockSpec((CONV_K, cb), lambda bi, i, c: (0, blk0 + c))],
        out_specs=pl.BlockSpec((1, tt, cb), lambda bi, i, c: (bi, i, c)),
        out_shape=jax.ShapeDtypeStruct((b, t, width), BF16),
        scratch_shapes=[pltpu.VMEM((tt + 2 * CONV_HALO, cb), F32)],
        compiler_params=_cparams("parallel", "parallel", "parallel"),
        name="dn_conv",
    )(h, h, h, conv_w)


GATE_TILE = 256


def _split3(x):
    hi = x.astype(BF16)
    r1 = x - hi.astype(F32)
    mid = r1.astype(BF16)
    lo = (r1 - mid.astype(F32)).astype(BF16)
    return hi, mid, lo


def _gates_body(h_ref, alog_ref, dtb_ref, o_ref):
    t = h_ref[0]
    lane = lax.broadcasted_iota(jnp.int32, t.shape, 1) % 4
    beta = jax.nn.sigmoid(t)
    z = t + dtb_ref[...]
    softplus = jnp.maximum(z, 0.0) + jnp.log1p(jnp.exp(-jnp.abs(z)))
    g = -jnp.exp(alog_ref[...]) * softplus
    n = t.shape[0]
    r = lax.broadcasted_iota(jnp.int32, (n, n), 0)
    c = lax.broadcasted_iota(jnp.int32, (n, n), 1)
    same = (r // CHUNK) == (c // CHUNK)
    tri_f = jnp.where(same & (r >= c), 1.0, 0.0).astype(BF16)
    tri_b = jnp.where(same & (r <= c), 1.0, 0.0).astype(BF16)
    pf = pb = None
    for part in _split3(g):
        df = jnp.dot(tri_f, part, preferred_element_type=F32)
        db = jnp.dot(tri_b, part, preferred_element_type=F32)
        pf = df if pf is None else pf + df
        pb = db if pb is None else pb + db
    o_ref[0] = jnp.where(lane < 2, beta, jnp.where(lane == 2, pf, pb))


def _dn_gates(h_small, alog, dtb):
    b, t, n = h_small.shape
    tt = min(GATE_TILE, t)
    return pl.pallas_call(
        _gates_body,
        grid=(b, t // tt),
        in_specs=[pl.BlockSpec((1, tt, n), lambda bi, i: (bi, i, 0)),
                  pl.BlockSpec((1, n), lambda bi, i: (0, 0)),
                  pl.BlockSpec((1, n), lambda bi, i: (0, 0))],
        out_specs=pl.BlockSpec((1, tt, n), lambda bi, i: (bi, i, 0)),
        out_shape=jax.ShapeDtypeStruct((b, t, n), F32),
        compiler_params=_cparams("parallel", "parallel"),
        name="dn_gates",
    )(h_small, alog, dtb)


HEADS_PER_CALL = 4
N_CHAINS = 4 * HEADS_PER_CALL
GATE_LANES = 8
MERGE_LEVELS = tuple(4 << i for i in range((CHUNK // 4).bit_length() - 1))


def _bdot(a, b):
    return jnp.dot(a.astype(BF16), b.astype(BF16), preferred_element_type=F32)


def _unit_tri_inverses(lms, eye, level):
    l4s = [jnp.where(level == 0, lm, 0.0) for lm in lms]
    ts = [eye - l4 for l4 in l4s]
    sq = [_bdot(l4, l4) for l4 in l4s]
    ts = [t + _bdot(t, p) for t, p in zip(ts, sq)]
    for m in range(1, len(MERGE_LEVELS) + 1):
        cs = [jnp.where(level == m, lm, 0.0).astype(BF16) for lm in lms]
        tbs = [t.astype(BF16) for t in ts]
        xs = [jnp.dot(tb, c, preferred_element_type=F32) for tb, c in zip(tbs, cs)]
        ts = [t - _bdot(x, tb) for t, x, tb in zip(ts, xs, tbs)]
    return ts


def _chunk_body(qc_ref, kc_ref, vc_ref, gcc_ref, grc_ref, qx_ref, kx_ref, vx_ref, gcx_ref, grx_ref,
                o_ref, s_scr, o_scr, u_scr, wq_scr, ak_scr, dl_scr):
    jg = pl.program_id(1)
    group_lanes = GATE_LANES * HEADS_PER_CALL
    lane_shift = (LANES - group_lanes * jg) % LANES
    s_scr[...] = jnp.zeros_like(s_scr)
    o_scr[...] = jnp.zeros_like(o_scr)
    row = lax.broadcasted_iota(jnp.int32, (CHUNK, CHUNK), 0)
    col = lax.broadcasted_iota(jnp.int32, (CHUNK, CHUNK), 1)
    eye = jnp.where(row == col, 1.0, 0.0)
    level = jnp.zeros((CHUNK, CHUNK), jnp.int32)
    for size in MERGE_LEVELS:
        level = level + jnp.where((row // size) == (col // size), 0, 1)
    incl = (row >= col, row <= col)
    strict = (row > col, row < col)
    nt = (((1,), (1,)), ((), ()))
    chains = [(jh, vh, d) for d in range(2) for jh in range(HEADS_PER_CALL) for vh in range(2)]
    streams = [(jh, d) for d in range(2) for jh in range(HEADS_PER_CALL)]

    def head_cols(jh):
        return slice(jh * HEAD_DIM, (jh + 1) * HEAD_DIM)

    def value_cols(jh, vh):
        return slice((2 * jh + vh) * HEAD_DIM, (2 * jh + vh + 1) * HEAD_DIM)

    def run(q_ref, k_ref, v_ref, gc_ref, gr_ref, write_out):
        n_chunks = q_ref.shape[1] // CHUNK

        def chunk_of(i, d):
            n = i if d == 0 else n_chunks - 1 - i
            return n, pl.multiple_of(n * CHUNK, CHUNK)

        def prepare(i, slot):
            ns, r0s, gcols = {}, {}, {}
            for d in range(2):
                ns[d], r0s[d] = chunk_of(i, d)
                gcols[d] = pltpu.roll(gc_ref[0, pl.ds(r0s[d], CHUNK), :], lane_shift, axis=1)
            ks = {(jh, d): k_ref[0, pl.ds(r0s[d], CHUNK), head_cols(jh)] for jh, d in streams}
            kks = {sd: lax.dot_general(ks[sd], ks[sd], nt, preferred_element_type=F32) for sd in streams}
            if write_out:
                qs = {(jh, d): q_ref[0, pl.ds(r0s[d], CHUNK), head_cols(jh)] for jh, d in streams}
                qks = {sd: lax.dot_general(qs[sd], ks[sd], nt, preferred_element_type=F32) for sd in streams}
            grows = {(jh, d): gr_ref[0, jh, ns[d]] for jh, d in streams}
            lane0 = [GATE_LANES * jh + 4 * vh + d for jh, vh, d in chains]
            beta_c = [gcols[d][:, l:l + 1] for l, (jh, vh, d) in zip(lane0, chains)]
            g_c = [gcols[d][:, l + 2:l + 3] for l, (jh, vh, d) in zip(lane0, chains)]
            beta_r = [grows[jh, d][4 * vh + d:4 * vh + d + 1, :] for jh, vh, d in chains]
            g_r = [grows[jh, d][4 * vh + 2 + d:4 * vh + 3 + d, :] for jh, vh, d in chains]
            g_last = [g[:, CHUNK - 1:CHUNK] if d == 0 else g[:, 0:1] for g, (jh, vh, d) in zip(g_r, chains)]
            dec = [jnp.where(incl[d], jnp.exp(jnp.where(incl[d], gc - gr, 0.0)), 0.0)
                   for gc, gr, (jh, vh, d) in zip(g_c, g_r, chains)]
            lms = [jnp.where(strict[d], kks[jh, d] * dc, 0.0) * bc
                   for dc, bc, (jh, vh, d) in zip(dec, beta_c, chains)]
            tbs = [t * br for t, br in zip(_unit_tri_inverses(lms, eye, level), beta_r)]
            us = [jnp.dot(tb.astype(BF16), v_ref[0, pl.ds(r0s[d], CHUNK), value_cols(jh, vh)],
                          preferred_element_type=F32) for tb, (jh, vh, d) in zip(tbs, chains)]
            ws = [jnp.dot((tb * jnp.exp(gr)).astype(BF16), ks[jh, d], preferred_element_type=F32)
                  for tb, gr, (jh, vh, d) in zip(tbs, g_r, chains)]
            for ci, (jh, vh, d) in enumerate(chains):
                u_scr[slot, ci] = us[ci]
                wq_scr[slot, ci, 0:CHUNK, :] = ws[ci].astype(BF16)
                kg = ks[jh, d].astype(F32) * jnp.exp(g_last[ci] - g_c[ci])
                ak_scr[slot, ci, CHUNK:CHUNK + HEAD_DIM, :] = kg.T.astype(BF16)
                dl_scr[slot, ci] = jnp.broadcast_to(jnp.exp(g_last[ci]), (1, HEAD_DIM))
                if write_out:
                    wq_scr[slot, ci, CHUNK:2 * CHUNK, :] = (qs[jh, d].astype(F32) * jnp.exp(g_c[ci])).astype(BF16)
                    ak_scr[slot, ci, 0:CHUNK, :] = (qks[jh, d] * dec[ci]).astype(BF16)

        def advance(i, slot):
            r0s = [chunk_of(i, d)[1] for d in range(2)]
            s_old = [s_scr[ci] for ci in range(N_CHAINS)]
            sbs = [s.astype(BF16) for s in s_old]
            rows = slice(0, 2 * CHUNK if write_out else CHUNK)
            r1 = [jnp.dot(wq_scr[slot, ci, rows, :], sbs[ci], preferred_element_type=F32) for ci in range(N_CHAINS)]
            vns = [(u_scr[slot, ci] - r1[ci][0:CHUNK]).astype(BF16) for ci in range(N_CHAINS)]
            rows = slice(0 if write_out else CHUNK, CHUNK + HEAD_DIM)
            r2 = [jnp.dot(ak_scr[slot, ci, rows, :], vns[ci], preferred_element_type=F32) for ci in range(N_CHAINS)]
            for ci, (jh, vh, d) in enumerate(chains):
                if write_out:
                    o_scr[pl.ds(r0s[d], CHUNK), value_cols(jh, vh)] += r1[ci][CHUNK:2 * CHUNK] + r2[ci][0:CHUNK]
                s_scr[ci] = s_old[ci] * dl_scr[slot, ci] + r2[ci][-HEAD_DIM:]

        prepare(0, 0)

        def pair_of_steps(m, carry):
            i = 2 * m
            advance(i, 0)
            prepare(i + 1, 1)
            advance(i + 1, 1)
            prepare(jnp.minimum(i + 2, n_chunks - 1), 0)
            return carry

        lax.fori_loop(0, n_chunks // 2, pair_of_steps, 0)

    run(qc_ref, kc_ref, vc_ref, gcc_ref, grc_ref, False)
    run(qx_ref, kx_ref, vx_ref, gcx_ref, grx_ref, True)
    o_ref[0] = o_scr[...].astype(o_ref.dtype)


def _gate_rows(gcol):
    b, t, n = gcol.shape
    g = gcol.reshape(b, t // CHUNK, CHUNK, n // GATE_LANES, GATE_LANES)
    return g.transpose(0, 3, 1, 4, 2)


def _dn_chunk(qc, kc, vc, gcc, qx, kx, vx, gcx):
    b, s, _ = qx.shape
    l = qc.shape[1]
    assert l % (2 * CHUNK) == 0 and s % (2 * CHUNK) == 0, "the chunk loop advances two chunks per trip"
    grc, grx = _gate_rows(gcc), _gate_rows(gcx)
    qk_cols = HEADS_PER_CALL * HEAD_DIM
    v_cols = 2 * qk_cols

    def seq_specs(t):
        return [pl.BlockSpec((1, t, qk_cols), lambda bi, j: (bi, 0, j)),
                pl.BlockSpec((1, t, qk_cols), lambda bi, j: (bi, 0, j)),
                pl.BlockSpec((1, t, v_cols), lambda bi, j: (bi, 0, j)),
                pl.BlockSpec((1, t, LANES), lambda bi, j: (bi, 0, 0)),
                pl.BlockSpec((1, HEADS_PER_CALL, t // CHUNK, GATE_LANES, CHUNK), lambda bi, j: (bi, j, 0, 0, 0))]

    return pl.pallas_call(
        _chunk_body,
        grid=(b, DN_QK_HEADS // HEADS_PER_CALL),
        in_specs=seq_specs(l) + seq_specs(s),
        out_specs=pl.BlockSpec((1, s, v_cols), lambda bi, j: (bi, 0, j)),
        out_shape=jax.ShapeDtypeStruct((b, s, DN_V_HEADS * HEAD_DIM), BF16),
        scratch_shapes=[pltpu.VMEM((N_CHAINS, HEAD_DIM, HEAD_DIM), F32),
                        pltpu.VMEM((s, v_cols), F32),
                        pltpu.VMEM((2, N_CHAINS, CHUNK, HEAD_DIM), F32),
                        pltpu.VMEM((2, N_CHAINS, 2 * CHUNK, HEAD_DIM), BF16),
                        pltpu.VMEM((2, N_CHAINS, CHUNK + HEAD_DIM, CHUNK), BF16),
                        pltpu.VMEM((2, N_CHAINS, 1, HEAD_DIM), F32)],
        compiler_params=_cparams("parallel", "parallel"),
        name="dn_chunk",
    )(qc, kc, vc, gcc, grc, qx, kx, vx, gcx, grx)


def _rot_cols(w):
    q = MLA_ROPE_DIM // 4
    return jnp.concatenate([-w[..., q:2 * q], w[..., 0:q], -w[..., 3 * q:4 * q], w[..., 2 * q:3 * q]], axis=-1)


def _rope_cos_sin(t_len):
    q = MLA_ROPE_DIM // 4
    pos = np.arange(t_len)
    inv = ROPE_BASE ** (-np.arange(q, dtype=np.float32) / q)
    ang_r = (pos // GRID_W).astype(np.float32)[:, None] * inv[None, :]
    ang_c = (pos % GRID_W).astype(np.float32)[:, None] * inv[None, :]
    ang = jnp.asarray(np.concatenate([ang_r, ang_r, ang_c, ang_c], axis=1))
    return jnp.cos(ang), jnp.sin(ang)


def _q_tables(cos, sin):
    t = cos.shape[0]
    one, zero = jnp.ones((t, HEAD_DIM), F32), jnp.zeros((t, MLA_ROPE_DIM), F32)
    c1 = jnp.concatenate([one, cos, zero], axis=1) * MLA_SCALE
    c2 = jnp.concatenate([jnp.zeros((t, HEAD_DIM), F32), sin, zero], axis=1) * MLA_SCALE
    return c1, c2


def _k_tables(cos, sin):
    zero = jnp.zeros_like(cos)
    return jnp.concatenate([cos, zero], axis=1), jnp.concatenate([sin, zero], axis=1)


def _mod_params(c, c_ctx, w_mod, b_mod):
    b, d = c.shape
    rows = ((b + 1 + 7) // 8) * 8
    cond = jnp.concatenate([c, c_ctx[None], jnp.zeros((rows - b - 1, d), F32)], axis=0)
    mod = _adaln(cond, w_mod, b_mod)
    sh, sc, gt = (mod[:b, i * d:(i + 1) * d].reshape(b, 1, d) for i in range(3))
    sh_c, sc_c, gt_c = (jnp.broadcast_to(mod[b, i * d:(i + 1) * d].reshape(1, 1, d), (b, 1, d)) for i in range(3))
    return (sh, sc, gt), (sh_c, sc_c, gt_c)


def _na_mla_layer(x, ctx, c, c_ctx, w_mod, b_mod, norm, w_in, rpb, q_norm, w_qb, kv_norm, w_kvb, w_out):
    b, s, d = x.shape
    l = ctx.shape[1]
    (sh, sc, gt), (sh_c, sc_c, gt_c) = _mod_params(c, c_ctx, w_mod, b_mod)
    na_w = NA_HEADS * HEAD_DIM
    n_main = 3 * na_w + 2 * MLA_RANK
    kr_w = w_in[:, n_main:n_main + MLA_ROPE_DIM]
    w_main = jnp.concatenate([w_in[:, :n_main], w_in[:, n_main + MLA_ROPE_DIM:]], axis=1).astype(BF16)
    w_small = jnp.concatenate([kr_w, _rot_cols(kr_w)], axis=1).astype(BF16)
    cos, sin = _rope_cos_sin(s)
    ident = (jnp.ones((l, MLA_ROPE_DIM), F32), jnp.zeros((l, MLA_ROPE_DIM), F32))
    hx, krx = _norm_proj(x, 0, d, norm, w_main, tm=IN_PROJ_ROWS, tn=1024, shift=sh, scale=sc, w_small=w_small,
                         rope_small=_k_tables(cos, sin), small_dtype=BF16, name="in_proj_ab_x")
    ident_flat = tuple(jnp.tile(t, (b, 1)) for t in ident)
    hc, krc = _norm_proj(ctx.reshape(1, b * l, d), 0, d, norm, w_main, tm=IN_PROJ_ROWS, tn=1024, shift=sh_c[:1],
                         scale=sc_c[:1], w_small=w_small, rope_small=_k_tables(*ident_flat), small_dtype=BF16,
                         name="in_proj_ab_ctx")
    hc, krc = hc.reshape(b, l, -1), krc.reshape(b, l, -1)

    wq = w_qb.reshape(MLA_RANK, MLA_HEADS, HEAD_DIM + MLA_ROPE_DIM)
    wq_r = wq[..., HEAD_DIM:]
    wq_p = jnp.concatenate([wq[..., :HEAD_DIM], wq_r, _rot_cols(wq_r)], axis=-1)
    wq_p = wq_p.reshape(MLA_RANK, MLA_HEADS * MLA_QK_PAD).astype(BF16)
    wkv = w_kvb.reshape(MLA_RANK, MLA_HEADS, 2 * HEAD_DIM)
    wkv_p = jnp.concatenate([wkv[..., :HEAD_DIM].reshape(MLA_RANK, -1), wkv[..., HEAD_DIM:].reshape(MLA_RANK, -1)],
                            axis=1).astype(BF16)
    cq_blk = 3 * na_w // MLA_RANK
    nq = MLA_HEADS * MLA_QK_PAD
    qx = _norm_proj(hx, cq_blk, MLA_RANK, q_norm, wq_p, tm=1024, tn=nq, rope_main=_q_tables(cos, sin), name="mla_q_x")
    qc = _norm_proj(hc, cq_blk, MLA_RANK, q_norm, wq_p, tm=1024, tn=nq, rope_main=_q_tables(*ident), name="mla_q_ctx")
    kvx = _norm_proj(hx, cq_blk + 1, MLA_RANK, kv_norm, wkv_p, tm=1024, tn=nq, name="mla_kv_x")
    kvc = _norm_proj(hc, cq_blk + 1, MLA_RANK, kv_norm, wkv_p, tm=1024, tn=nq, name="mla_kv_ctx")

    o_na = _na_attention(hx, hc, _na_bias_table(rpb, s // GRID_W))
    seg_x = (kvx, 0, krx, kvx, MLA_HEADS)
    seg_c = (kvc, 0, krc, kvc, MLA_HEADS)
    o_mla = _attention(qx, 0, MLA_QK_PAD, [seg_x, seg_c], MLA_HEADS, tq=512, scale=1.0, name="mla_attention")
    z_blk0 = n_main // na_w
    w_out_b = w_out.astype(BF16)
    x_new = _out_proj_ab(x, gt, o_na, o_mla, hx, z_blk0, w_out_b, tm=512)

    o_na_c = _attention(hc, 0, HEAD_DIM, [(hc, NA_HEADS, None, hc, 2 * NA_HEADS)], NA_HEADS, tq=l,
                        scale=HEAD_DIM ** -0.5, name="na_ctx_attention")
    o_mla_c = _attention(qc, 0, MLA_QK_PAD, [seg_c], MLA_HEADS, tq=l, scale=1.0, name="mla_ctx_attention")
    ctx_new = _out_proj_ab(ctx, gt_c, o_na_c, o_mla_c, hc, z_blk0, w_out_b, tm=512)
    return x_new, ctx_new


def _deltanet_final_layer(x, ctx, c, c_ctx, w_mod, b_mod, norm, w_in, conv_w, a_log, dt_bias, o_norm, w_out, final_norm):
    b, s, d = x.shape
    (sh, sc, gt), (sh_c, sc_c, _) = _mod_params(c, c_ctx, w_mod, b_mod)
    qk_w = DN_QK_HEADS * HEAD_DIM
    v_w = DN_V_HEADS * HEAD_DIM
    n_main = 2 * qk_w + 2 * v_w
    lane = np.arange(4 * DN_V_HEADS)
    perm = (lane % 4) * DN_V_HEADS + lane // 4
    w_main = w_in[:, :n_main].astype(BF16)
    w_small = w_in[:, n_main:][:, perm].astype(BF16)
    zeros = jnp.zeros((2 * DN_V_HEADS,), F32)
    alog = jnp.concatenate([zeros, a_log.reshape(-1).astype(F32)])[perm].reshape(1, -1)
    dtb = jnp.concatenate([zeros, dt_bias.reshape(-1).astype(F32)])[perm].reshape(1, -1)
    conv_w = conv_w.astype(F32)

    def inputs(t, shift, scale, name, shared_modulation=False):
        bt, tl, _ = t.shape
        if shared_modulation:
            t, shift, scale = t.reshape(1, bt * tl, d), shift[:1], scale[:1]
        h, hs = _norm_proj(t, 0, d, norm, w_main, tm=IN_PROJ_ROWS, tn=1024, shift=shift, scale=scale, w_small=w_small,
                           name=name)
        h, hs = h.reshape(bt, tl, -1), hs.reshape(bt, tl, -1)
        q = _dn_conv(h, conv_w, 0, qk_w, HEAD_DIM ** -0.5, tt=512)
        k = _dn_conv(h, conv_w, qk_w, qk_w, 1.0, tt=512)
        v = _dn_conv(h, conv_w, 2 * qk_w, v_w, None, tt=512)
        return h, q, k, v, _dn_gates(hs, alog, dtb)

    hx, qx, kx, vx, gx = inputs(x, sh, sc, "in_proj_dn_x")
    _, qc, kc, vc, gc = inputs(ctx, sh_c, sc_c, "in_proj_dn_ctx", shared_modulation=True)
    o = _dn_chunk(qc, kc, vc, gc, qx, kx, vx, gx)
    return _out_proj_dn(x, gt, o, hx, (2 * qk_w + v_w) // v_w, o_norm, w_out.astype(BF16), final_norm, tm=256)


def kernel(x, c, ctx, c_ctx, ab_w_mod, ab_b_mod, ab_norm, ab_w_in, ab_rpb, ab_q_norm, ab_w_qb, ab_kv_norm, ab_w_kvb, ab_w_out, dn_w_mod, dn_b_mod, dn_norm, dn_w_in, dn_conv, dn_a_log, dn_dt_bias, dn_o_norm, dn_w_out, final_norm):
    assert ab_w_mod.shape[0] == 1 and dn_w_mod.shape[0] == 1, "depth-2 trunk: one even and one odd layer"
    x, ctx = _na_mla_layer(x, ctx, c, c_ctx, ab_w_mod[0], ab_b_mod[0], ab_norm[0], ab_w_in[0], ab_rpb[0],
                           ab_q_norm[0], ab_w_qb[0], ab_kv_norm[0], ab_w_kvb[0], ab_w_out[0])
    return _deltanet_final_layer(x, ctx, c, c_ctx, dn_w_mod[0], dn_b_mod[0], dn_norm[0], dn_w_in[0], dn_conv[0],
                                 dn_a_log[0], dn_dt_bias[0], dn_o_norm[0], dn_w_out[0], final_norm)
```

```python
import functools

import numpy as np
import jax
import jax.numpy as jnp
from jax import lax
from jax.experimental import pallas as pl
from jax.experimental.pallas import tpu as pltpu

F32 = jnp.float32
BF16 = jnp.bfloat16

EPS = 1e-6
GRID_W = 64
WIN_R = 8
WIN_C = 16
NA_HEADS = 8
HEAD_DIM = 128
MLA_HEADS = 8
MLA_RANK = 512
MLA_ROPE_DIM = 64
MLA_SCALE = (HEAD_DIM + MLA_ROPE_DIM) ** -0.5
ROPE_BASE = 10000.0
DN_QK_HEADS = 16
DN_V_HEADS = 32
CONV_K = 5
CHUNK = 128
NEG_BIG = -1e30

LANES = 128
MLA_QK_PAD = 256
NA_Q_ROWS = 8
NA_KEY_ROWS = 16
ATTN_SUB_ROWS = 256
IN_PROJ_ROWS = 1024
VMEM_LIMIT = 56 * 1024 * 1024


def _cparams(*sem):
    return pltpu.CompilerParams(dimension_semantics=sem, vmem_limit_bytes=VMEM_LIMIT)


def _adaln_body(c_ref, w_ref, b_ref, o_ref):
    cf = c_ref[...]
    a = (cf * jax.nn.sigmoid(cf)).astype(BF16)
    o_ref[...] = jnp.dot(a, w_ref[...].astype(BF16), preferred_element_type=F32) + b_ref[...]


def _adaln(cond, w_mod, b_mod):
    r, d = cond.shape
    n = w_mod.shape[1]
    tn = 512
    return pl.pallas_call(
        _adaln_body,
        grid=(n // tn,),
        in_specs=[pl.BlockSpec((r, d), lambda j: (0, 0)),
                  pl.BlockSpec((d, tn), lambda j: (0, j)),
                  pl.BlockSpec((1, tn), lambda j: (0, j))],
        out_specs=pl.BlockSpec((r, tn), lambda j: (0, j)),
        out_shape=jax.ShapeDtypeStruct((r, n), F32),
        compiler_params=_cparams("parallel"),
        name="adaln",
    )(cond, w_mod, b_mod.reshape(1, n))


def _proj_body(*refs, modulate, has_small, rope_main, rope_small):
    it = iter(refs)
    x_ref, gain_ref = next(it), next(it)
    shift_ref = scale_ref = ws_ref = c1_ref = c2_ref = s1_ref = s2_ref = os_ref = None
    if modulate:
        shift_ref, scale_ref = next(it), next(it)
    w_ref = next(it)
    if has_small:
        ws_ref = next(it)
    if rope_main:
        c1_ref, c2_ref = next(it), next(it)
    if rope_small:
        s1_ref, s2_ref = next(it), next(it)
    o_ref = next(it)
    if has_small:
        os_ref = next(it)
    xm_ref = next(it)

    @pl.when(pl.program_id(2) == 0)
    def _():
        xf = x_ref[0].astype(F32)
        ms = jnp.mean(xf * xf, axis=-1, keepdims=True)
        y = xf * lax.rsqrt(ms + EPS) * gain_ref[...]
        if modulate:
            y = y * (1.0 + scale_ref[0]) + shift_ref[0]
        xm = y.astype(BF16)
        xm_ref[...] = xm
        if has_small:
            s = jnp.dot(xm, ws_ref[...], preferred_element_type=F32)
            if rope_small:
                s = s * s1_ref[...] + pltpu.roll(s, LANES // 2, axis=1) * s2_ref[...]
            os_ref[0] = s.astype(os_ref.dtype)

    acc = jnp.dot(xm_ref[...], w_ref[...], preferred_element_type=F32)
    if rope_main:
        for h in range(acc.shape[1] // MLA_QK_PAD):
            a = acc[:, h * MLA_QK_PAD:(h + 1) * MLA_QK_PAD]
            a = a * c1_ref[...] + pltpu.roll(a, MLA_QK_PAD - MLA_ROPE_DIM, axis=1) * c2_ref[...]
            o_ref[0, :, h * MLA_QK_PAD:(h + 1) * MLA_QK_PAD] = a.astype(o_ref.dtype)
    else:
        o_ref[0] = acc.astype(o_ref.dtype)


def _norm_proj(x, kblk, k, gain, w, *, tm, tn, shift=None, scale=None, w_small=None,
               rope_main=None, rope_small=None, small_dtype=F32, name="proj"):
    b, t, _ = x.shape
    n = w.shape[1]
    modulate = shift is not None
    has_small = w_small is not None
    tm = min(tm, t)
    args = [x, gain.reshape(1, k).astype(F32)]
    in_specs = [pl.BlockSpec((1, tm, k), lambda bi, i, j: (bi, i, kblk)),
                pl.BlockSpec((1, k), lambda bi, i, j: (0, 0))]
    if modulate:
        args += [shift, scale]
        in_specs += [pl.BlockSpec((1, 1, k), lambda bi, i, j: (bi, 0, 0))] * 2
    args.append(w)
    in_specs.append(pl.BlockSpec((k, tn), lambda bi, i, j: (0, j)))
    if has_small:
        ns = w_small.shape[1]
        args.append(w_small)
        in_specs.append(pl.BlockSpec((k, ns), lambda bi, i, j: (0, 0)))
    if rope_main is not None:
        args += list(rope_main)
        in_specs += [pl.BlockSpec((tm, MLA_QK_PAD), lambda bi, i, j: (i, 0))] * 2
    if rope_small is not None:
        args += list(rope_small)
        in_specs += [pl.BlockSpec((tm, LANES), lambda bi, i, j: (i, 0))] * 2
    out_shape = [jax.ShapeDtypeStruct((b, t, n), BF16)]
    out_specs = [pl.BlockSpec((1, tm, tn), lambda bi, i, j: (bi, i, j))]
    if has_small:
        out_shape.append(jax.ShapeDtypeStruct((b, t, ns), small_dtype))
        out_specs.append(pl.BlockSpec((1, tm, ns), lambda bi, i, j: (bi, i, 0)))
    body = functools.partial(_proj_body, modulate=modulate, has_small=has_small,
                             rope_main=rope_main is not None, rope_small=rope_small is not None)
    out = pl.pallas_call(
        body,
        grid=(b, t // tm, n // tn),
        in_specs=in_specs,
        out_specs=out_specs,
        out_shape=out_shape,
        scratch_shapes=[pltpu.VMEM((tm, k), BF16)],
        compiler_params=_cparams("parallel", "parallel", "arbitrary"),
        name=name,
    )(*args)
    return out if has_small else out[0]


def _attn_body(*refs, seg_lens, has_extra, scale):
    it = iter(refs)
    q_ref = next(it)
    segs = []
    for _ in seg_lens:
        km = next(it)
        ke = next(it) if has_extra else None
        segs.append((km, ke, next(it)))
    o_ref, k_scr, v_scr = next(it), next(it), next(it)

    @pl.when(pl.program_id(2) == 0)
    def _():
        off = 0
        for (km, ke, v), ln in zip(segs, seg_lens):
            k_scr[off:off + ln, 0:HEAD_DIM] = km[0]
            if has_extra:
                k_scr[off:off + ln, HEAD_DIM:2 * HEAD_DIM] = ke[0]
            v_scr[off:off + ln, :] = v[0]
            off += ln

    sub = min(ATTN_SUB_ROWS, q_ref.shape[1])
    n_sub = q_ref.shape[1] // sub

    def scores(j):
        s = lax.dot_general(q_ref[0, j * sub:(j + 1) * sub, :], k_scr[...], (((1,), (1,)), ((), ())),
                            preferred_element_type=F32)
        return s if scale == 1.0 else s * scale

    s_next = scores(0)
    for j in range(n_sub):
        s = s_next
        if j + 1 < n_sub:
            s_next = scores(j + 1)
        m = jnp.max(s, axis=-1, keepdims=True)
        e = jnp.exp(s - m)
        l = jnp.sum(e, axis=-1, keepdims=True)
        o = jnp.dot(e.astype(BF16), v_scr[...], preferred_element_type=F32)
        o_ref[0, j * sub:(j + 1) * sub, :] = (o / l).astype(o_ref.dtype)


def _attention(q, q_blk0, dq, segs, heads, *, tq, scale, name):
    b, t, _ = q.shape
    tq = min(tq, t)
    has_extra = segs[0][2] is not None
    seg_lens = tuple(s[0].shape[1] for s in segs)
    tk = sum(seg_lens)
    args = [q]
    in_specs = [pl.BlockSpec((1, tq, dq), lambda bi, h, i: (bi, i, q_blk0 + h))]
    for km, kb0, ke, v, vb0 in segs:
        ln = km.shape[1]
        args.append(km)
        in_specs.append(pl.BlockSpec((1, ln, HEAD_DIM), lambda bi, h, i, kb0=kb0: (bi, 0, kb0 + h)))
        if has_extra:
            args.append(ke)
            in_specs.append(pl.BlockSpec((1, ln, HEAD_DIM), lambda bi, h, i: (bi, 0, 0)))
        args.append(v)
        in_specs.append(pl.BlockSpec((1, ln, HEAD_DIM), lambda bi, h, i, vb0=vb0: (bi, 0, vb0 + h)))
    body = functools.partial(_attn_body, seg_lens=seg_lens, has_extra=has_extra, scale=scale)
    return pl.pallas_call(
        body,
        grid=(b, heads, t // tq),
        in_specs=in_specs,
        out_specs=pl.BlockSpec((1, tq, HEAD_DIM), lambda bi, h, i: (bi, i, h)),
        out_shape=jax.ShapeDtypeStruct((b, t, heads * HEAD_DIM), BF16),
        scratch_shapes=[pltpu.VMEM((tk, dq), BF16), pltpu.VMEM((tk, HEAD_DIM), BF16)],
        compiler_params=_cparams("parallel", "parallel", "arbitrary"),
        name=name,
    )(*args)


def _na_geometry(rows):
    groups = rows // NA_Q_ROWS
    key_rows = min(NA_KEY_ROWS, rows)
    return groups, key_rows


def _na_bias_table(rpb, rows):
    heads = rpb.shape[0]
    groups, key_rows = _na_geometry(rows)
    g = np.arange(groups)
    slab0 = np.clip(NA_Q_ROWS * g - WIN_R // 2, 0, rows - key_rows)
    r = NA_Q_ROWS * g[:, None] + np.arange(NA_Q_ROWS)[None]
    k_abs = slab0[:, None] + np.arange(key_rows)[None]
    win0 = np.clip(r - WIN_R // 2, 0, rows - WIN_R)
    d_row = k_abs[:, None, :] - r[:, :, None] + (WIN_R - 1)
    ok_row = (k_abs[:, None, :] >= win0[:, :, None]) & (k_abs[:, None, :] < win0[:, :, None] + WIN_R)
    col = np.arange(GRID_W)
    c0 = np.clip(col - WIN_C // 2, 0, GRID_W - WIN_C)
    d_col = col[None, :] - col[:, None] + (WIN_C - 1)
    ok_col = (col[None, :] >= c0[:, None]) & (col[None, :] < c0[:, None] + WIN_C)
    sel_row = (np.clip(d_row, 0, 2 * WIN_R - 2)[..., None] == np.arange(2 * WIN_R - 1)).astype(np.float32)
    sel_col = (np.clip(d_col, 0, 2 * WIN_C - 2)[..., None] == np.arange(2 * WIN_C - 1)).astype(np.float32)
    ok = jnp.asarray(ok_row)[:, :, None, :, None] & jnp.asarray(ok_col)[None, None, :, None, :]
    t = jnp.einsum("grka,hab,qcb->hgrqkc", jnp.asarray(sel_row), rpb.astype(F32), jnp.asarray(sel_col),
                   precision=lax.Precision.HIGHEST)
    t = jnp.where(ok[None], t, NEG_BIG)
    return t.reshape(heads, groups, NA_Q_ROWS * GRID_W, key_rows * GRID_W)


def _na_body(q_ref, k_ref, v_ref, kc_ref, vc_ref, bias_ref, o_ref, *, rows, scale):
    groups, key_rows = _na_geometry(rows)
    g = pl.program_id(2)
    slab0 = jnp.clip(NA_Q_ROWS * g - WIN_R // 2, 0, rows - key_rows)
    start = pl.multiple_of(slab0 * GRID_W, 4 * GRID_W)
    nk = key_rows * GRID_W
    nt = (((1,), (1,)), ((), ()))
    sub = min(ATTN_SUB_ROWS, q_ref.shape[1])
    n_sub = q_ref.shape[1] // sub

    def scores(j):
        q = q_ref[0, j * sub:(j + 1) * sub, :]
        s_nb = lax.dot_general(q, k_ref[0, pl.ds(start, nk), :], nt, preferred_element_type=F32)
        s_cx = lax.dot_general(q, kc_ref[0], nt, preferred_element_type=F32)
        return s_nb, s_cx

    s_next = scores(0)
    for j in range(n_sub):
        s_nb, s_cx = s_next
        if j + 1 < n_sub:
            s_next = scores(j + 1)
        s_nb = s_nb * scale + bias_ref[0, g, j * sub:(j + 1) * sub, :]
        s_cx = s_cx * scale
        m = jnp.maximum(jnp.max(s_nb, axis=-1, keepdims=True), jnp.max(s_cx, axis=-1, keepdims=True))
        e_nb = jnp.exp(s_nb - m)
        e_cx = jnp.exp(s_cx - m)
        l = jnp.sum(e_nb, axis=-1, keepdims=True) + jnp.sum(e_cx, axis=-1, keepdims=True)
        o = jnp.dot(e_nb.astype(BF16), v_ref[0, pl.ds(start, nk), :], preferred_element_type=F32)
        o = o + jnp.dot(e_cx.astype(BF16), vc_ref[0], preferred_element_type=F32)
        o_ref[0, j * sub:(j + 1) * sub, :] = (o / l).astype(o_ref.dtype)


def _na_attention(hx, hc, bias):
    b, s, _ = hx.shape
    l = hc.shape[1]
    rows = s // GRID_W
    groups, key_rows = _na_geometry(rows)
    tq = NA_Q_ROWS * GRID_W
    body = functools.partial(_na_body, rows=rows, scale=HEAD_DIM ** -0.5)
    return pl.pallas_call(
        body,
        grid=(NA_HEADS, b, groups),
        in_specs=[pl.BlockSpec((1, tq, HEAD_DIM), lambda h, bi, g: (bi, g, h)),
                  pl.BlockSpec((1, s, HEAD_DIM), lambda h, bi, g: (bi, 0, NA_HEADS + h)),
                  pl.BlockSpec((1, s, HEAD_DIM), lambda h, bi, g: (bi, 0, 2 * NA_HEADS + h)),
                  pl.BlockSpec((1, l, HEAD_DIM), lambda h, bi, g: (bi, 0, NA_HEADS + h)),
                  pl.BlockSpec((1, l, HEAD_DIM), lambda h, bi, g: (bi, 0, 2 * NA_HEADS + h)),
                  pl.BlockSpec((1, groups, tq, key_rows * GRID_W), lambda h, bi, g: (h, 0, 0, 0))],
        out_specs=pl.BlockSpec((1, tq, HEAD_DIM), lambda h, bi, g: (bi, g, h)),
        out_shape=jax.ShapeDtypeStruct((b, s, NA_HEADS * HEAD_DIM), BF16),
        compiler_params=_cparams("parallel", "parallel", "parallel"),
        name="na_attention",
    )(hx, hx, hx, hc, hc, bias)


def _silu(x):
    return x * jax.nn.sigmoid(x)


def _out_ab_body(x_ref, gt_ref, o1_ref, o2_ref, z1_ref, z2_ref, w_ref, y_ref):
    half = o1_ref.shape[2]
    a1 = (o1_ref[0].astype(F32) * _silu(z1_ref[0].astype(F32))).astype(BF16)
    acc = jnp.dot(a1, w_ref[0:half, :], preferred_element_type=F32)
    a2 = (o2_ref[0].astype(F32) * _silu(z2_ref[0].astype(F32))).astype(BF16)
    acc = acc + jnp.dot(a2, w_ref[half:2 * half, :], preferred_element_type=F32)
    y_ref[0] = x_ref[0] + gt_ref[0] * acc


def _out_proj_ab(x, gt, o1, o2, h, z_blk0, w, *, tm):
    b, t, d = x.shape
    half = o1.shape[2]
    tm = min(tm, t)
    return pl.pallas_call(
        _out_ab_body,
        grid=(b, t // tm),
        in_specs=[pl.BlockSpec((1, tm, d), lambda bi, i: (bi, i, 0)),
                  pl.BlockSpec((1, 1, d), lambda bi, i: (bi, 0, 0)),
                  pl.BlockSpec((1, tm, half), lambda bi, i: (bi, i, 0)),
                  pl.BlockSpec((1, tm, half), lambda bi, i: (bi, i, 0)),
                  pl.BlockSpec((1, tm, half), lambda bi, i: (bi, i, z_blk0)),
                  pl.BlockSpec((1, tm, half), lambda bi, i: (bi, i, z_blk0 + 1)),
                  pl.BlockSpec((2 * half, d), lambda bi, i: (0, 0))],
        out_specs=pl.BlockSpec((1, tm, d), lambda bi, i: (bi, i, 0)),
        out_shape=jax.ShapeDtypeStruct((b, t, d), F32),
        compiler_params=_cparams("parallel", "parallel"),
        name="out_proj_ab",
    )(x, gt, o1, o2, h, h, w)


def _out_dn_body(x_ref, gt_ref, o_ref, z_ref, on_ref, w_ref, fn_ref, y_ref, a_scr):
    heads = o_ref.shape[2] // HEAD_DIM
    parts = 2
    acc = None
    for p in range(parts):
        for h in range(p * heads // parts, (p + 1) * heads // parts):
            sl = slice(h * HEAD_DIM, (h + 1) * HEAD_DIM)
            oh = o_ref[0, :, sl].astype(F32)
            ms = jnp.mean(oh * oh, axis=-1, keepdims=True)
            a = oh * lax.rsqrt(ms + EPS) * on_ref[...] * _silu(z_ref[0, :, sl].astype(F32))
            a_scr[:, sl] = a.astype(BF16)
        cols = slice(p * heads // parts * HEAD_DIM, (p + 1) * heads // parts * HEAD_DIM)
        part = jnp.dot(a_scr[:, cols], w_ref[cols, :], preferred_element_type=F32)
        acc = part if acc is None else acc + part
    y = x_ref[0] + gt_ref[0] * acc
    ms = jnp.mean(y * y, axis=-1, keepdims=True)
    y_ref[0] = y * lax.rsqrt(ms + EPS) * fn_ref[...]


def _out_proj_dn(x, gt, o, h, z_blk, o_norm, w, final_norm, *, tm):
    b, t, d = x.shape
    km = o.shape[2]
    tm = min(tm, t)
    return pl.pallas_call(
        _out_dn_body,
        grid=(b, t // tm),
        in_specs=[pl.BlockSpec((1, tm, d), lambda bi, i: (bi, i, 0)),
                  pl.BlockSpec((1, 1, d), lambda bi, i: (bi, 0, 0)),
                  pl.BlockSpec((1, tm, km), lambda bi, i: (bi, i, 0)),
                  pl.BlockSpec((1, tm, km), lambda bi, i: (bi, i, z_blk)),
                  pl.BlockSpec((1, HEAD_DIM), lambda bi, i: (0, 0)),
                  pl.BlockSpec((km, d), lambda bi, i: (0, 0)),
                  pl.BlockSpec((1, d), lambda bi, i: (0, 0))],
        out_specs=pl.BlockSpec((1, tm, d), lambda bi, i: (bi, i, 0)),
        out_shape=jax.ShapeDtypeStruct((b, t, d), F32),
        scratch_shapes=[pltpu.VMEM((tm, km), BF16)],
        compiler_params=_cparams("parallel", "parallel"),
        name="out_proj_dn",
    )(x, gt, o, h, o_norm.reshape(1, HEAD_DIM).astype(F32), w, final_norm.reshape(1, d).astype(F32))


CONV_HALO = 16


SUBLANES = 8
SEG_PAD = 8


def _conv_body(xp_ref, x_ref, xn_ref, w_ref, o_ref, in_scr, out_scr, *, l2_scale):
    i = pl.program_id(1)
    nt = pl.num_programs(1)
    tt = x_ref.shape[1]
    seg = tt // SUBLANES
    pitch = seg + SEG_PAD
    half = CONV_K // 2
    prev = jnp.where(i > 0, xp_ref[0].astype(F32), 0.0)
    nxt = jnp.where(i < nt - 1, xn_ref[0].astype(F32), 0.0)
    for sl in range(x_ref.shape[2] // LANES):
        lanes = slice(sl * LANES, (sl + 1) * LANES)
        in_scr[sl, seg - SUBLANES:seg, :] = prev[CONV_HALO - SUBLANES:, lanes]
        for s in range(SUBLANES):
            in_scr[sl, (s + 1) * pitch:(s + 1) * pitch + seg, :] = x_ref[0, s * seg:(s + 1) * seg, lanes].astype(F32)
        in_scr[sl, (SUBLANES + 1) * pitch:(SUBLANES + 1) * pitch + SUBLANES, :] = nxt[:SUBLANES, lanes]
        w = [w_ref[j:j + 1, lanes] for j in range(CONV_K)]

        def strided(n):
            start = pitch + n - (SEG_PAD if n < 0 else 0) + (SEG_PAD if n >= seg else 0)
            return in_scr[sl, pl.ds(start, SUBLANES, stride=pitch), :]

        window = [strided(n) for n in range(-half, half)]
        for m in range(seg):
            window.append(strided(m + half))
            acc = window[0] * w[0]
            for j in range(1, CONV_K):
                acc = acc + window[j] * w[j]
            window.pop(0)
            y = _silu(acc)
            if l2_scale is not None:
                y = y * (lax.rsqrt(jnp.sum(y * y, axis=-1, keepdims=True) + EPS) * l2_scale)
            out_scr[sl, pl.ds(m, SUBLANES, stride=pitch), :] = y
        for s in range(SUBLANES):
            o_ref[0, s * seg:(s + 1) * seg, lanes] = out_scr[sl, s * pitch:s * pitch + seg, :].astype(o_ref.dtype)


def _dn_conv(h, conv_w, col0, width, l2_scale, *, tt, cb=512):
    b, t, _ = h.shape
    tt = min(tt, t)
    blk0 = col0 // cb
    hb = tt // CONV_HALO
    nhalo = t // CONV_HALO
    pitch = tt // SUBLANES + SEG_PAD
    body = functools.partial(_conv_body, l2_scale=l2_scale)
    return pl.pallas_call(
        body,
        grid=(b, t // tt, width // cb),
        in_specs=[pl.BlockSpec((1, CONV_HALO, cb), lambda bi, i, c: (bi, jnp.maximum(i * hb - 1, 0), blk0 + c)),
                  pl.BlockSpec((1, tt, cb), lambda bi, i, c: (bi, i, blk0 + c)),
                  pl.BlockSpec((1, CONV_HALO, cb), lambda bi, i, c: (bi, jnp.minimum((i + 1) * hb, nhalo - 1), blk0 + c)),
                  pl.BlockSpec((CONV_K, cb), lambda bi, i, c: (0, blk0 + c))],
        out_specs=pl.BlockSpec((1, tt, cb), lambda bi, i, c: (bi, i, c)),
        out_shape=jax.ShapeDtypeStruct((b, t, width), BF16),
        scratch_shapes=[pltpu.VMEM((cb // LANES, (SUBLANES + 2) * pitch, LANES), F32),
                        pltpu.VMEM((cb // LANES, SUBLANES * pitch, LANES), F32)],
        compiler_params=_cparams("parallel", "parallel", "parallel"),
        name="dn_conv",
    )(h, h, h, conv_w)


GATE_TILE = 256


def _split3(x):
    hi = x.astype(BF16)
    r1 = x - hi.astype(F32)
    mid = r1.astype(BF16)
    lo = (r1 - mid.astype(F32)).astype(BF16)
    return hi, mid, lo


def _gates_body(h_ref, alog_ref, dtb_ref, o_ref):
    t = h_ref[0]
    lane = lax.broadcasted_iota(jnp.int32, t.shape, 1) % 4
    beta = jax.nn.sigmoid(t)
    z = t + dtb_ref[...]
    softplus = jnp.maximum(z, 0.0) + jnp.log1p(jnp.exp(-jnp.abs(z)))
    g = -jnp.exp(alog_ref[...]) * softplus
    n = t.shape[0]
    r = lax.broadcasted_iota(jnp.int32, (n, n), 0)
    c = lax.broadcasted_iota(jnp.int32, (n, n), 1)
    same = (r // CHUNK) == (c // CHUNK)
    tri_f = jnp.where(same & (r >= c), 1.0, 0.0).astype(BF16)
    tri_b = jnp.where(same & (r <= c), 1.0, 0.0).astype(BF16)
    pf = pb = None
    for part in _split3(g):
        df = jnp.dot(tri_f, part, preferred_element_type=F32)
        db = jnp.dot(tri_b, part, preferred_element_type=F32)
        pf = df if pf is None else pf + df
        pb = db if pb is None else pb + db
    o_ref[0] = jnp.where(lane < 2, beta, jnp.where(lane == 2, pf, pb))


def _dn_gates(h_small, alog, dtb):
    b, t, n = h_small.shape
    tt = min(GATE_TILE, t)
    return pl.pallas_call(
        _gates_body,
        grid=(b, t // tt),
        in_specs=[pl.BlockSpec((1, tt, n), lambda bi, i: (bi, i, 0)),
                  pl.BlockSpec((1, n), lambda bi, i: (0, 0)),
                  pl.BlockSpec((1, n), lambda bi, i: (0, 0))],
        out_specs=pl.BlockSpec((1, tt, n), lambda bi, i: (bi, i, 0)),
        out_shape=jax.ShapeDtypeStruct((b, t, n), F32),
        compiler_params=_cparams("parallel", "parallel"),
        name="dn_gates",
    )(h_small, alog, dtb)


HEADS_PER_CALL = 4
N_CHAINS = 4 * HEADS_PER_CALL
GATE_LANES = 8
MERGE_LEVELS = tuple(4 << i for i in range((CHUNK // 4).bit_length() - 1))


def _bdot(a, b):
    return jnp.dot(a.astype(BF16), b.astype(BF16), preferred_element_type=F32)


def _unit_tri_inverses(lms, eye, level):
    l4s = [jnp.where(level == 0, lm, 0.0) for lm in lms]
    ts = [eye - l4 for l4 in l4s]
    sq = [_bdot(l4, l4) for l4 in l4s]
    ts = [t + _bdot(t, p) for t, p in zip(ts, sq)]
    for m in range(1, len(MERGE_LEVELS) + 1):
        cs = [jnp.where(level == m, lm, 0.0).astype(BF16) for lm in lms]
        tbs = [t.astype(BF16) for t in ts]
        xs = [jnp.dot(tb, c, preferred_element_type=F32) for tb, c in zip(tbs, cs)]
        ts = [t - _bdot(x, tb) for t, x, tb in zip(ts, xs, tbs)]
    return ts


def _chunk_body(qc_ref, kc_ref, vc_ref, gcc_ref, grc_ref, qx_ref, kx_ref, vx_ref, gcx_ref, grx_ref,
                o_ref, s_scr, o_scr, u_scr, wq_scr, ak_scr, dl_scr):
    jg = pl.program_id(1)
    group_lanes = GATE_LANES * HEADS_PER_CALL
    lane_shift = (LANES - group_lanes * jg) % LANES
    s_scr[...] = jnp.zeros_like(s_scr)
    o_scr[...] = jnp.zeros_like(o_scr)
    row = lax.broadcasted_iota(jnp.int32, (CHUNK, CHUNK), 0)
    col = lax.broadcasted_iota(jnp.int32, (CHUNK, CHUNK), 1)
    eye = jnp.where(row == col, 1.0, 0.0)
    level = jnp.zeros((CHUNK, CHUNK), jnp.int32)
    for size in MERGE_LEVELS:
        level = level + jnp.where((row // size) == (col // size), 0, 1)
    incl = (row >= col, row <= col)
    strict = (row > col, row < col)
    nt = (((1,), (1,)), ((), ()))
    chains = [(jh, vh, d) for d in range(2) for jh in range(HEADS_PER_CALL) for vh in range(2)]
    streams = [(jh, d) for d in range(2) for jh in range(HEADS_PER_CALL)]

    def head_cols(jh):
        return slice(jh * HEAD_DIM, (jh + 1) * HEAD_DIM)

    def value_cols(jh, vh):
        return slice((2 * jh + vh) * HEAD_DIM, (2 * jh + vh + 1) * HEAD_DIM)

    def run(q_ref, k_ref, v_ref, gc_ref, gr_ref, write_out):
        n_chunks = q_ref.shape[1] // CHUNK

        def chunk_of(i, d):
            n = i if d == 0 else n_chunks - 1 - i
            return n, pl.multiple_of(n * CHUNK, CHUNK)

        def prepare(i, slot):
            ns, r0s, gcols = {}, {}, {}
            for d in range(2):
                ns[d], r0s[d] = chunk_of(i, d)
                gcols[d] = pltpu.roll(gc_ref[0, pl.ds(r0s[d], CHUNK), :], lane_shift, axis=1)
            ks = {(jh, d): k_ref[0, pl.ds(r0s[d], CHUNK), head_cols(jh)] for jh, d in streams}
            kks = {sd: lax.dot_general(ks[sd], ks[sd], nt, preferred_element_type=F32) for sd in streams}
            if write_out:
                qs = {(jh, d): q_ref[0, pl.ds(r0s[d], CHUNK), head_cols(jh)] for jh, d in streams}
                qks = {sd: lax.dot_general(qs[sd], ks[sd], nt, preferred_element_type=F32) for sd in streams}
            grows = {(jh, d): gr_ref[0, jh, ns[d]] for jh, d in streams}
            lane0 = [GATE_LANES * jh + 4 * vh + d for jh, vh, d in chains]
            beta_c = [gcols[d][:, l:l + 1] for l, (jh, vh, d) in zip(lane0, chains)]
            g_c = [gcols[d][:, l + 2:l + 3] for l, (jh, vh, d) in zip(lane0, chains)]
            beta_r = [grows[jh, d][4 * vh + d:4 * vh + d + 1, :] for jh, vh, d in chains]
            g_r = [grows[jh, d][4 * vh + 2 + d:4 * vh + 3 + d, :] for jh, vh, d in chains]
            g_last = [g[:, CHUNK - 1:CHUNK] if d == 0 else g[:, 0:1] for g, (jh, vh, d) in zip(g_r, chains)]
            dec = [jnp.where(incl[d], jnp.exp(jnp.where(incl[d], gc - gr, 0.0)), 0.0)
                   for gc, gr, (jh, vh, d) in zip(g_c, g_r, chains)]
            lms = [jnp.where(strict[d], kks[jh, d] * dc, 0.0) * bc
                   for dc, bc, (jh, vh, d) in zip(dec, beta_c, chains)]
            tbs = [t * br for t, br in zip(_unit_tri_inverses(lms, eye, level), beta_r)]
            us = [jnp.dot(tb.astype(BF16), v_ref[0, pl.ds(r0s[d], CHUNK), value_cols(jh, vh)],
                          preferred_element_type=F32) for tb, (jh, vh, d) in zip(tbs, chains)]
            ws = [jnp.dot((tb * jnp.exp(gr)).astype(BF16), ks[jh, d], preferred_element_type=F32)
                  for tb, gr, (jh, vh, d) in zip(tbs, g_r, chains)]
            for ci, (jh, vh, d) in enumerate(chains):
                u_scr[slot, ci] = us[ci]
                wq_scr[slot, ci, 0:CHUNK, :] = ws[ci].astype(BF16)
                kg = ks[jh, d].astype(F32) * jnp.exp(g_last[ci] - g_c[ci])
                ak_scr[slot, ci, CHUNK:CHUNK + HEAD_DIM, :] = kg.T.astype(BF16)
                dl_scr[slot, ci] = jnp.broadcast_to(jnp.exp(g_last[ci]), (1, HEAD_DIM))
                if write_out:
                    wq_scr[slot, ci, CHUNK:2 * CHUNK, :] = (qs[jh, d].astype(F32) * jnp.exp(g_c[ci])).astype(BF16)
                    ak_scr[slot, ci, 0:CHUNK, :] = (qks[jh, d] * dec[ci]).astype(BF16)

        def advance(i, slot):
            r0s = [chunk_of(i, d)[1] for d in range(2)]
            s_old = [s_scr[ci] for ci in range(N_CHAINS)]
            sbs = [s.astype(BF16) for s in s_old]
            rows = slice(0, 2 * CHUNK if write_out else CHUNK)
            r1 = [jnp.dot(wq_scr[slot, ci, rows, :], sbs[ci], preferred_element_type=F32) for ci in range(N_CHAINS)]
            vns = [(u_scr[slot, ci] - r1[ci][0:CHUNK]).astype(BF16) for ci in range(N_CHAINS)]
            rows = slice(0 if write_out else CHUNK, CHUNK + HEAD_DIM)
            r2 = [jnp.dot(ak_scr[slot, ci, rows, :], vns[ci], preferred_element_type=F32) for ci in range(N_CHAINS)]
            for ci, (jh, vh, d) in enumerate(chains):
                if write_out:
                    o_scr[pl.ds(r0s[d], CHUNK), value_cols(jh, vh)] += r1[ci][CHUNK:2 * CHUNK] + r2[ci][0:CHUNK]
                s_scr[ci] = s_old[ci] * dl_scr[slot, ci] + r2[ci][-HEAD_DIM:]

        prepare(0, 0)

        def pair_of_steps(m, carry):
            i = 2 * m
            advance(i, 0)
            prepare(i + 1, 1)
            advance(i + 1, 1)
            prepare(jnp.minimum(i + 2, n_chunks - 1), 0)
            return carry

        lax.fori_loop(0, n_chunks // 2, pair_of_steps, 0)

    run(qc_ref, kc_ref, vc_ref, gcc_ref, grc_ref, False)
    run(qx_ref, kx_ref, vx_ref, gcx_ref, grx_ref, True)
    o_ref[0] = o_scr[...].astype(o_ref.dtype)


def _gate_rows(gcol):
    b, t, n = gcol.shape
    g = gcol.reshape(b, t // CHUNK, CHUNK, n // GATE_LANES, GATE_LANES)
    return g.transpose(0, 3, 1, 4, 2)


def _dn_chunk(qc, kc, vc, gcc, qx, kx, vx, gcx):
    b, s, _ = qx.shape
    l = qc.shape[1]
    assert l % (2 * CHUNK) == 0 and s % (2 * CHUNK) == 0, "the chunk loop advances two chunks per trip"
    grc, grx = _gate_rows(gcc), _gate_rows(gcx)
    qk_cols = HEADS_PER_CALL * HEAD_DIM
    v_cols = 2 * qk_cols

    def seq_specs(t):
        return [pl.BlockSpec((1, t, qk_cols), lambda bi, j: (bi, 0, j)),
                pl.BlockSpec((1, t, qk_cols), lambda bi, j: (bi, 0, j)),
                pl.BlockSpec((1, t, v_cols), lambda bi, j: (bi, 0, j)),
                pl.BlockSpec((1, t, LANES), lambda bi, j: (bi, 0, 0)),
                pl.BlockSpec((1, HEADS_PER_CALL, t // CHUNK, GATE_LANES, CHUNK), lambda bi, j: (bi, j, 0, 0, 0))]

    return pl.pallas_call(
        _chunk_body,
        grid=(b, DN_QK_HEADS // HEADS_PER_CALL),
        in_specs=seq_specs(l) + seq_specs(s),
        out_specs=pl.BlockSpec((1, s, v_cols), lambda bi, j: (bi, 0, j)),
        out_shape=jax.ShapeDtypeStruct((b, s, DN_V_HEADS * HEAD_DIM), BF16),
        scratch_shapes=[pltpu.VMEM((N_CHAINS, HEAD_DIM, HEAD_DIM), F32),
                        pltpu.VMEM((s, v_cols), F32),
                        pltpu.VMEM((2, N_CHAINS, CHUNK, HEAD_DIM), F32),
                        pltpu.VMEM((2, N_CHAINS, 2 * CHUNK, HEAD_DIM), BF16),
                        pltpu.VMEM((2, N_CHAINS, CHUNK + HEAD_DIM, CHUNK), BF16),
                        pltpu.VMEM((2, N_CHAINS, 1, HEAD_DIM), F32)],
        compiler_params=_cparams("parallel", "parallel"),
        name="dn_chunk",
    )(qc, kc, vc, gcc, grc, qx, kx, vx, gcx, grx)


def _rot_cols(w):
    q = MLA_ROPE_DIM // 4
    return jnp.concatenate([-w[..., q:2 * q], w[..., 0:q], -w[..., 3 * q:4 * q], w[..., 2 * q:3 * q]], axis=-1)


def _rope_cos_sin(t_len):
    q = MLA_ROPE_DIM // 4
    pos = np.arange(t_len)
    inv = ROPE_BASE ** (-np.arange(q, dtype=np.float32) / q)
    ang_r = (pos // GRID_W).astype(np.float32)[:, None] * inv[None, :]
    ang_c = (pos % GRID_W).astype(np.float32)[:, None] * inv[None, :]
    ang = jnp.asarray(np.concatenate([ang_r, ang_r, ang_c, ang_c], axis=1))
    return jnp.cos(ang), jnp.sin(ang)


def _q_tables(cos, sin):
    t = cos.shape[0]
    one, zero = jnp.ones((t, HEAD_DIM), F32), jnp.zeros((t, MLA_ROPE_DIM), F32)
    c1 = jnp.concatenate([one, cos, zero], axis=1) * MLA_SCALE
    c2 = jnp.concatenate([jnp.zeros((t, HEAD_DIM), F32), sin, zero], axis=1) * MLA_SCALE
    return c1, c2


def _k_tables(cos, sin):
    zero = jnp.zeros_like(cos)
    return jnp.concatenate([cos, zero], axis=1), jnp.concatenate([sin, zero], axis=1)


def _mod_params(c, c_ctx, w_mod, b_mod):
    b, d = c.shape
    rows = ((b + 1 + 7) // 8) * 8
    cond = jnp.concatenate([c, c_ctx[None], jnp.zeros((rows - b - 1, d), F32)], axis=0)
    mod = _adaln(cond, w_mod, b_mod)
    sh, sc, gt = (mod[:b, i * d:(i + 1) * d].reshape(b, 1, d) for i in range(3))
    sh_c, sc_c, gt_c = (jnp.broadcast_to(mod[b, i * d:(i + 1) * d].reshape(1, 1, d), (b, 1, d)) for i in range(3))
    return (sh, sc, gt), (sh_c, sc_c, gt_c)


def _na_mla_layer(x, ctx, c, c_ctx, w_mod, b_mod, norm, w_in, rpb, q_norm, w_qb, kv_norm, w_kvb, w_out):
    b, s, d = x.shape
    l = ctx.shape[1]
    (sh, sc, gt), (sh_c, sc_c, gt_c) = _mod_params(c, c_ctx, w_mod, b_mod)
    na_w = NA_HEADS * HEAD_DIM
    n_main = 3 * na_w + 2 * MLA_RANK
    kr_w = w_in[:, n_main:n_main + MLA_ROPE_DIM]
    w_main = jnp.concatenate([w_in[:, :n_main], w_in[:, n_main + MLA_ROPE_DIM:]], axis=1).astype(BF16)
    w_small = jnp.concatenate([kr_w, _rot_cols(kr_w)], axis=1).astype(BF16)
    cos, sin = _rope_cos_sin(s)
    ident = (jnp.ones((l, MLA_ROPE_DIM), F32), jnp.zeros((l, MLA_ROPE_DIM), F32))
    hx, krx = _norm_proj(x, 0, d, norm, w_main, tm=IN_PROJ_ROWS, tn=1024, shift=sh, scale=sc, w_small=w_small,
                         rope_small=_k_tables(cos, sin), small_dtype=BF16, name="in_proj_ab_x")
    ident_flat = tuple(jnp.tile(t, (b, 1)) for t in ident)
    hc, krc = _norm_proj(ctx.reshape(1, b * l, d), 0, d, norm, w_main, tm=IN_PROJ_ROWS, tn=1024, shift=sh_c[:1],
                         scale=sc_c[:1], w_small=w_small, rope_small=_k_tables(*ident_flat), small_dtype=BF16,
                         name="in_proj_ab_ctx")
    hc, krc = hc.reshape(b, l, -1), krc.reshape(b, l, -1)

    wq = w_qb.reshape(MLA_RANK, MLA_HEADS, HEAD_DIM + MLA_ROPE_DIM)
    wq_r = wq[..., HEAD_DIM:]
    wq_p = jnp.concatenate([wq[..., :HEAD_DIM], wq_r, _rot_cols(wq_r)], axis=-1)
    wq_p = wq_p.reshape(MLA_RANK, MLA_HEADS * MLA_QK_PAD).astype(BF16)
    wkv = w_kvb.reshape(MLA_RANK, MLA_HEADS, 2 * HEAD_DIM)
    wkv_p = jnp.concatenate([wkv[..., :HEAD_DIM].reshape(MLA_RANK, -1), wkv[..., HEAD_DIM:].reshape(MLA_RANK, -1)],
                            axis=1).astype(BF16)
    cq_blk = 3 * na_w // MLA_RANK
    nq = MLA_HEADS * MLA_QK_PAD
    qx = _norm_proj(hx, cq_blk, MLA_RANK, q_norm, wq_p, tm=1024, tn=nq, rope_main=_q_tables(cos, sin), name="mla_q_x")
    qc = _norm_proj(hc, cq_blk, MLA_RANK, q_norm, wq_p, tm=1024, tn=nq, rope_main=_q_tables(*ident), name="mla_q_ctx")
    kvx = _norm_proj(hx, cq_blk + 1, MLA_RANK, kv_norm, wkv_p, tm=1024, tn=nq, name="mla_kv_x")
    kvc = _norm_proj(hc, cq_blk + 1, MLA_RANK, kv_norm, wkv_p, tm=1024, tn=nq, name="mla_kv_ctx")

    o_na = _na_attention(hx, hc, _na_bias_table(rpb, s // GRID_W))
    seg_x = (kvx, 0, krx, kvx, MLA_HEADS)
    seg_c = (kvc, 0, krc, kvc, MLA_HEADS)
    o_mla = _attention(qx, 0, MLA_QK_PAD, [seg_x, seg_c], MLA_HEADS, tq=512, scale=1.0, name="mla_attention")
    z_blk0 = n_main // na_w
    w_out_b = w_out.astype(BF16)
    x_new = _out_proj_ab(x, gt, o_na, o_mla, hx, z_blk0, w_out_b, tm=512)

    o_na_c = _attention(hc, 0, HEAD_DIM, [(hc, NA_HEADS, None, hc, 2 * NA_HEADS)], NA_HEADS, tq=l,
                        scale=HEAD_DIM ** -0.5, name="na_ctx_attention")
    o_mla_c = _attention(qc, 0, MLA_QK_PAD, [seg_c], MLA_HEADS, tq=l, scale=1.0, name="mla_ctx_attention")
    ctx_new = _out_proj_ab(ctx, gt_c, o_na_c, o_mla_c, hc, z_blk0, w_out_b, tm=512)
    return x_new, ctx_new


def _deltanet_final_layer(x, ctx, c, c_ctx, w_mod, b_mod, norm, w_in, conv_w, a_log, dt_bias, o_norm, w_out, final_norm):
    b, s, d = x.shape
    (sh, sc, gt), (sh_c, sc_c, _) = _mod_params(c, c_ctx, w_mod, b_mod)
    qk_w = DN_QK_HEADS * HEAD_DIM
    v_w = DN_V_HEADS * HEAD_DIM
    n_main = 2 * qk_w + 2 * v_w
    lane = np.arange(4 * DN_V_HEADS)
    perm = (lane % 4) * DN_V_HEADS + lane // 4
    w_main = w_in[:, :n_main].astype(BF16)
    w_small = w_in[:, n_main:][:, perm].astype(BF16)
    zeros = jnp.zeros((2 * DN_V_HEADS,), F32)
    alog = jnp.concatenate([zeros, a_log.reshape(-1).astype(F32)])[perm].reshape(1, -1)
    dtb = jnp.concatenate([zeros, dt_bias.reshape(-1).astype(F32)])[perm].reshape(1, -1)
    conv_w = conv_w.astype(F32)

    def inputs(t, shift, scale, name, shared_modulation=False):
        bt, tl, _ = t.shape
        if shared_modulation:
            t, shift, scale = t.reshape(1, bt * tl, d), shift[:1], scale[:1]
        h, hs = _norm_proj(t, 0, d, norm, w_main, tm=IN_PROJ_ROWS, tn=1024, shift=shift, scale=scale, w_small=w_small,
                           name=name)
        h, hs = h.reshape(bt, tl, -1), hs.reshape(bt, tl, -1)
        q = _dn_conv(h, conv_w, 0, qk_w, HEAD_DIM ** -0.5, tt=512)
        k = _dn_conv(h, conv_w, qk_w, qk_w, 1.0, tt=512)
        v = _dn_conv(h, conv_w, 2 * qk_w, v_w, None, tt=512)
        return h, q, k, v, _dn_gates(hs, alog, dtb)

    hx, qx, kx, vx, gx = inputs(x, sh, sc, "in_proj_dn_x")
    _, qc, kc, vc, gc = inputs(ctx, sh_c, sc_c, "in_proj_dn_ctx", shared_modulation=True)
    o = _dn_chunk(qc, kc, vc, gc, qx, kx, vx, gx)
    return _out_proj_dn(x, gt, o, hx, (2 * qk_w + v_w) // v_w, o_norm, w_out.astype(BF16), final_norm, tm=256)


def kernel(x, c, ctx, c_ctx, ab_w_mod, ab_b_mod, ab_norm, ab_w_in, ab_rpb, ab_q_norm, ab_w_qb, ab_kv_norm, ab_w_kvb, ab_w_out, dn_w_mod, dn_b_mod, dn_norm, dn_w_in, dn_conv, dn_a_log, dn_dt_bias, dn_o_norm, dn_w_out, final_norm):
    assert ab_w_mod.shape[0] == 1 and dn_w_mod.shape[0] == 1, "depth-2 trunk: one even and one odd layer"
    x, ctx = _na_mla_layer(x, ctx, c, c_ctx, ab_w_mod[0], ab_b_mod[0], ab_norm[0], ab_w_in[0], ab_rpb[0],
                           ab_q_norm[0], ab_w_qb[0], ab_kv_norm[0], ab_w_kvb[0], ab_w_out[0])
    return _deltanet_final_layer(x, ctx, c, c_ctx, dn_w_mod[0], dn_b_mod[0], dn_norm[0], dn_w_in[0], dn_conv[0],
                                 dn_a_log[0], dn_dt_bias[0], dn_o_norm[0], dn_w_out[0], final_norm)
```

```python
import functools

import numpy as np
import jax
import jax.numpy as jnp
from jax import lax
from jax.experimental import pallas as pl
from jax.experimental.pallas import tpu as pltpu

F32 = jnp.float32
BF16 = jnp.bfloat16

EPS = 1e-6
GRID_W = 64
WIN_R = 8
WIN_C = 16
NA_HEADS = 8
HEAD_DIM = 128
MLA_HEADS = 8
MLA_RANK = 512
MLA_ROPE_DIM = 64
MLA_SCALE = (HEAD_DIM + MLA_ROPE_DIM) ** -0.5
ROPE_BASE = 10000.0
DN_QK_HEADS = 16
DN_V_HEADS = 32
CONV_K = 5
CHUNK = 128
NEG_BIG = -1e30

LANES = 128
MLA_QK_PAD = 256
NA_Q_ROWS = 8
NA_KEY_ROWS = 16
ATTN_SUB_ROWS = 256
IN_PROJ_ROWS = 1024
VMEM_LIMIT = 56 * 1024 * 1024


def _cparams(*sem):
    return pltpu.CompilerParams(dimension_semantics=sem, vmem_limit_bytes=VMEM_LIMIT)


def _adaln_body(c_ref, w_ref, b_ref, o_ref):
    cf = c_ref[...]
    a = (cf * jax.nn.sigmoid(cf)).astype(BF16)
    o_ref[...] = jnp.dot(a, w_ref[...].astype(BF16), preferred_element_type=F32) + b_ref[...]


def _adaln(cond, w_mod, b_mod):
    r, d = cond.shape
    n = w_mod.shape[1]
    tn = 512
    return pl.pallas_call(
        _adaln_body,
        grid=(n // tn,),
        in_specs=[pl.BlockSpec((r, d), lambda j: (0, 0)),
                  pl.BlockSpec((d, tn), lambda j: (0, j)),
                  pl.BlockSpec((1, tn), lambda j: (0, j))],
        out_specs=pl.BlockSpec((r, tn), lambda j: (0, j)),
        out_shape=jax.ShapeDtypeStruct((r, n), F32),
        compiler_params=_cparams("parallel"),
        name="adaln",
    )(cond, w_mod, b_mod.reshape(1, n))


def _proj_body(*refs, modulate, has_small, rope_main, rope_small):
    it = iter(refs)
    x_ref, gain_ref = next(it), next(it)
    shift_ref = scale_ref = ws_ref = c1_ref = c2_ref = s1_ref = s2_ref = os_ref = None
    if modulate:
        shift_ref, scale_ref = next(it), next(it)
    w_ref = next(it)
    if has_small:
        ws_ref = next(it)
    if rope_main:
        c1_ref, c2_ref = next(it), next(it)
    if rope_small:
        s1_ref, s2_ref = next(it), next(it)
    o_ref = next(it)
    if has_small:
        os_ref = next(it)
    xm_ref = next(it)

    @pl.when(pl.program_id(2) == 0)
    def _():
        xf = x_ref[0].astype(F32)
        ms = jnp.mean(xf * xf, axis=-1, keepdims=True)
        y = xf * lax.rsqrt(ms + EPS) * gain_ref[...]
        if modulate:
            y = y * (1.0 + scale_ref[0]) + shift_ref[0]
        xm = y.astype(BF16)
        xm_ref[...] = xm
        if has_small:
            s = jnp.dot(xm, ws_ref[...], preferred_element_type=F32)
            if rope_small:
                s = s * s1_ref[...] + pltpu.roll(s, LANES // 2, axis=1) * s2_ref[...]
            os_ref[0] = s.astype(os_ref.dtype)

    acc = jnp.dot(xm_ref[...], w_ref[...], preferred_element_type=F32)
    if rope_main:
        for h in range(acc.shape[1] // MLA_QK_PAD):
            a = acc[:, h * MLA_QK_PAD:(h + 1) * MLA_QK_PAD]
            a = a * c1_ref[...] + pltpu.roll(a, MLA_QK_PAD - MLA_ROPE_DIM, axis=1) * c2_ref[...]
            o_ref[0, :, h * MLA_QK_PAD:(h + 1) * MLA_QK_PAD] = a.astype(o_ref.dtype)
    else:
        o_ref[0] = acc.astype(o_ref.dtype)


def _norm_proj(x, kblk, k, gain, w, *, tm, tn, shift=None, scale=None, w_small=None,
               rope_main=None, rope_small=None, small_dtype=F32, name="proj"):
    b, t, _ = x.shape
    n = w.shape[1]
    modulate = shift is not None
    has_small = w_small is not None
    tm = min(tm, t)
    args = [x, gain.reshape(1, k).astype(F32)]
    in_specs = [pl.BlockSpec((1, tm, k), lambda bi, i, j: (bi, i, kblk)),
                pl.BlockSpec((1, k), lambda bi, i, j: (0, 0))]
    if modulate:
        args += [shift, scale]
        in_specs += [pl.BlockSpec((1, 1, k), lambda bi, i, j: (bi, 0, 0))] * 2
    args.append(w)
    in_specs.append(pl.BlockSpec((k, tn), lambda bi, i, j: (0, j)))
    if has_small:
        ns = w_small.shape[1]
        args.append(w_small)
        in_specs.append(pl.BlockSpec((k, ns), lambda bi, i, j: (0, 0)))
    if rope_main is not None:
        args += list(rope_main)
        in_specs += [pl.BlockSpec((tm, MLA_QK_PAD), lambda bi, i, j: (i, 0))] * 2
    if rope_small is not None:
        args += list(rope_small)
        in_specs += [pl.BlockSpec((tm, LANES), lambda bi, i, j: (i, 0))] * 2
    out_shape = [jax.ShapeDtypeStruct((b, t, n), BF16)]
    out_specs = [pl.BlockSpec((1, tm, tn), lambda bi, i, j: (bi, i, j))]
    if has_small:
        out_shape.append(jax.ShapeDtypeStruct((b, t, ns), small_dtype))
        out_specs.append(pl.BlockSpec((1, tm, ns), lambda bi, i, j: (bi, i, 0)))
    body = functools.partial(_proj_body, modulate=modulate, has_small=has_small,
                             rope_main=rope_main is not None, rope_small=rope_small is not None)
    out = pl.pallas_call(
        body,
        grid=(b, t // tm, n // tn),
        in_specs=in_specs,
        out_specs=out_specs,
        out_shape=out_shape,
        scratch_shapes=[pltpu.VMEM((tm, k), BF16)],
        compiler_params=_cparams("parallel", "parallel", "arbitrary"),
        name=name,
    )(*args)
    return out if has_small else out[0]


def _attn_body(*refs, seg_lens, has_extra, scale):
    it = iter(refs)
    q_ref = next(it)
    segs = []
    for _ in seg_lens:
        km = next(it)
        ke = next(it) if has_extra else None
        segs.append((km, ke, next(it)))
    o_ref, k_scr, v_scr = next(it), next(it), next(it)

    @pl.when(pl.program_id(2) == 0)
    def _():
        off = 0
        for (km, ke, v), ln in zip(segs, seg_lens):
            k_scr[off:off + ln, 0:HEAD_DIM] = km[0]
            if has_extra:
                k_scr[off:off + ln, HEAD_DIM:2 * HEAD_DIM] = ke[0]
            v_scr[off:off + ln, :] = v[0]
            off += ln

    sub = min(ATTN_SUB_ROWS, q_ref.shape[1])
    n_sub = q_ref.shape[1] // sub

    def scores(j):
        s = lax.dot_general(q_ref[0, j * sub:(j + 1) * sub, :], k_scr[...], (((1,), (1,)), ((), ())),
                            preferred_element_type=F32)
        return s if scale == 1.0 else s * scale

    s_next = scores(0)
    for j in range(n_sub):
        s = s_next
        if j + 1 < n_sub:
            s_next = scores(j + 1)
        m = jnp.max(s, axis=-1, keepdims=True)
        e = jnp.exp(s - m)
        l = jnp.sum(e, axis=-1, keepdims=True)
        o = jnp.dot(e.astype(BF16), v_scr[...], preferred_element_type=F32)
        o_ref[0, j * sub:(j + 1) * sub, :] = (o / l).astype(o_ref.dtype)


def _attention(q, q_blk0, dq, segs, heads, *, tq, scale, name):
    b, t, _ = q.shape
    tq = min(tq, t)
    has_extra = segs[0][2] is not None
    seg_lens = tuple(s[0].shape[1] for s in segs)
    tk = sum(seg_lens)
    args = [q]
    in_specs = [pl.BlockSpec((1, tq, dq), lambda bi, h, i: (bi, i, q_blk0 + h))]
    for km, kb0, ke, v, vb0 in segs:
        ln = km.shape[1]
        args.append(km)
        in_specs.append(pl.BlockSpec((1, ln, HEAD_DIM), lambda bi, h, i, kb0=kb0: (bi, 0, kb0 + h)))
        if has_extra:
            args.append(ke)
            in_specs.append(pl.BlockSpec((1, ln, HEAD_DIM), lambda bi, h, i: (bi, 0, 0)))
        args.append(v)
        in_specs.append(pl.BlockSpec((1, ln, HEAD_DIM), lambda bi, h, i, vb0=vb0: (bi, 0, vb0 + h)))
    body = functools.partial(_attn_body, seg_lens=seg_lens, has_extra=has_extra, scale=scale)
    return pl.pallas_call(
        body,
        grid=(b, heads, t // tq),
        in_specs=in_specs,
        out_specs=pl.BlockSpec((1, tq, HEAD_DIM), lambda bi, h, i: (bi, i, h)),
        out_shape=jax.ShapeDtypeStruct((b, t, heads * HEAD_DIM), BF16),
        scratch_shapes=[pltpu.VMEM((tk, dq), BF16), pltpu.VMEM((tk, HEAD_DIM), BF16)],
        compiler_params=_cparams("parallel", "parallel", "arbitrary"),
        name=name,
    )(*args)


def _na_geometry(rows):
    groups = rows // NA_Q_ROWS
    key_rows = min(NA_KEY_ROWS, rows)
    return groups, key_rows


def _na_bias_table(rpb, rows):
    heads = rpb.shape[0]
    groups, key_rows = _na_geometry(rows)
    g = np.arange(groups)
    slab0 = np.clip(NA_Q_ROWS * g - WIN_R // 2, 0, rows - key_rows)
    r = NA_Q_ROWS * g[:, None] + np.arange(NA_Q_ROWS)[None]
    k_abs = slab0[:, None] + np.arange(key_rows)[None]
    win0 = np.clip(r - WIN_R // 2, 0, rows - WIN_R)
    d_row = k_abs[:, None, :] - r[:, :, None] + (WIN_R - 1)
    ok_row = (k_abs[:, None, :] >= win0[:, :, None]) & (k_abs[:, None, :] < win0[:, :, None] + WIN_R)
    col = np.arange(GRID_W)
    c0 = np.clip(col - WIN_C // 2, 0, GRID_W - WIN_C)
    d_col = col[None, :] - col[:, None] + (WIN_C - 1)
    ok_col = (col[None, :] >= c0[:, None]) & (col[None, :] < c0[:, None] + WIN_C)
    sel_row = (np.clip(d_row, 0, 2 * WIN_R - 2)[..., None] == np.arange(2 * WIN_R - 1)).astype(np.float32)
    sel_col = (np.clip(d_col, 0, 2 * WIN_C - 2)[..., None] == np.arange(2 * WIN_C - 1)).astype(np.float32)
    ok = jnp.asarray(ok_row)[:, :, None, :, None] & jnp.asarray(ok_col)[None, None, :, None, :]
    t = jnp.einsum("grka,hab,qcb->hgrqkc", jnp.asarray(sel_row), rpb.astype(F32), jnp.asarray(sel_col),
                   precision=lax.Precision.HIGHEST)
    t = jnp.where(ok[None], t, NEG_BIG)
    return t.reshape(heads, groups, NA_Q_ROWS * GRID_W, key_rows * GRID_W)


def _na_body(q_ref, k_ref, v_ref, kc_ref, vc_ref, bias_ref, o_ref, *, rows, scale):
    groups, key_rows = _na_geometry(rows)
    g = pl.program_id(2)
    slab0 = jnp.clip(NA_Q_ROWS * g - WIN_R // 2, 0, rows - key_rows)
    start = pl.multiple_of(slab0 * GRID_W, 4 * GRID_W)
    nk = key_rows * GRID_W
    nt = (((1,), (1,)), ((), ()))
    sub = min(ATTN_SUB_ROWS, q_ref.shape[1])
    n_sub = q_ref.shape[1] // sub

    def scores(j):
        q = q_ref[0, j * sub:(j + 1) * sub, :]
        s_nb = lax.dot_general(q, k_ref[0, pl.ds(start, nk), :], nt, preferred_element_type=F32)
        s_cx = lax.dot_general(q, kc_ref[0], nt, preferred_element_type=F32)
        return s_nb, s_cx

    s_next = scores(0)
    for j in range(n_sub):
        s_nb, s_cx = s_next
        if j + 1 < n_sub:
            s_next = scores(j + 1)
        s_nb = s_nb * scale + bias_ref[0, g, j * sub:(j + 1) * sub, :]
        s_cx = s_cx * scale
        m = jnp.maximum(jnp.max(s_nb, axis=-1, keepdims=True), jnp.max(s_cx, axis=-1, keepdims=True))
        e_nb = jnp.exp(s_nb - m)
        e_cx = jnp.exp(s_cx - m)
        l = jnp.sum(e_nb, axis=-1, keepdims=True) + jnp.sum(e_cx, axis=-1, keepdims=True)
        o = jnp.dot(e_nb.astype(BF16), v_ref[0, pl.ds(start, nk), :], preferred_element_type=F32)
        o = o + jnp.dot(e_cx.astype(BF16), vc_ref[0], preferred_element_type=F32)
        o_ref[0, j * sub:(j + 1) * sub, :] = (o / l).astype(o_ref.dtype)


def _na_attention(hx, hc, bias):
    b, s, _ = hx.shape
    l = hc.shape[1]
    rows = s // GRID_W
    groups, key_rows = _na_geometry(rows)
    tq = NA_Q_ROWS * GRID_W
    body = functools.partial(_na_body, rows=rows, scale=HEAD_DIM ** -0.5)
    return pl.pallas_call(
        body,
        grid=(NA_HEADS, b, groups),
        in_specs=[pl.BlockSpec((1, tq, HEAD_DIM), lambda h, bi, g: (bi, g, h)),
                  pl.BlockSpec((1, s, HEAD_DIM), lambda h, bi, g: (bi, 0, NA_HEADS + h)),
                  pl.BlockSpec((1, s, HEAD_DIM), lambda h, bi, g: (bi, 0, 2 * NA_HEADS + h)),
                  pl.BlockSpec((1, l, HEAD_DIM), lambda h, bi, g: (bi, 0, NA_HEADS + h)),
                  pl.BlockSpec((1, l, HEAD_DIM), lambda h, bi, g: (bi, 0, 2 * NA_HEADS + h)),
                  pl.BlockSpec((1, groups, tq, key_rows * GRID_W), lambda h, bi, g: (h, 0, 0, 0))],
        out_specs=pl.BlockSpec((1, tq, HEAD_DIM), lambda h, bi, g: (bi, g, h)),
        out_shape=jax.ShapeDtypeStruct((b, s, NA_HEADS * HEAD_DIM), BF16),
        compiler_params=_cparams("parallel", "parallel", "parallel"),
        name="na_attention",
    )(hx, hx, hx, hc, hc, bias)


def _silu(x):
    return x * jax.nn.sigmoid(x)


def _out_ab_body(x_ref, gt_ref, o1_ref, o2_ref, z1_ref, z2_ref, w_ref, y_ref):
    half = o1_ref.shape[2]
    a1 = (o1_ref[0].astype(F32) * _silu(z1_ref[0].astype(F32))).astype(BF16)
    acc = jnp.dot(a1, w_ref[0:half, :], preferred_element_type=F32)
    a2 = (o2_ref[0].astype(F32) * _silu(z2_ref[0].astype(F32))).astype(BF16)
    acc = acc + jnp.dot(a2, w_ref[half:2 * half, :], preferred_element_type=F32)
    y_ref[0] = x_ref[0] + gt_ref[0] * acc


def _out_proj_ab(x, gt, o1, o2, h, z_blk0, w, *, tm):
    b, t, d = x.shape
    half = o1.shape[2]
    tm = min(tm, t)
    return pl.pallas_call(
        _out_ab_body,
        grid=(b, t // tm),
        in_specs=[pl.BlockSpec((1, tm, d), lambda bi, i: (bi, i, 0)),
                  pl.BlockSpec((1, 1, d), lambda bi, i: (bi, 0, 0)),
                  pl.BlockSpec((1, tm, half), lambda bi, i: (bi, i, 0)),
                  pl.BlockSpec((1, tm, half), lambda bi, i: (bi, i, 0)),
                  pl.BlockSpec((1, tm, half), lambda bi, i: (bi, i, z_blk0)),
                  pl.BlockSpec((1, tm, half), lambda bi, i: (bi, i, z_blk0 + 1)),
                  pl.BlockSpec((2 * half, d), lambda bi, i: (0, 0))],
        out_specs=pl.BlockSpec((1, tm, d), lambda bi, i: (bi, i, 0)),
        out_shape=jax.ShapeDtypeStruct((b, t, d), F32),
        compiler_params=_cparams("parallel", "parallel"),
        name="out_proj_ab",
    )(x, gt, o1, o2, h, h, w)


def _out_dn_body(x_ref, gt_ref, o_ref, z_ref, on_ref, w_ref, fn_ref, y_ref, a_scr):
    heads = o_ref.shape[2] // HEAD_DIM
    parts = 2
    acc = None
    for p in range(parts):
        for h in range(p * heads // parts, (p + 1) * heads // parts):
            sl = slice(h * HEAD_DIM, (h + 1) * HEAD_DIM)
            oh = o_ref[0, :, sl].astype(F32)
            ms = jnp.mean(oh * oh, axis=-1, keepdims=True)
            a = oh * lax.rsqrt(ms + EPS) * on_ref[...] * _silu(z_ref[0, :, sl].astype(F32))
            a_scr[:, sl] = a.astype(BF16)
        cols = slice(p * heads // parts * HEAD_DIM, (p + 1) * heads // parts * HEAD_DIM)
        part = jnp.dot(a_scr[:, cols], w_ref[cols, :], preferred_element_type=F32)
        acc = part if acc is None else acc + part
    y = x_ref[0] + gt_ref[0] * acc
    ms = jnp.mean(y * y, axis=-1, keepdims=True)
    y_ref[0] = y * lax.rsqrt(ms + EPS) * fn_ref[...]


def _out_proj_dn(x, gt, o, h, z_blk, o_norm, w, final_norm, *, tm):
    b, t, d = x.shape
    km = o.shape[2]
    tm = min(tm, t)
    return pl.pallas_call(
        _out_dn_body,
        grid=(b, t // tm),
        in_specs=[pl.BlockSpec((1, tm, d), lambda bi, i: (bi, i, 0)),
                  pl.BlockSpec((1, 1, d), lambda bi, i: (bi, 0, 0)),
                  pl.BlockSpec((1, tm, km), lambda bi, i: (bi, i, 0)),
                  pl.BlockSpec((1, tm, km), lambda bi, i: (bi, i, z_blk)),
                  pl.BlockSpec((1, HEAD_DIM), lambda bi, i: (0, 0)),
                  pl.BlockSpec((km, d), lambda bi, i: (0, 0)),
                  pl.BlockSpec((1, d), lambda bi, i: (0, 0))],
        out_specs=pl.BlockSpec((1, tm, d), lambda bi, i: (bi, i, 0)),
        out_shape=jax.ShapeDtypeStruct((b, t, d), F32),
        scratch_shapes=[pltpu.VMEM((tm, km), BF16)],
        compiler_params=_cparams("parallel", "parallel"),
        name="out_proj_dn",
    )(x, gt, o, h, o_norm.reshape(1, HEAD_DIM).astype(F32), w, final_norm.reshape(1, d).astype(F32))


CONV_HALO = 16


SUBLANES = 8
SEG_PAD = 8


def _conv_body(xp_ref, x_ref, xn_ref, w_ref, o_ref, in_scr, out_scr, *, l2_scale):
    i = pl.program_id(1)
    nt = pl.num_programs(1)
    tt = x_ref.shape[1]
    seg = tt // SUBLANES
    pitch = seg + SEG_PAD
    half = CONV_K // 2
    prev = jnp.where(i > 0, xp_ref[0].astype(F32), 0.0)
    nxt = jnp.where(i < nt - 1, xn_ref[0].astype(F32), 0.0)
    for sl in range(x_ref.shape[2] // LANES):
        lanes = slice(sl * LANES, (sl + 1) * LANES)
        in_scr[sl, seg - SUBLANES:seg, :] = prev[CONV_HALO - SUBLANES:, lanes]
        for s in range(SUBLANES):
            in_scr[sl, (s + 1) * pitch:(s + 1) * pitch + seg, :] = x_ref[0, s * seg:(s + 1) * seg, lanes].astype(F32)
        in_scr[sl, (SUBLANES + 1) * pitch:(SUBLANES + 1) * pitch + SUBLANES, :] = nxt[:SUBLANES, lanes]
        w = [w_ref[j:j + 1, lanes] for j in range(CONV_K)]

        def strided(n):
            start = pitch + n - (SEG_PAD if n < 0 else 0) + (SEG_PAD if n >= seg else 0)
            return in_scr[sl, pl.ds(start, SUBLANES, stride=pitch), :]

        window = [strided(n) for n in range(-half, half)]
        for m in range(seg):
            window.append(strided(m + half))
            acc = window[0] * w[0]
            for j in range(1, CONV_K):
                acc = acc + window[j] * w[j]
            window.pop(0)
            y = _silu(acc)
            if l2_scale is not None:
                y = y * (lax.rsqrt(jnp.sum(y * y, axis=-1, keepdims=True) + EPS) * l2_scale)
            out_scr[sl, pl.ds(m, SUBLANES, stride=pitch), :] = y
        for s in range(SUBLANES):
            o_ref[0, s * seg:(s + 1) * seg, lanes] = out_scr[sl, s * pitch:s * pitch + seg, :].astype(o_ref.dtype)


def _dn_conv(h, conv_w, col0, width, l2_scale, *, tt, cb=1024):
    b, t, _ = h.shape
    tt = min(tt, t)
    blk0 = col0 // cb
    hb = tt // CONV_HALO
    nhalo = t // CONV_HALO
    pitch = tt // SUBLANES + SEG_PAD
    body = functools.partial(_conv_body, l2_scale=l2_scale)
    return pl.pallas_call(
        body,
        grid=(b, t // tt, width // cb),
        in_specs=[pl.BlockSpec((1, CONV_HALO, cb), lambda bi, i, c: (bi, jnp.maximum(i * hb - 1, 0), blk0 + c)),
                  pl.BlockSpec((1, tt, cb), lambda bi, i, c: (bi, i, blk0 + c)),
                  pl.BlockSpec((1, CONV_HALO, cb), lambda bi, i, c: (bi, jnp.minimum((i + 1) * hb, nhalo - 1), blk0 + c)),
                  pl.BlockSpec((CONV_K, cb), lambda bi, i, c: (0, blk0 + c))],
        out_specs=pl.BlockSpec((1, tt, cb), lambda bi, i, c: (bi, i, c)),
        out_shape=jax.ShapeDtypeStruct((b, t, width), BF16),
        scratch_shapes=[pltpu.VMEM((cb // LANES, (SUBLANES + 2) * pitch, LANES), F32),
                        pltpu.VMEM((cb // LANES, SUBLANES * pitch, LANES), F32)],
        compiler_params=_cparams("parallel", "parallel", "parallel"),
        name="dn_conv",
    )(h, h, h, conv_w)


GATE_TILE = 256


def _split3(x):
    hi = x.astype(BF16)
    r1 = x - hi.astype(F32)
    mid = r1.astype(BF16)
    lo = (r1 - mid.astype(F32)).astype(BF16)
    return hi, mid, lo


def _gates_body(h_ref, alog_ref, dtb_ref, o_ref):
    t = h_ref[0]
    lane = lax.broadcasted_iota(jnp.int32, t.shape, 1) % 4
    beta = jax.nn.sigmoid(t)
    z = t + dtb_ref[...]
    softplus = jnp.maximum(z, 0.0) + jnp.log1p(jnp.exp(-jnp.abs(z)))
    g = -jnp.exp(alog_ref[...]) * softplus
    n = t.shape[0]
    r = lax.broadcasted_iota(jnp.int32, (n, n), 0)
    c = lax.broadcasted_iota(jnp.int32, (n, n), 1)
    same = (r // CHUNK) == (c // CHUNK)
    tri_f = jnp.where(same & (r >= c), 1.0, 0.0).astype(BF16)
    tri_b = jnp.where(same & (r <= c), 1.0, 0.0).astype(BF16)
    pf = pb = None
    for part in _split3(g):
        df = jnp.dot(tri_f, part, preferred_element_type=F32)
        db = jnp.dot(tri_b, part, preferred_element_type=F32)
        pf = df if pf is None else pf + df
        pb = db if pb is None else pb + db
    o_ref[0] = jnp.where(lane < 2, beta, jnp.where(lane == 2, pf, pb))


def _dn_gates(h_small, alog, dtb):
    b, t, n = h_small.shape
    tt = min(GATE_TILE, t)
    return pl.pallas_call(
        _gates_body,
        grid=(b, t // tt),
        in_specs=[pl.BlockSpec((1, tt, n), lambda bi, i: (bi, i, 0)),
                  pl.BlockSpec((1, n), lambda bi, i: (0, 0)),
                  pl.BlockSpec((1, n), lambda bi, i: (0, 0))],
        out_specs=pl.BlockSpec((1, tt, n), lambda bi, i: (bi, i, 0)),
        out_shape=jax.ShapeDtypeStruct((b, t, n), F32),
        compiler_params=_cparams("parallel", "parallel"),
        name="dn_gates",
    )(h_small, alog, dtb)


HEADS_PER_CALL = 4
N_CHAINS = 4 * HEADS_PER_CALL
GATE_LANES = 8
MERGE_LEVELS = tuple(4 << i for i in range((CHUNK // 4).bit_length() - 1))


def _bdot(a, b):
    return jnp.dot(a.astype(BF16), b.astype(BF16), preferred_element_type=F32)


def _unit_tri_inverses(lms, eye, level):
    l4s = [jnp.where(level == 0, lm, 0.0) for lm in lms]
    ts = [eye - l4 for l4 in l4s]
    sq = [_bdot(l4, l4) for l4 in l4s]
    ts = [t + _bdot(t, p) for t, p in zip(ts, sq)]
    for m in range(1, len(MERGE_LEVELS) + 1):
        cs = [jnp.where(level == m, lm, 0.0).astype(BF16) for lm in lms]
        tbs = [t.astype(BF16) for t in ts]
        xs = [jnp.dot(tb, c, preferred_element_type=F32) for tb, c in zip(tbs, cs)]
        ts = [t - _bdot(x, tb) for t, x, tb in zip(ts, xs, tbs)]
    return ts


def _chunk_body(qc_ref, kc_ref, vc_ref, gcc_ref, grc_ref, qx_ref, kx_ref, vx_ref, gcx_ref, grx_ref,
                o_ref, s_scr, o_scr, u_scr, wq_scr, ak_scr, dl_scr):
    jg = pl.program_id(1)
    group_lanes = GATE_LANES * HEADS_PER_CALL
    lane_shift = (LANES - group_lanes * jg) % LANES
    s_scr[...] = jnp.zeros_like(s_scr)
    o_scr[...] = jnp.zeros_like(o_scr)
    row = lax.broadcasted_iota(jnp.int32, (CHUNK, CHUNK), 0)
    col = lax.broadcasted_iota(jnp.int32, (CHUNK, CHUNK), 1)
    eye = jnp.where(row == col, 1.0, 0.0)
    level = jnp.zeros((CHUNK, CHUNK), jnp.int32)
    for size in MERGE_LEVELS:
        level = level + jnp.where((row // size) == (col // size), 0, 1)
    incl = (row >= col, row <= col)
    strict = (row > col, row < col)
    nt = (((1,), (1,)), ((), ()))
    chains = [(jh, vh, d) for d in range(2) for jh in range(HEADS_PER_CALL) for vh in range(2)]
    streams = [(jh, d) for d in range(2) for jh in range(HEADS_PER_CALL)]

    def head_cols(jh):
        return slice(jh * HEAD_DIM, (jh + 1) * HEAD_DIM)

    def value_cols(jh, vh):
        return slice((2 * jh + vh) * HEAD_DIM, (2 * jh + vh + 1) * HEAD_DIM)

    def run(q_ref, k_ref, v_ref, gc_ref, gr_ref, write_out):
        n_chunks = q_ref.shape[1] // CHUNK

        def chunk_of(i, d):
            n = i if d == 0 else n_chunks - 1 - i
            return n, pl.multiple_of(n * CHUNK, CHUNK)

        def prepare(i, slot):
            ns, r0s, gcols = {}, {}, {}
            for d in range(2):
                ns[d], r0s[d] = chunk_of(i, d)
                gcols[d] = pltpu.roll(gc_ref[0, pl.ds(r0s[d], CHUNK), :], lane_shift, axis=1)
            ks = {(jh, d): k_ref[0, pl.ds(r0s[d], CHUNK), head_cols(jh)] for jh, d in streams}
            kks = {sd: lax.dot_general(ks[sd], ks[sd], nt, preferred_element_type=F32) for sd in streams}
            if write_out:
                qs = {(jh, d): q_ref[0, pl.ds(r0s[d], CHUNK), head_cols(jh)] for jh, d in streams}
                qks = {sd: lax.dot_general(qs[sd], ks[sd], nt, preferred_element_type=F32) for sd in streams}
            grows = {(jh, d): gr_ref[0, jh, ns[d]] for jh, d in streams}
            lane0 = [GATE_LANES * jh + 4 * vh + d for jh, vh, d in chains]
            beta_c = [gcols[d][:, l:l + 1] for l, (jh, vh, d) in zip(lane0, chains)]
            g_c = [gcols[d][:, l + 2:l + 3] for l, (jh, vh, d) in zip(lane0, chains)]
            beta_r = [grows[jh, d][4 * vh + d:4 * vh + d + 1, :] for jh, vh, d in chains]
            g_r = [grows[jh, d][4 * vh + 2 + d:4 * vh + 3 + d, :] for jh, vh, d in chains]
            g_last = [g[:, CHUNK - 1:CHUNK] if d == 0 else g[:, 0:1] for g, (jh, vh, d) in zip(g_r, chains)]
            dec = [jnp.where(incl[d], jnp.exp(jnp.where(incl[d], gc - gr, 0.0)), 0.0)
                   for gc, gr, (jh, vh, d) in zip(g_c, g_r, chains)]
            lms = [jnp.where(strict[d], kks[jh, d] * dc, 0.0) * bc
                   for dc, bc, (jh, vh, d) in zip(dec, beta_c, chains)]
            tbs = [t * br for t, br in zip(_unit_tri_inverses(lms, eye, level), beta_r)]
            us = [jnp.dot(tb.astype(BF16), v_ref[0, pl.ds(r0s[d], CHUNK), value_cols(jh, vh)],
                          preferred_element_type=F32) for tb, (jh, vh, d) in zip(tbs, chains)]
            ws = [jnp.dot((tb * jnp.exp(gr)).astype(BF16), ks[jh, d], preferred_element_type=F32)
                  for tb, gr, (jh, vh, d) in zip(tbs, g_r, chains)]
            for ci, (jh, vh, d) in enumerate(chains):
                u_scr[slot, ci] = us[ci]
                wq_scr[slot, ci, 0:CHUNK, :] = ws[ci].astype(BF16)
                kg = ks[jh, d].astype(F32) * jnp.exp(g_last[ci] - g_c[ci])
                ak_scr[slot, ci, CHUNK:CHUNK + HEAD_DIM, :] = kg.T.astype(BF16)
                dl_scr[slot, ci] = jnp.broadcast_to(jnp.exp(g_last[ci]), (1, HEAD_DIM))
                if write_out:
                    wq_scr[slot, ci, CHUNK:2 * CHUNK, :] = (qs[jh, d].astype(F32) * jnp.exp(g_c[ci])).astype(BF16)
                    ak_scr[slot, ci, 0:CHUNK, :] = (qks[jh, d] * dec[ci]).astype(BF16)

        def advance(i, slot):
            r0s = [chunk_of(i, d)[1] for d in range(2)]
            s_old = [s_scr[ci] for ci in range(N_CHAINS)]
            sbs = [s.astype(BF16) for s in s_old]
            rows = slice(0, 2 * CHUNK if write_out else CHUNK)
            r1 = [jnp.dot(wq_scr[slot, ci, rows, :], sbs[ci], preferred_element_type=F32) for ci in range(N_CHAINS)]
            vns = [(u_scr[slot, ci] - r1[ci][0:CHUNK]).astype(BF16) for ci in range(N_CHAINS)]
            rows = slice(0 if write_out else CHUNK, CHUNK + HEAD_DIM)
            r2 = [jnp.dot(ak_scr[slot, ci, rows, :], vns[ci], preferred_element_type=F32) for ci in range(N_CHAINS)]
            for ci, (jh, vh, d) in enumerate(chains):
                if write_out:
                    o_scr[pl.ds(r0s[d], CHUNK), value_cols(jh, vh)] += r1[ci][CHUNK:2 * CHUNK] + r2[ci][0:CHUNK]
                s_scr[ci] = s_old[ci] * dl_scr[slot, ci] + r2[ci][-HEAD_DIM:]

        prepare(0, 0)

        def pair_of_steps(m, carry):
            i = 2 * m
            advance(i, 0)
            prepare(i + 1, 1)
            advance(i + 1, 1)
            prepare(jnp.minimum(i + 2, n_chunks - 1), 0)
            return carry

        lax.fori_loop(0, n_chunks // 2, pair_of_steps, 0)

    run(qc_ref, kc_ref, vc_ref, gcc_ref, grc_ref, False)
    run(qx_ref, kx_ref, vx_ref, gcx_ref, grx_ref, True)
    o_ref[0] = o_scr[...].astype(o_ref.dtype)


def _gate_rows(gcol):
    b, t, n = gcol.shape
    g = gcol.reshape(b, t // CHUNK, CHUNK, n // GATE_LANES, GATE_LANES)
    return g.transpose(0, 3, 1, 4, 2)


def _dn_chunk(qc, kc, vc, gcc, qx, kx, vx, gcx):
    b, s, _ = qx.shape
    l = qc.shape[1]
    assert l % (2 * CHUNK) == 0 and s % (2 * CHUNK) == 0, "the chunk loop advances two chunks per trip"
    grc, grx = _gate_rows(gcc), _gate_rows(gcx)
    qk_cols = HEADS_PER_CALL * HEAD_DIM
    v_cols = 2 * qk_cols

    def seq_specs(t):
        return [pl.BlockSpec((1, t, qk_cols), lambda bi, j: (bi, 0, j)),
                pl.BlockSpec((1, t, qk_cols), lambda bi, j: (bi, 0, j)),
                pl.BlockSpec((1, t, v_cols), lambda bi, j: (bi, 0, j)),
                pl.BlockSpec((1, t, LANES), lambda bi, j: (bi, 0, 0)),
                pl.BlockSpec((1, HEADS_PER_CALL, t // CHUNK, GATE_LANES, CHUNK), lambda bi, j: (bi, j, 0, 0, 0))]

    return pl.pallas_call(
        _chunk_body,
        grid=(b, DN_QK_HEADS // HEADS_PER_CALL),
        in_specs=seq_specs(l) + seq_specs(s),
        out_specs=pl.BlockSpec((1, s, v_cols), lambda bi, j: (bi, 0, j)),
        out_shape=jax.ShapeDtypeStruct((b, s, DN_V_HEADS * HEAD_DIM), BF16),
        scratch_shapes=[pltpu.VMEM((N_CHAINS, HEAD_DIM, HEAD_DIM), F32),
                        pltpu.VMEM((s, v_cols), F32),
                        pltpu.VMEM((2, N_CHAINS, CHUNK, HEAD_DIM), F32),
                        pltpu.VMEM((2, N_CHAINS, 2 * CHUNK, HEAD_DIM), BF16),
                        pltpu.VMEM((2, N_CHAINS, CHUNK + HEAD_DIM, CHUNK), BF16),
                        pltpu.VMEM((2, N_CHAINS, 1, HEAD_DIM), F32)],
        compiler_params=_cparams("parallel", "parallel"),
        name="dn_chunk",
    )(qc, kc, vc, gcc, grc, qx, kx, vx, gcx, grx)


def _rot_cols(w):
    q = MLA_ROPE_DIM // 4
    return jnp.concatenate([-w[..., q:2 * q], w[..., 0:q], -w[..., 3 * q:4 * q], w[..., 2 * q:3 * q]], axis=-1)


def _rope_cos_sin(t_len):
    q = MLA_ROPE_DIM // 4
    pos = np.arange(t_len)
    inv = ROPE_BASE ** (-np.arange(q, dtype=np.float32) / q)
    ang_r = (pos // GRID_W).astype(np.float32)[:, None] * inv[None, :]
    ang_c = (pos % GRID_W).astype(np.float32)[:, None] * inv[None, :]
    ang = jnp.asarray(np.concatenate([ang_r, ang_r, ang_c, ang_c], axis=1))
    return jnp.cos(ang), jnp.sin(ang)


def _q_tables(cos, sin):
    t = cos.shape[0]
    one, zero = jnp.ones((t, HEAD_DIM), F32), jnp.zeros((t, MLA_ROPE_DIM), F32)
    c1 = jnp.concatenate([one, cos, zero], axis=1) * MLA_SCALE
    c2 = jnp.concatenate([jnp.zeros((t, HEAD_DIM), F32), sin, zero], axis=1) * MLA_SCALE
    return c1, c2


def _k_tables(cos, sin):
    zero = jnp.zeros_like(cos)
    return jnp.concatenate([cos, zero], axis=1), jnp.concatenate([sin, zero], axis=1)


def _mod_params(c, c_ctx, w_mod, b_mod):
    b, d = c.shape
    rows = ((b + 1 + 7) // 8) * 8
    cond = jnp.concatenate([c, c_ctx[None], jnp.zeros((rows - b - 1, d), F32)], axis=0)
    mod = _adaln(cond, w_mod, b_mod)
    sh, sc, gt = (mod[:b, i * d:(i + 1) * d].reshape(b, 1, d) for i in range(3))
    sh_c, sc_c, gt_c = (jnp.broadcast_to(mod[b, i * d:(i + 1) * d].reshape(1, 1, d), (b, 1, d)) for i in range(3))
    return (sh, sc, gt), (sh_c, sc_c, gt_c)


def _na_mla_layer(x, ctx, c, c_ctx, w_mod, b_mod, norm, w_in, rpb, q_norm, w_qb, kv_norm, w_kvb, w_out):
    b, s, d = x.shape
    l = ctx.shape[1]
    (sh, sc, gt), (sh_c, sc_c, gt_c) = _mod_params(c, c_ctx, w_mod, b_mod)
    na_w = NA_HEADS * HEAD_DIM
    n_main = 3 * na_w + 2 * MLA_RANK
    kr_w = w_in[:, n_main:n_main + MLA_ROPE_DIM]
    w_main = jnp.concatenate([w_in[:, :n_main], w_in[:, n_main + MLA_ROPE_DIM:]], axis=1).astype(BF16)
    w_small = jnp.concatenate([kr_w, _rot_cols(kr_w)], axis=1).astype(BF16)
    cos, sin = _rope_cos_sin(s)
    ident = (jnp.ones((l, MLA_ROPE_DIM), F32), jnp.zeros((l, MLA_ROPE_DIM), F32))
    hx, krx = _norm_proj(x, 0, d, norm, w_main, tm=IN_PROJ_ROWS, tn=1024, shift=sh, scale=sc, w_small=w_small,
                         rope_small=_k_tables(cos, sin), small_dtype=BF16, name="in_proj_ab_x")
    ident_flat = tuple(jnp.tile(t, (b, 1)) for t in ident)
    hc, krc = _norm_proj(ctx.reshape(1, b * l, d), 0, d, norm, w_main, tm=IN_PROJ_ROWS, tn=1024, shift=sh_c[:1],
                         scale=sc_c[:1], w_small=w_small, rope_small=_k_tables(*ident_flat), small_dtype=BF16,
                         name="in_proj_ab_ctx")
    hc, krc = hc.reshape(b, l, -1), krc.reshape(b, l, -1)

    wq = w_qb.reshape(MLA_RANK, MLA_HEADS, HEAD_DIM + MLA_ROPE_DIM)
    wq_r = wq[..., HEAD_DIM:]
    wq_p = jnp.concatenate([wq[..., :HEAD_DIM], wq_r, _rot_cols(wq_r)], axis=-1)
    wq_p = wq_p.reshape(MLA_RANK, MLA_HEADS * MLA_QK_PAD).astype(BF16)
    wkv = w_kvb.reshape(MLA_RANK, MLA_HEADS, 2 * HEAD_DIM)
    wkv_p = jnp.concatenate([wkv[..., :HEAD_DIM].reshape(MLA_RANK, -1), wkv[..., HEAD_DIM:].reshape(MLA_RANK, -1)],
                            axis=1).astype(BF16)
    cq_blk = 3 * na_w // MLA_RANK
    nq = MLA_HEADS * MLA_QK_PAD
    qx = _norm_proj(hx, cq_blk, MLA_RANK, q_norm, wq_p, tm=1024, tn=nq, rope_main=_q_tables(cos, sin), name="mla_q_x")
    qc = _norm_proj(hc, cq_blk, MLA_RANK, q_norm, wq_p, tm=1024, tn=nq, rope_main=_q_tables(*ident), name="mla_q_ctx")
    kvx = _norm_proj(hx, cq_blk + 1, MLA_RANK, kv_norm, wkv_p, tm=1024, tn=nq, name="mla_kv_x")
    kvc = _norm_proj(hc, cq_blk + 1, MLA_RANK, kv_norm, wkv_p, tm=1024, tn=nq, name="mla_kv_ctx")

    o_na = _na_attention(hx, hc, _na_bias_table(rpb, s // GRID_W))
    seg_x = (kvx, 0, krx, kvx, MLA_HEADS)
    seg_c = (kvc, 0, krc, kvc, MLA_HEADS)
    o_mla = _attention(qx, 0, MLA_QK_PAD, [seg_x, seg_c], MLA_HEADS, tq=1024, scale=1.0, name="mla_attention")
    z_blk0 = n_main // na_w
    w_out_b = w_out.astype(BF16)
    x_new = _out_proj_ab(x, gt, o_na, o_mla, hx, z_blk0, w_out_b, tm=512)

    o_na_c = _attention(hc, 0, HEAD_DIM, [(hc, NA_HEADS, None, hc, 2 * NA_HEADS)], NA_HEADS, tq=l,
                        scale=HEAD_DIM ** -0.5, name="na_ctx_attention")
    o_mla_c = _attention(qc, 0, MLA_QK_PAD, [seg_c], MLA_HEADS, tq=l, scale=1.0, name="mla_ctx_attention")
    ctx_new = _out_proj_ab(ctx, gt_c, o_na_c, o_mla_c, hc, z_blk0, w_out_b, tm=512)
    return x_new, ctx_new


def _deltanet_final_layer(x, ctx, c, c_ctx, w_mod, b_mod, norm, w_in, conv_w, a_log, dt_bias, o_norm, w_out, final_norm):
    b, s, d = x.shape
    (sh, sc, gt), (sh_c, sc_c, _) = _mod_params(c, c_ctx, w_mod, b_mod)
    qk_w = DN_QK_HEADS * HEAD_DIM
    v_w = DN_V_HEADS * HEAD_DIM
    n_main = 2 * qk_w + 2 * v_w
    lane = np.arange(4 * DN_V_HEADS)
    perm = (lane % 4) * DN_V_HEADS + lane // 4
    w_main = w_in[:, :n_main].astype(BF16)
    w_small = w_in[:, n_main:][:, perm].astype(BF16)
    zeros = jnp.zeros((2 * DN_V_HEADS,), F32)
    alog = jnp.concatenate([zeros, a_log.reshape(-1).astype(F32)])[perm].reshape(1, -1)
    dtb = jnp.concatenate([zeros, dt_bias.reshape(-1).astype(F32)])[perm].reshape(1, -1)
    conv_w = conv_w.astype(F32)

    def inputs(t, shift, scale, name, shared_modulation=False):
        bt, tl, _ = t.shape
        if shared_modulation:
            t, shift, scale = t.reshape(1, bt * tl, d), shift[:1], scale[:1]
        h, hs = _norm_proj(t, 0, d, norm, w_main, tm=IN_PROJ_ROWS, tn=1024, shift=shift, scale=scale, w_small=w_small,
                           name=name)
        h, hs = h.reshape(bt, tl, -1), hs.reshape(bt, tl, -1)
        q = _dn_conv(h, conv_w, 0, qk_w, HEAD_DIM ** -0.5, tt=512)
        k = _dn_conv(h, conv_w, qk_w, qk_w, 1.0, tt=512)
        v = _dn_conv(h, conv_w, 2 * qk_w, v_w, None, tt=512)
        return h, q, k, v, _dn_gates(hs, alog, dtb)

    hx, qx, kx, vx, gx = inputs(x, sh, sc, "in_proj_dn_x")
    _, qc, kc, vc, gc = inputs(ctx, sh_c, sc_c, "in_proj_dn_ctx", shared_modulation=True)
    o = _dn_chunk(qc, kc, vc, gc, qx, kx, vx, gx)
    return _out_proj_dn(x, gt, o, hx, (2 * qk_w + v_w) // v_w, o_norm, w_out.astype(BF16), final_norm, tm=256)


def kernel(x, c, ctx, c_ctx, ab_w_mod, ab_b_mod, ab_norm, ab_w_in, ab_rpb, ab_q_norm, ab_w_qb, ab_kv_norm, ab_w_kvb, ab_w_out, dn_w_mod, dn_b_mod, dn_norm, dn_w_in, dn_conv, dn_a_log, dn_dt_bias, dn_o_norm, dn_w_out, final_norm):
    assert ab_w_mod.shape[0] == 1 and dn_w_mod.shape[0] == 1, "depth-2 trunk: one even and one odd layer"
    x, ctx = _na_mla_layer(x, ctx, c, c_ctx, ab_w_mod[0], ab_b_mod[0], ab_norm[0], ab_w_in[0], ab_rpb[0],
                           ab_q_norm[0], ab_w_qb[0], ab_kv_norm[0], ab_w_kvb[0], ab_w_out[0])
    return _deltanet_final_layer(x, ctx, c, c_ctx, dn_w_mod[0], dn_b_mod[0], dn_norm[0], dn_w_in[0], dn_conv[0],
                                 dn_a_log[0], dn_dt_bias[0], dn_o_norm[0], dn_w_out[0], final_norm)
```

```python
import functools

import numpy as np
import jax
import jax.numpy as jnp
from jax import lax
from jax.experimental import pallas as pl
from jax.experimental.pallas import tpu as pltpu

F32 = jnp.float32
BF16 = jnp.bfloat16

EPS = 1e-6
GRID_W = 64
WIN_R = 8
WIN_C = 16
NA_HEADS = 8
HEAD_DIM = 128
MLA_HEADS = 8
MLA_RANK = 512
MLA_ROPE_DIM = 64
MLA_SCALE = (HEAD_DIM + MLA_ROPE_DIM) ** -0.5
ROPE_BASE = 10000.0
DN_QK_HEADS = 16
DN_V_HEADS = 32
CONV_K = 5
CHUNK = 128
NEG_BIG = -1e30

LANES = 128
MLA_QK_PAD = 256
NA_Q_ROWS = 8
NA_KEY_ROWS = 16
ATTN_SUB_ROWS = 256
IN_PROJ_ROWS = 1024
VMEM_LIMIT = 56 * 1024 * 1024


def _cparams(*sem):
    return pltpu.CompilerParams(dimension_semantics=sem, vmem_limit_bytes=VMEM_LIMIT)


def _adaln_body(c_ref, w_ref, b_ref, o_ref):
    cf = c_ref[...]
    a = (cf * jax.nn.sigmoid(cf)).astype(BF16)
    o_ref[...] = jnp.dot(a, w_ref[...].astype(BF16), preferred_element_type=F32) + b_ref[...]


def _adaln(cond, w_mod, b_mod):
    r, d = cond.shape
    n = w_mod.shape[1]
    tn = 512
    return pl.pallas_call(
        _adaln_body,
        grid=(n // tn,),
        in_specs=[pl.BlockSpec((r, d), lambda j: (0, 0)),
                  pl.BlockSpec((d, tn), lambda j: (0, j)),
                  pl.BlockSpec((1, tn), lambda j: (0, j))],
        out_specs=pl.BlockSpec((r, tn), lambda j: (0, j)),
        out_shape=jax.ShapeDtypeStruct((r, n), F32),
        compiler_params=_cparams("parallel"),
        name="adaln",
    )(cond, w_mod, b_mod.reshape(1, n))


def _proj_body(*refs, modulate, has_small, rope_main, rope_small):
    it = iter(refs)
    x_ref, gain_ref = next(it), next(it)
    shift_ref = scale_ref = ws_ref = c1_ref = c2_ref = s1_ref = s2_ref = os_ref = None
    if modulate:
        shift_ref, scale_ref = next(it), next(it)
    w_ref = next(it)
    if has_small:
        ws_ref = next(it)
    if rope_main:
        c1_ref, c2_ref = next(it), next(it)
    if rope_small:
        s1_ref, s2_ref = next(it), next(it)
    o_ref = next(it)
    if has_small:
        os_ref = next(it)
    xm_ref = next(it)

    @pl.when(pl.program_id(2) == 0)
    def _():
        xf = x_ref[0].astype(F32)
        ms = jnp.mean(xf * xf, axis=-1, keepdims=True)
        y = xf * lax.rsqrt(ms + EPS) * gain_ref[...]
        if modulate:
            y = y * (1.0 + scale_ref[0]) + shift_ref[0]
        xm = y.astype(BF16)
        xm_ref[...] = xm
        if has_small:
            s = jnp.dot(xm, ws_ref[...], preferred_element_type=F32)
            if rope_small:
                s = s * s1_ref[...] + pltpu.roll(s, LANES // 2, axis=1) * s2_ref[...]
            os_ref[0] = s.astype(os_ref.dtype)

    acc = jnp.dot(xm_ref[...], w_ref[...], preferred_element_type=F32)
    if rope_main:
        for h in range(acc.shape[1] // MLA_QK_PAD):
            a = acc[:, h * MLA_QK_PAD:(h + 1) * MLA_QK_PAD]
            a = a * c1_ref[...] + pltpu.roll(a, MLA_QK_PAD - MLA_ROPE_DIM, axis=1) * c2_ref[...]
            o_ref[0, :, h * MLA_QK_PAD:(h + 1) * MLA_QK_PAD] = a.astype(o_ref.dtype)
    else:
        o_ref[0] = acc.astype(o_ref.dtype)


def _norm_proj(x, kblk, k, gain, w, *, tm, tn, shift=None, scale=None, w_small=None,
               rope_main=None, rope_small=None, small_dtype=F32, name="proj"):
    b, t, _ = x.shape
    n = w.shape[1]
    modulate = shift is not None
    has_small = w_small is not None
    tm = min(tm, t)
    args = [x, gain.reshape(1, k).astype(F32)]
    in_specs = [pl.BlockSpec((1, tm, k), lambda bi, i, j: (bi, i, kblk)),
                pl.BlockSpec((1, k), lambda bi, i, j: (0, 0))]
    if modulate:
        args += [shift, scale]
        in_specs += [pl.BlockSpec((1, 1, k), lambda bi, i, j: (bi, 0, 0))] * 2
    args.append(w)
    in_specs.append(pl.BlockSpec((k, tn), lambda bi, i, j: (0, j)))
    if has_small:
        ns = w_small.shape[1]
        args.append(w_small)
        in_specs.append(pl.BlockSpec((k, ns), lambda bi, i, j: (0, 0)))
    if rope_main is not None:
        args += list(rope_main)
        in_specs += [pl.BlockSpec((tm, MLA_QK_PAD), lambda bi, i, j: (i, 0))] * 2
    if rope_small is not None:
        args += list(rope_small)
        in_specs += [pl.BlockSpec((tm, LANES), lambda bi, i, j: (i, 0))] * 2
    out_shape = [jax.ShapeDtypeStruct((b, t, n), BF16)]
    out_specs = [pl.BlockSpec((1, tm, tn), lambda bi, i, j: (bi, i, j))]
    if has_small:
        out_shape.append(jax.ShapeDtypeStruct((b, t, ns), small_dtype))
        out_specs.append(pl.BlockSpec((1, tm, ns), lambda bi, i, j: (bi, i, 0)))
    body = functools.partial(_proj_body, modulate=modulate, has_small=has_small,
                             rope_main=rope_main is not None, rope_small=rope_small is not None)
    out = pl.pallas_call(
        body,
        grid=(b, t // tm, n // tn),
        in_specs=in_specs,
        out_specs=out_specs,
        out_shape=out_shape,
        scratch_shapes=[pltpu.VMEM((tm, k), BF16)],
        compiler_params=_cparams("parallel", "parallel", "arbitrary"),
        name=name,
    )(*args)
    return out if has_small else out[0]


def _attn_body(*refs, seg_lens, has_extra, scale):
    it = iter(refs)
    q_ref = next(it)
    segs = []
    for _ in seg_lens:
        km = next(it)
        ke = next(it) if has_extra else None
        segs.append((km, ke, next(it)))
    o_ref, k_scr, v_scr = next(it), next(it), next(it)

    @pl.when(pl.program_id(2) == 0)
    def _():
        off = 0
        for (km, ke, v), ln in zip(segs, seg_lens):
            k_scr[off:off + ln, 0:HEAD_DIM] = km[0]
            if has_extra:
                k_scr[off:off + ln, HEAD_DIM:2 * HEAD_DIM] = ke[0]
            v_scr[off:off + ln, :] = v[0]
            off += ln

    sub = min(ATTN_SUB_ROWS, q_ref.shape[1])
    n_sub = q_ref.shape[1] // sub

    def scores(j):
        s = lax.dot_general(q_ref[0, j * sub:(j + 1) * sub, :], k_scr[...], (((1,), (1,)), ((), ())),
                            preferred_element_type=F32)
        return s if scale == 1.0 else s * scale

    s_next = scores(0)
    for j in range(n_sub):
        s = s_next
        if j + 1 < n_sub:
            s_next = scores(j + 1)
        m = jnp.max(s, axis=-1, keepdims=True)
        e = jnp.exp(s - m)
        l = jnp.sum(e, axis=-1, keepdims=True)
        o = jnp.dot(e.astype(BF16), v_scr[...], preferred_element_type=F32)
        o_ref[0, j * sub:(j + 1) * sub, :] = (o / l).astype(o_ref.dtype)


def _attention(q, q_blk0, dq, segs, heads, *, tq, scale, name):
    b, t, _ = q.shape
    tq = min(tq, t)
    has_extra = segs[0][2] is not None
    seg_lens = tuple(s[0].shape[1] for s in segs)
    tk = sum(seg_lens)
    args = [q]
    in_specs = [pl.BlockSpec((1, tq, dq), lambda bi, h, i: (bi, i, q_blk0 + h))]
    for km, kb0, ke, v, vb0 in segs:
        ln = km.shape[1]
        args.append(km)
        in_specs.append(pl.BlockSpec((1, ln, HEAD_DIM), lambda bi, h, i, kb0=kb0: (bi, 0, kb0 + h)))
        if has_extra:
            args.append(ke)
            in_specs.append(pl.BlockSpec((1, ln, HEAD_DIM), lambda bi, h, i: (bi, 0, 0)))
        args.append(v)
        in_specs.append(pl.BlockSpec((1, ln, HEAD_DIM), lambda bi, h, i, vb0=vb0: (bi, 0, vb0 + h)))
    body = functools.partial(_attn_body, seg_lens=seg_lens, has_extra=has_extra, scale=scale)
    return pl.pallas_call(
        body,
        grid=(b, heads, t // tq),
        in_specs=in_specs,
        out_specs=pl.BlockSpec((1, tq, HEAD_DIM), lambda bi, h, i: (bi, i, h)),
        out_shape=jax.ShapeDtypeStruct((b, t, heads * HEAD_DIM), BF16),
        scratch_shapes=[pltpu.VMEM((tk, dq), BF16), pltpu.VMEM((tk, HEAD_DIM), BF16)],
        compiler_params=_cparams("parallel", "parallel", "arbitrary"),
        name=name,
    )(*args)


def _na_geometry(rows):
    groups = rows // NA_Q_ROWS
    key_rows = min(NA_KEY_ROWS, rows)
    return groups, key_rows


def _na_bias_table(rpb, rows):
    heads = rpb.shape[0]
    groups, key_rows = _na_geometry(rows)
    g = np.arange(groups)
    slab0 = np.clip(NA_Q_ROWS * g - WIN_R // 2, 0, rows - key_rows)
    r = NA_Q_ROWS * g[:, None] + np.arange(NA_Q_ROWS)[None]
    k_abs = slab0[:, None] + np.arange(key_rows)[None]
    win0 = np.clip(r - WIN_R // 2, 0, rows - WIN_R)
    d_row = k_abs[:, None, :] - r[:, :, None] + (WIN_R - 1)
    ok_row = (k_abs[:, None, :] >= win0[:, :, None]) & (k_abs[:, None, :] < win0[:, :, None] + WIN_R)
    col = np.arange(GRID_W)
    c0 = np.clip(col - WIN_C // 2, 0, GRID_W - WIN_C)
    d_col = col[None, :] - col[:, None] + (WIN_C - 1)
    ok_col = (col[None, :] >= c0[:, None]) & (col[None, :] < c0[:, None] + WIN_C)
    sel_row = (np.clip(d_row, 0, 2 * WIN_R - 2)[..., None] == np.arange(2 * WIN_R - 1)).astype(np.float32)
    sel_col = (np.clip(d_col, 0, 2 * WIN_C - 2)[..., None] == np.arange(2 * WIN_C - 1)).astype(np.float32)
    ok = jnp.asarray(ok_row)[:, :, None, :, None] & jnp.asarray(ok_col)[None, None, :, None, :]
    t = jnp.einsum("grka,hab,qcb->hgrqkc", jnp.asarray(sel_row), rpb.astype(F32), jnp.asarray(sel_col),
                   precision=lax.Precision.HIGHEST)
    t = jnp.where(ok[None], t, NEG_BIG)
    return t.reshape(heads, groups, NA_Q_ROWS * GRID_W, key_rows * GRID_W)


def _na_body(q_ref, k_ref, v_ref, kc_ref, vc_ref, bias_ref, o_ref, *, rows, scale):
    groups, key_rows = _na_geometry(rows)
    g = pl.program_id(2)
    slab0 = jnp.clip(NA_Q_ROWS * g - WIN_R // 2, 0, rows - key_rows)
    start = pl.multiple_of(slab0 * GRID_W, 4 * GRID_W)
    nk = key_rows * GRID_W
    nt = (((1,), (1,)), ((), ()))
    sub = min(ATTN_SUB_ROWS, q_ref.shape[1])
    n_sub = q_ref.shape[1] // sub

    def scores(j):
        q = q_ref[0, j * sub:(j + 1) * sub, :]
        s_nb = lax.dot_general(q, k_ref[0, pl.ds(start, nk), :], nt, preferred_element_type=F32)
        s_cx = lax.dot_general(q, kc_ref[0], nt, preferred_element_type=F32)
        return s_nb, s_cx

    s_next = scores(0)
    for j in range(n_sub):
        s_nb, s_cx = s_next
        if j + 1 < n_sub:
            s_next = scores(j + 1)
        s_nb = s_nb * scale + bias_ref[0, g, j * sub:(j + 1) * sub, :]
        s_cx = s_cx * scale
        m = jnp.maximum(jnp.max(s_nb, axis=-1, keepdims=True), jnp.max(s_cx, axis=-1, keepdims=True))
        e_nb = jnp.exp(s_nb - m)
        e_cx = jnp.exp(s_cx - m)
        l = jnp.sum(e_nb, axis=-1, keepdims=True) + jnp.sum(e_cx, axis=-1, keepdims=True)
        o = jnp.dot(e_nb.astype(BF16), v_ref[0, pl.ds(start, nk), :], preferred_element_type=F32)
        o = o + jnp.dot(e_cx.astype(BF16), vc_ref[0], preferred_element_type=F32)
        o_ref[0, j * sub:(j + 1) * sub, :] = (o / l).astype(o_ref.dtype)


def _na_attention(hx, hc, bias):
    b, s, _ = hx.shape
    l = hc.shape[1]
    rows = s // GRID_W
    groups, key_rows = _na_geometry(rows)
    tq = NA_Q_ROWS * GRID_W
    body = functools.partial(_na_body, rows=rows, scale=HEAD_DIM ** -0.5)
    return pl.pallas_call(
        body,
        grid=(NA_HEADS, b, groups),
        in_specs=[pl.BlockSpec((1, tq, HEAD_DIM), lambda h, bi, g: (bi, g, h)),
                  pl.BlockSpec((1, s, HEAD_DIM), lambda h, bi, g: (bi, 0, NA_HEADS + h)),
                  pl.BlockSpec((1, s, HEAD_DIM), lambda h, bi, g: (bi, 0, 2 * NA_HEADS + h)),
                  pl.BlockSpec((1, l, HEAD_DIM), lambda h, bi, g: (bi, 0, NA_HEADS + h)),
                  pl.BlockSpec((1, l, HEAD_DIM), lambda h, bi, g: (bi, 0, 2 * NA_HEADS + h)),
                  pl.BlockSpec((1, groups, tq, key_rows * GRID_W), lambda h, bi, g: (h, 0, 0, 0))],
        out_specs=pl.BlockSpec((1, tq, HEAD_DIM), lambda h, bi, g: (bi, g, h)),
        out_shape=jax.ShapeDtypeStruct((b, s, NA_HEADS * HEAD_DIM), BF16),
        compiler_params=_cparams("parallel", "parallel", "parallel"),
        name="na_attention",
    )(hx, hx, hx, hc, hc, bias)


def _silu(x):
    return x * jax.nn.sigmoid(x)


def _out_ab_body(x_ref, gt_ref, o1_ref, o2_ref, z1_ref, z2_ref, w_ref, y_ref):
    half = o1_ref.shape[2]
    a1 = (o1_ref[0].astype(F32) * _silu(z1_ref[0].astype(F32))).astype(BF16)
    acc = jnp.dot(a1, w_ref[0:half, :], preferred_element_type=F32)
    a2 = (o2_ref[0].astype(F32) * _silu(z2_ref[0].astype(F32))).astype(BF16)
    acc = acc + jnp.dot(a2, w_ref[half:2 * half, :], preferred_element_type=F32)
    y_ref[0] = x_ref[0] + gt_ref[0] * acc


def _out_proj_ab(x, gt, o1, o2, h, z_blk0, w, *, tm):
    b, t, d = x.shape
    half = o1.shape[2]
    tm = min(tm, t)
    return pl.pallas_call(
        _out_ab_body,
        grid=(b, t // tm),
        in_specs=[pl.BlockSpec((1, tm, d), lambda bi, i: (bi, i, 0)),
                  pl.BlockSpec((1, 1, d), lambda bi, i: (bi, 0, 0)),
                  pl.BlockSpec((1, tm, half), lambda bi, i: (bi, i, 0)),
                  pl.BlockSpec((1, tm, half), lambda bi, i: (bi, i, 0)),
                  pl.BlockSpec((1, tm, half), lambda bi, i: (bi, i, z_blk0)),
                  pl.BlockSpec((1, tm, half), lambda bi, i: (bi, i, z_blk0 + 1)),
                  pl.BlockSpec((2 * half, d), lambda bi, i: (0, 0))],
        out_specs=pl.BlockSpec((1, tm, d), lambda bi, i: (bi, i, 0)),
        out_shape=jax.ShapeDtypeStruct((b, t, d), F32),
        compiler_params=_cparams("parallel", "parallel"),
        name="out_proj_ab",
    )(x, gt, o1, o2, h, h, w)


def _out_dn_body(x_ref, gt_ref, o_ref, z_ref, on_ref, w_ref, fn_ref, y_ref, a_scr):
    heads = o_ref.shape[2] // HEAD_DIM
    parts = 2
    acc = None
    for p in range(parts):
        for h in range(p * heads // parts, (p + 1) * heads // parts):
            sl = slice(h * HEAD_DIM, (h + 1) * HEAD_DIM)
            oh = o_ref[0, :, sl].astype(F32)
            ms = jnp.mean(oh * oh, axis=-1, keepdims=True)
            a = oh * lax.rsqrt(ms + EPS) * on_ref[...] * _silu(z_ref[0, :, sl].astype(F32))
            a_scr[:, sl] = a.astype(BF16)
        cols = slice(p * heads // parts * HEAD_DIM, (p + 1) * heads // parts * HEAD_DIM)
        part = jnp.dot(a_scr[:, cols], w_ref[cols, :], preferred_element_type=F32)
        acc = part if acc is None else acc + part
    y = x_ref[0] + gt_ref[0] * acc
    ms = jnp.mean(y * y, axis=-1, keepdims=True)
    y_ref[0] = y * lax.rsqrt(ms + EPS) * fn_ref[...]


def _out_proj_dn(x, gt, o, h, z_blk, o_norm, w, final_norm, *, tm):
    b, t, d = x.shape
    km = o.shape[2]
    tm = min(tm, t)
    return pl.pallas_call(
        _out_dn_body,
        grid=(b, t // tm),
        in_specs=[pl.BlockSpec((1, tm, d), lambda bi, i: (bi, i, 0)),
                  pl.BlockSpec((1, 1, d), lambda bi, i: (bi, 0, 0)),
                  pl.BlockSpec((1, tm, km), lambda bi, i: (bi, i, 0)),
                  pl.BlockSpec((1, tm, km), lambda bi, i: (bi, i, z_blk)),
                  pl.BlockSpec((1, HEAD_DIM), lambda bi, i: (0, 0)),
                  pl.BlockSpec((km, d), lambda bi, i: (0, 0)),
                  pl.BlockSpec((1, d), lambda bi, i: (0, 0))],
        out_specs=pl.BlockSpec((1, tm, d), lambda bi, i: (bi, i, 0)),
        out_shape=jax.ShapeDtypeStruct((b, t, d), F32),
        scratch_shapes=[pltpu.VMEM((tm, km), BF16)],
        compiler_params=_cparams("parallel", "parallel"),
        name="out_proj_dn",
    )(x, gt, o, h, o_norm.reshape(1, HEAD_DIM).astype(F32), w, final_norm.reshape(1, d).astype(F32))


CONV_HALO = 16


SUBLANES = 8
SEG_PAD = 8


def _conv_body(xp_ref, x_ref, xn_ref, w_ref, o_ref, in_scr, out_scr, *, l2_scale):
    i = pl.program_id(1)
    nt = pl.num_programs(1)
    tt = x_ref.shape[1]
    seg = tt // SUBLANES
    pitch = seg + SEG_PAD
    half = CONV_K // 2
    prev = jnp.where(i > 0, xp_ref[0].astype(F32), 0.0)
    nxt = jnp.where(i < nt - 1, xn_ref[0].astype(F32), 0.0)
    for sl in range(x_ref.shape[2] // LANES):
        lanes = slice(sl * LANES, (sl + 1) * LANES)
        in_scr[sl, seg - SUBLANES:seg, :] = prev[CONV_HALO - SUBLANES:, lanes]
        for s in range(SUBLANES):
            in_scr[sl, (s + 1) * pitch:(s + 1) * pitch + seg, :] = x_ref[0, s * seg:(s + 1) * seg, lanes].astype(F32)
        in_scr[sl, (SUBLANES + 1) * pitch:(SUBLANES + 1) * pitch + SUBLANES, :] = nxt[:SUBLANES, lanes]
        w = [w_ref[j:j + 1, lanes] for j in range(CONV_K)]

        def strided(n):
            start = pitch + n - (SEG_PAD if n < 0 else 0) + (SEG_PAD if n >= seg else 0)
            return in_scr[sl, pl.ds(start, SUBLANES, stride=pitch), :]

        window = [strided(n) for n in range(-half, half)]
        for m in range(seg):
            window.append(strided(m + half))
            acc = window[0] * w[0]
            for j in range(1, CONV_K):
                acc = acc + window[j] * w[j]
            window.pop(0)
            y = _silu(acc)
            if l2_scale is not None:
                y = y * (lax.rsqrt(jnp.sum(y * y, axis=-1, keepdims=True) + EPS) * l2_scale)
            out_scr[sl, pl.ds(m, SUBLANES, stride=pitch), :] = y
        for s in range(SUBLANES):
            o_ref[0, s * seg:(s + 1) * seg, lanes] = out_scr[sl, s * pitch:s * pitch + seg, :].astype(o_ref.dtype)


def _dn_conv(h, conv_w, col0, width, l2_scale, *, tt, cb=1024):
    b, t, _ = h.shape
    tt = min(tt, t)
    blk0 = col0 // cb
    hb = tt // CONV_HALO
    nhalo = t // CONV_HALO
    pitch = tt // SUBLANES + SEG_PAD
    body = functools.partial(_conv_body, l2_scale=l2_scale)
    return pl.pallas_call(
        body,
        grid=(b, t // tt, width // cb),
        in_specs=[pl.BlockSpec((1, CONV_HALO, cb), lambda bi, i, c: (bi, jnp.maximum(i * hb - 1, 0), blk0 + c)),
                  pl.BlockSpec((1, tt, cb), lambda bi, i, c: (bi, i, blk0 + c)),
                  pl.BlockSpec((1, CONV_HALO, cb), lambda bi, i, c: (bi, jnp.minimum((i + 1) * hb, nhalo - 1), blk0 + c)),
                  pl.BlockSpec((CONV_K, cb), lambda bi, i, c: (0, blk0 + c))],
        out_specs=pl.BlockSpec((1, tt, cb), lambda bi, i, c: (bi, i, c)),
        out_shape=jax.ShapeDtypeStruct((b, t, width), BF16),
        scratch_shapes=[pltpu.VMEM((cb // LANES, (SUBLANES + 2) * pitch, LANES), F32),
                        pltpu.VMEM((cb // LANES, SUBLANES * pitch, LANES), F32)],
        compiler_params=_cparams("parallel", "parallel", "parallel"),
        name="dn_conv",
    )(h, h, h, conv_w)


GATE_TILE = 256


def _split3(x):
    hi = x.astype(BF16)
    r1 = x - hi.astype(F32)
    mid = r1.astype(BF16)
    lo = (r1 - mid.astype(F32)).astype(BF16)
    return hi, mid, lo


def _gates_body(h_ref, alog_ref, dtb_ref, o_ref):
    t = h_ref[0]
    lane = lax.broadcasted_iota(jnp.int32, t.shape, 1) % 4
    beta = jax.nn.sigmoid(t)
    z = t + dtb_ref[...]
    softplus = jnp.maximum(z, 0.0) + jnp.log1p(jnp.exp(-jnp.abs(z)))
    g = -jnp.exp(alog_ref[...]) * softplus
    n = t.shape[0]
    r = lax.broadcasted_iota(jnp.int32, (n, n), 0)
    c = lax.broadcasted_iota(jnp.int32, (n, n), 1)
    same = (r // CHUNK) == (c // CHUNK)
    tri_f = jnp.where(same & (r >= c), 1.0, 0.0).astype(BF16)
    tri_b = jnp.where(same & (r <= c), 1.0, 0.0).astype(BF16)
    pf = pb = None
    for part in _split3(g):
        df = jnp.dot(tri_f, part, preferred_element_type=F32)
        db = jnp.dot(tri_b, part, preferred_element_type=F32)
        pf = df if pf is None else pf + df
        pb = db if pb is None else pb + db
    o_ref[0] = jnp.where(lane < 2, beta, jnp.where(lane == 2, pf, pb))


def _dn_gates(h_small, alog, dtb):
    b, t, n = h_small.shape
    tt = min(GATE_TILE, t)
    return pl.pallas_call(
        _gates_body,
        grid=(b, t // tt),
        in_specs=[pl.BlockSpec((1, tt, n), lambda bi, i: (bi, i, 0)),
                  pl.BlockSpec((1, n), lambda bi, i: (0, 0)),
                  pl.BlockSpec((1, n), lambda bi, i: (0, 0))],
        out_specs=pl.BlockSpec((1, tt, n), lambda bi, i: (bi, i, 0)),
        out_shape=jax.ShapeDtypeStruct((b, t, n), F32),
        compiler_params=_cparams("parallel", "parallel"),
        name="dn_gates",
    )(h_small, alog, dtb)


HEADS_PER_CALL = 4
N_CHAINS = 4 * HEADS_PER_CALL
GATE_LANES = 8
MERGE_LEVELS = tuple(4 << i for i in range((CHUNK // 4).bit_length() - 1))


def _bdot(a, b):
    return jnp.dot(a.astype(BF16), b.astype(BF16), preferred_element_type=F32)


def _unit_tri_inverses(lms, eye, level):
    l4s = [jnp.where(level == 0, lm, 0.0) for lm in lms]
    ts = [eye - l4 for l4 in l4s]
    sq = [_bdot(l4, l4) for l4 in l4s]
    ts = [t + _bdot(t, p) for t, p in zip(ts, sq)]
    for m in range(1, len(MERGE_LEVELS) + 1):
        cs = [jnp.where(level == m, lm, 0.0).astype(BF16) for lm in lms]
        tbs = [t.astype(BF16) for t in ts]
        xs = [jnp.dot(tb, c, preferred_element_type=F32) for tb, c in zip(tbs, cs)]
        ts = [t - _bdot(x, tb) for t, x, tb in zip(ts, xs, tbs)]
    return ts


def _chunk_body(qc_ref, kc_ref, vc_ref, gcc_ref, grc_ref, qx_ref, kx_ref, vx_ref, gcx_ref, grx_ref,
                o_ref, s_scr, o_scr, u_scr, wq_scr, ak_scr, dl_scr):
    jg = pl.program_id(1)
    group_lanes = GATE_LANES * HEADS_PER_CALL
    lane_shift = (LANES - group_lanes * jg) % LANES
    s_scr[...] = jnp.zeros_like(s_scr)
    o_scr[...] = jnp.zeros_like(o_scr)
    row = lax.broadcasted_iota(jnp.int32, (CHUNK, CHUNK), 0)
    col = lax.broadcasted_iota(jnp.int32, (CHUNK, CHUNK), 1)
    eye = jnp.where(row == col, 1.0, 0.0)
    level = jnp.zeros((CHUNK, CHUNK), jnp.int32)
    for size in MERGE_LEVELS:
        level = level + jnp.where((row // size) == (col // size), 0, 1)
    incl = (row >= col, row <= col)
    strict = (row > col, row < col)
    nt = (((1,), (1,)), ((), ()))
    chains = [(jh, vh, d) for d in range(2) for jh in range(HEADS_PER_CALL) for vh in range(2)]
    streams = [(jh, d) for d in range(2) for jh in range(HEADS_PER_CALL)]

    def head_cols(jh):
        return slice(jh * HEAD_DIM, (jh + 1) * HEAD_DIM)

    def value_cols(jh, vh):
        return slice((2 * jh + vh) * HEAD_DIM, (2 * jh + vh + 1) * HEAD_DIM)

    def run(q_ref, k_ref, v_ref, gc_ref, gr_ref, write_out):
        n_chunks = q_ref.shape[1] // CHUNK

        def chunk_of(i, d):
            n = i if d == 0 else n_chunks - 1 - i
            return n, (n * CHUNK if isinstance(n, int) else pl.multiple_of(n * CHUNK, CHUNK))

        def prepare(i, slot):
            ns, r0s, gcols = {}, {}, {}
            for d in range(2):
                ns[d], r0s[d] = chunk_of(i, d)
                gcols[d] = pltpu.roll(gc_ref[0, pl.ds(r0s[d], CHUNK), :], lane_shift, axis=1)
            ks = {(jh, d): k_ref[0, pl.ds(r0s[d], CHUNK), head_cols(jh)] for jh, d in streams}
            kks = {sd: lax.dot_general(ks[sd], ks[sd], nt, preferred_element_type=F32) for sd in streams}
            if write_out:
                qs = {(jh, d): q_ref[0, pl.ds(r0s[d], CHUNK), head_cols(jh)] for jh, d in streams}
                qks = {sd: lax.dot_general(qs[sd], ks[sd], nt, preferred_element_type=F32) for sd in streams}
            grows = {(jh, d): gr_ref[0, jh, ns[d]] for jh, d in streams}
            lane0 = [GATE_LANES * jh + 4 * vh + d for jh, vh, d in chains]
            beta_c = [gcols[d][:, l:l + 1] for l, (jh, vh, d) in zip(lane0, chains)]
            g_c = [gcols[d][:, l + 2:l + 3] for l, (jh, vh, d) in zip(lane0, chains)]
            beta_r = [grows[jh, d][4 * vh + d:4 * vh + d + 1, :] for jh, vh, d in chains]
            g_r = [grows[jh, d][4 * vh + 2 + d:4 * vh + 3 + d, :] for jh, vh, d in chains]
            g_last = [g[:, CHUNK - 1:CHUNK] if d == 0 else g[:, 0:1] for g, (jh, vh, d) in zip(g_r, chains)]
            dec = [jnp.where(incl[d], jnp.exp(jnp.where(incl[d], gc - gr, 0.0)), 0.0)
                   for gc, gr, (jh, vh, d) in zip(g_c, g_r, chains)]
            lms = [jnp.where(strict[d], kks[jh, d] * dc, 0.0) * bc
                   for dc, bc, (jh, vh, d) in zip(dec, beta_c, chains)]
            tbs = [t * br for t, br in zip(_unit_tri_inverses(lms, eye, level), beta_r)]
            us = [jnp.dot(tb.astype(BF16), v_ref[0, pl.ds(r0s[d], CHUNK), value_cols(jh, vh)],
                          preferred_element_type=F32) for tb, (jh, vh, d) in zip(tbs, chains)]
            ws = [jnp.dot((tb * jnp.exp(gr)).astype(BF16), ks[jh, d], preferred_element_type=F32)
                  for tb, gr, (jh, vh, d) in zip(tbs, g_r, chains)]
            for ci, (jh, vh, d) in enumerate(chains):
                u_scr[slot, ci] = us[ci]
                wq_scr[slot, ci, 0:CHUNK, :] = ws[ci].astype(BF16)
                kg = ks[jh, d].astype(F32) * jnp.exp(g_last[ci] - g_c[ci])
                ak_scr[slot, ci, CHUNK:CHUNK + HEAD_DIM, :] = kg.T.astype(BF16)
                dl_scr[slot, ci] = jnp.broadcast_to(jnp.exp(g_last[ci]), (1, HEAD_DIM))
                if write_out:
                    wq_scr[slot, ci, CHUNK:2 * CHUNK, :] = (qs[jh, d].astype(F32) * jnp.exp(g_c[ci])).astype(BF16)
                    ak_scr[slot, ci, 0:CHUNK, :] = (qks[jh, d] * dec[ci]).astype(BF16)

        def advance(i, slot):
            r0s = [chunk_of(i, d)[1] for d in range(2)]
            s_old = [s_scr[ci] for ci in range(N_CHAINS)]
            sbs = [s.astype(BF16) for s in s_old]
            rows = slice(0, 2 * CHUNK if write_out else CHUNK)
            r1 = [jnp.dot(wq_scr[slot, ci, rows, :], sbs[ci], preferred_element_type=F32) for ci in range(N_CHAINS)]
            vns = [(u_scr[slot, ci] - r1[ci][0:CHUNK]).astype(BF16) for ci in range(N_CHAINS)]
            rows = slice(0 if write_out else CHUNK, CHUNK + HEAD_DIM)
            r2 = [jnp.dot(ak_scr[slot, ci, rows, :], vns[ci], preferred_element_type=F32) for ci in range(N_CHAINS)]
            for ci, (jh, vh, d) in enumerate(chains):
                if write_out:
                    o_scr[pl.ds(r0s[d], CHUNK), value_cols(jh, vh)] += r1[ci][CHUNK:2 * CHUNK] + r2[ci][0:CHUNK]
                s_scr[ci] = s_old[ci] * dl_scr[slot, ci] + r2[ci][-HEAD_DIM:]

        prepare(0, 0)

        def pair_of_steps(m, carry):
            i = 2 * m
            advance(i, 0)
            prepare(i + 1, 1)
            advance(i + 1, 1)
            prepare(i + 2, 0)
            return carry

        lax.fori_loop(0, n_chunks // 2 - 1, pair_of_steps, 0)
        advance(n_chunks - 2, 0)
        prepare(n_chunks - 1, 1)
        advance(n_chunks - 1, 1)

    run(qc_ref, kc_ref, vc_ref, gcc_ref, grc_ref, False)
    run(qx_ref, kx_ref, vx_ref, gcx_ref, grx_ref, True)
    o_ref[0] = o_scr[...].astype(o_ref.dtype)


def _gate_rows(gcol):
    b, t, n = gcol.shape
    g = gcol.reshape(b, t // CHUNK, CHUNK, n // GATE_LANES, GATE_LANES)
    return g.transpose(0, 3, 1, 4, 2)


def _dn_chunk(qc, kc, vc, gcc, qx, kx, vx, gcx):
    b, s, _ = qx.shape
    l = qc.shape[1]
    assert l % (2 * CHUNK) == 0 and s % (2 * CHUNK) == 0, "the chunk loop advances two chunks per trip"
    grc, grx = _gate_rows(gcc), _gate_rows(gcx)
    qk_cols = HEADS_PER_CALL * HEAD_DIM
    v_cols = 2 * qk_cols

    def seq_specs(t):
        return [pl.BlockSpec((1, t, qk_cols), lambda bi, j: (bi, 0, j)),
                pl.BlockSpec((1, t, qk_cols), lambda bi, j: (bi, 0, j)),
                pl.BlockSpec((1, t, v_cols), lambda bi, j: (bi, 0, j)),
                pl.BlockSpec((1, t, LANES), lambda bi, j: (bi, 0, 0)),
                pl.BlockSpec((1, HEADS_PER_CALL, t // CHUNK, GATE_LANES, CHUNK), lambda bi, j: (bi, j, 0, 0, 0))]

    return pl.pallas_call(
        _chunk_body,
        grid=(b, DN_QK_HEADS // HEADS_PER_CALL),
        in_specs=seq_specs(l) + seq_specs(s),
        out_specs=pl.BlockSpec((1, s, v_cols), lambda bi, j: (bi, 0, j)),
        out_shape=jax.ShapeDtypeStruct((b, s, DN_V_HEADS * HEAD_DIM), BF16),
        scratch_shapes=[pltpu.VMEM((N_CHAINS, HEAD_DIM, HEAD_DIM), F32),
                        pltpu.VMEM((s, v_cols), F32),
                        pltpu.VMEM((2, N_CHAINS, CHUNK, HEAD_DIM), F32),
                        pltpu.VMEM((2, N_CHAINS, 2 * CHUNK, HEAD_DIM), BF16),
                        pltpu.VMEM((2, N_CHAINS, CHUNK + HEAD_DIM, CHUNK), BF16),
                        pltpu.VMEM((2, N_CHAINS, 1, HEAD_DIM), F32)],
        compiler_params=_cparams("parallel", "parallel"),
        name="dn_chunk",
    )(qc, kc, vc, gcc, grc, qx, kx, vx, gcx, grx)


def _rot_cols(w):
    q = MLA_ROPE_DIM // 4
    return jnp.concatenate([-w[..., q:2 * q], w[..., 0:q], -w[..., 3 * q:4 * q], w[..., 2 * q:3 * q]], axis=-1)


def _rope_cos_sin(t_len):
    q = MLA_ROPE_DIM // 4
    pos = np.arange(t_len)
    inv = ROPE_BASE ** (-np.arange(q, dtype=np.float32) / q)
    ang_r = (pos // GRID_W).astype(np.float32)[:, None] * inv[None, :]
    ang_c = (pos % GRID_W).astype(np.float32)[:, None] * inv[None, :]
    ang = jnp.asarray(np.concatenate([ang_r, ang_r, ang_c, ang_c], axis=1))
    return jnp.cos(ang), jnp.sin(ang)


def _q_tables(cos, sin):
    t = cos.shape[0]
    one, zero = jnp.ones((t, HEAD_DIM), F32), jnp.zeros((t, MLA_ROPE_DIM), F32)
    c1 = jnp.concatenate([one, cos, zero], axis=1) * MLA_SCALE
    c2 = jnp.concatenate([jnp.zeros((t, HEAD_DIM), F32), sin, zero], axis=1) * MLA_SCALE
    return c1, c2


def _k_tables(cos, sin):
    zero = jnp.zeros_like(cos)
    return jnp.concatenate([cos, zero], axis=1), jnp.concatenate([sin, zero], axis=1)


def _mod_params(c, c_ctx, w_mod, b_mod):
    b, d = c.shape
    rows = ((b + 1 + 7) // 8) * 8
    cond = jnp.concatenate([c, c_ctx[None], jnp.zeros((rows - b - 1, d), F32)], axis=0)
    mod = _adaln(cond, w_mod, b_mod)
    sh, sc, gt = (mod[:b, i * d:(i + 1) * d].reshape(b, 1, d) for i in range(3))
    sh_c, sc_c, gt_c = (jnp.broadcast_to(mod[b, i * d:(i + 1) * d].reshape(1, 1, d), (b, 1, d)) for i in range(3))
    return (sh, sc, gt), (sh_c, sc_c, gt_c)


def _na_mla_layer(x, ctx, c, c_ctx, w_mod, b_mod, norm, w_in, rpb, q_norm, w_qb, kv_norm, w_kvb, w_out):
    b, s, d = x.shape
    l = ctx.shape[1]
    (sh, sc, gt), (sh_c, sc_c, gt_c) = _mod_params(c, c_ctx, w_mod, b_mod)
    na_w = NA_HEADS * HEAD_DIM
    n_main = 3 * na_w + 2 * MLA_RANK
    kr_w = w_in[:, n_main:n_main + MLA_ROPE_DIM]
    w_main = jnp.concatenate([w_in[:, :n_main], w_in[:, n_main + MLA_ROPE_DIM:]], axis=1).astype(BF16)
    w_small = jnp.concatenate([kr_w, _rot_cols(kr_w)], axis=1).astype(BF16)
    cos, sin = _rope_cos_sin(s)
    ident = (jnp.ones((l, MLA_ROPE_DIM), F32), jnp.zeros((l, MLA_ROPE_DIM), F32))
    hx, krx = _norm_proj(x, 0, d, norm, w_main, tm=IN_PROJ_ROWS, tn=1024, shift=sh, scale=sc, w_small=w_small,
                         rope_small=_k_tables(cos, sin), small_dtype=BF16, name="in_proj_ab_x")
    ident_flat = tuple(jnp.tile(t, (b, 1)) for t in ident)
    hc, krc = _norm_proj(ctx.reshape(1, b * l, d), 0, d, norm, w_main, tm=IN_PROJ_ROWS, tn=1024, shift=sh_c[:1],
                         scale=sc_c[:1], w_small=w_small, rope_small=_k_tables(*ident_flat), small_dtype=BF16,
                         name="in_proj_ab_ctx")
    hc, krc = hc.reshape(b, l, -1), krc.reshape(b, l, -1)

    wq = w_qb.reshape(MLA_RANK, MLA_HEADS, HEAD_DIM + MLA_ROPE_DIM)
    wq_r = wq[..., HEAD_DIM:]
    wq_p = jnp.concatenate([wq[..., :HEAD_DIM], wq_r, _rot_cols(wq_r)], axis=-1)
    wq_p = wq_p.reshape(MLA_RANK, MLA_HEADS * MLA_QK_PAD).astype(BF16)
    wkv = w_kvb.reshape(MLA_RANK, MLA_HEADS, 2 * HEAD_DIM)
    wkv_p = jnp.concatenate([wkv[..., :HEAD_DIM].reshape(MLA_RANK, -1), wkv[..., HEAD_DIM:].reshape(MLA_RANK, -1)],
                            axis=1).astype(BF16)
    cq_blk = 3 * na_w // MLA_RANK
    nq = MLA_HEADS * MLA_QK_PAD
    qx = _norm_proj(hx, cq_blk, MLA_RANK, q_norm, wq_p, tm=1024, tn=nq, rope_main=_q_tables(cos, sin), name="mla_q_x")
    qc = _norm_proj(hc, cq_blk, MLA_RANK, q_norm, wq_p, tm=1024, tn=nq, rope_main=_q_tables(*ident), name="mla_q_ctx")
    kvx = _norm_proj(hx, cq_blk + 1, MLA_RANK, kv_norm, wkv_p, tm=1024, tn=nq, name="mla_kv_x")
    kvc = _norm_proj(hc, cq_blk + 1, MLA_RANK, kv_norm, wkv_p, tm=1024, tn=nq, name="mla_kv_ctx")

    o_na = _na_attention(hx, hc, _na_bias_table(rpb, s // GRID_W))
    seg_x = (kvx, 0, krx, kvx, MLA_HEADS)
    seg_c = (kvc, 0, krc, kvc, MLA_HEADS)
    o_mla = _attention(qx, 0, MLA_QK_PAD, [seg_x, seg_c], MLA_HEADS, tq=1024, scale=1.0, name="mla_attention")
    z_blk0 = n_main // na_w
    w_out_b = w_out.astype(BF16)
    x_new = _out_proj_ab(x, gt, o_na, o_mla, hx, z_blk0, w_out_b, tm=512)

    o_na_c = _attention(hc, 0, HEAD_DIM, [(hc, NA_HEADS, None, hc, 2 * NA_HEADS)], NA_HEADS, tq=l,
                        scale=HEAD_DIM ** -0.5, name="na_ctx_attention")
    o_mla_c = _attention(qc, 0, MLA_QK_PAD, [seg_c], MLA_HEADS, tq=l, scale=1.0, name="mla_ctx_attention")
    ctx_new = _out_proj_ab(ctx, gt_c, o_na_c, o_mla_c, hc, z_blk0, w_out_b, tm=512)
    return x_new, ctx_new


def _deltanet_final_layer(x, ctx, c, c_ctx, w_mod, b_mod, norm, w_in, conv_w, a_log, dt_bias, o_norm, w_out, final_norm):
    b, s, d = x.shape
    (sh, sc, gt), (sh_c, sc_c, _) = _mod_params(c, c_ctx, w_mod, b_mod)
    qk_w = DN_QK_HEADS * HEAD_DIM
    v_w = DN_V_HEADS * HEAD_DIM
    n_main = 2 * qk_w + 2 * v_w
    lane = np.arange(4 * DN_V_HEADS)
    perm = (lane % 4) * DN_V_HEADS + lane // 4
    w_main = w_in[:, :n_main].astype(BF16)
    w_small = w_in[:, n_main:][:, perm].astype(BF16)
    zeros = jnp.zeros((2 * DN_V_HEADS,), F32)
    alog = jnp.concatenate([zeros, a_log.reshape(-1).astype(F32)])[perm].reshape(1, -1)
    dtb = jnp.concatenate([zeros, dt_bias.reshape(-1).astype(F32)])[perm].reshape(1, -1)
    conv_w = conv_w.astype(F32)

    def inputs(t, shift, scale, name, shared_modulation=False):
        bt, tl, _ = t.shape
        if shared_modulation:
            t, shift, scale = t.reshape(1, bt * tl, d), shift[:1], scale[:1]
        h, hs = _norm_proj(t, 0, d, norm, w_main, tm=IN_PROJ_ROWS, tn=1024, shift=shift, scale=scale, w_small=w_small,
                           name=name)
        h, hs = h.reshape(bt, tl, -1), hs.reshape(bt, tl, -1)
        q = _dn_conv(h, conv_w, 0, qk_w, HEAD_DIM ** -0.5, tt=512)
        k = _dn_conv(h, conv_w, qk_w, qk_w, 1.0, tt=512)
        v = _dn_conv(h, conv_w, 2 * qk_w, v_w, None, tt=512)
        return h, q, k, v, _dn_gates(hs, alog, dtb)

    hx, qx, kx, vx, gx = inputs(x, sh, sc, "in_proj_dn_x")
    _, qc, kc, vc, gc = inputs(ctx, sh_c, sc_c, "in_proj_dn_ctx", shared_modulation=True)
    o = _dn_chunk(qc, kc, vc, gc, qx, kx, vx, gx)
    return _out_proj_dn(x, gt, o, hx, (2 * qk_w + v_w) // v_w, o_norm, w_out.astype(BF16), final_norm, tm=256)


def kernel(x, c, ctx, c_ctx, ab_w_mod, ab_b_mod, ab_norm, ab_w_in, ab_rpb, ab_q_norm, ab_w_qb, ab_kv_norm, ab_w_kvb, ab_w_out, dn_w_mod, dn_b_mod, dn_norm, dn_w_in, dn_conv, dn_a_log, dn_dt_bias, dn_o_norm, dn_w_out, final_norm):
    assert ab_w_mod.shape[0] == 1 and dn_w_mod.shape[0] == 1, "depth-2 trunk: one even and one odd layer"
    x, ctx = _na_mla_layer(x, ctx, c, c_ctx, ab_w_mod[0], ab_b_mod[0], ab_norm[0], ab_w_in[0], ab_rpb[0],
                           ab_q_norm[0], ab_w_qb[0], ab_kv_norm[0], ab_w_kvb[0], ab_w_out[0])
    return _deltanet_final_layer(x, ctx, c, c_ctx, dn_w_mod[0], dn_b_mod[0], dn_norm[0], dn_w_in[0], dn_conv[0],
                                 dn_a_log[0], dn_dt_bias[0], dn_o_norm[0], dn_w_out[0], final_norm)
```

```python
import functools

import numpy as np
import jax
import jax.numpy as jnp
from jax import lax
from jax.experimental import pallas as pl
from jax.experimental.pallas import tpu as pltpu

F32 = jnp.float32
BF16 = jnp.bfloat16

EPS = 1e-6
GRID_W = 64
WIN_R = 8
WIN_C = 16
NA_HEADS = 8
HEAD_DIM = 128
MLA_HEADS = 8
MLA_RANK = 512
MLA_ROPE_DIM = 64
MLA_SCALE = (HEAD_DIM + MLA_ROPE_DIM) ** -0.5
ROPE_BASE = 10000.0
DN_QK_HEADS = 16
DN_V_HEADS = 32
CONV_K = 5
CHUNK = 128
NEG_BIG = -1e30

LANES = 128
MLA_QK_PAD = 256
NA_Q_ROWS = 8
NA_KEY_ROWS = 16
ATTN_SUB_ROWS = 256
IN_PROJ_ROWS = 1024
VMEM_LIMIT = 56 * 1024 * 1024


def _cparams(*sem):
    return pltpu.CompilerParams(dimension_semantics=sem, vmem_limit_bytes=VMEM_LIMIT)


def _adaln_body(c_ref, w_ref, b_ref, o_ref):
    cf = c_ref[...]
    a = (cf * jax.nn.sigmoid(cf)).astype(BF16)
    o_ref[...] = jnp.dot(a, w_ref[...].astype(BF16), preferred_element_type=F32) + b_ref[...]


def _adaln(cond, w_mod, b_mod):
    r, d = cond.shape
    n = w_mod.shape[1]
    tn = 512
    return pl.pallas_call(
        _adaln_body,
        grid=(n // tn,),
        in_specs=[pl.BlockSpec((r, d), lambda j: (0, 0)),
                  pl.BlockSpec((d, tn), lambda j: (0, j)),
                  pl.BlockSpec((1, tn), lambda j: (0, j))],
        out_specs=pl.BlockSpec((r, tn), lambda j: (0, j)),
        out_shape=jax.ShapeDtypeStruct((r, n), F32),
        compiler_params=_cparams("parallel"),
        name="adaln",
    )(cond, w_mod, b_mod.reshape(1, n))


def _proj_body(*refs, modulate, has_small, rope_main, rope_small):
    it = iter(refs)
    x_ref, gain_ref = next(it), next(it)
    shift_ref = scale_ref = ws_ref = c1_ref = c2_ref = s1_ref = s2_ref = os_ref = None
    if modulate:
        shift_ref, scale_ref = next(it), next(it)
    w_ref = next(it)
    if has_small:
        ws_ref = next(it)
    if rope_main:
        c1_ref, c2_ref = next(it), next(it)
    if rope_small:
        s1_ref, s2_ref = next(it), next(it)
    o_ref = next(it)
    if has_small:
        os_ref = next(it)
    xm_ref = next(it)

    @pl.when(pl.program_id(2) == 0)
    def _():
        xf = x_ref[0].astype(F32)
        ms = jnp.mean(xf * xf, axis=-1, keepdims=True)
        y = xf * lax.rsqrt(ms + EPS) * gain_ref[...]
        if modulate:
            y = y * (1.0 + scale_ref[0]) + shift_ref[0]
        xm = y.astype(BF16)
        xm_ref[...] = xm
        if has_small:
            s = jnp.dot(xm, ws_ref[...], preferred_element_type=F32)
            if rope_small:
                s = s * s1_ref[...] + pltpu.roll(s, LANES // 2, axis=1) * s2_ref[...]
            os_ref[0] = s.astype(os_ref.dtype)

    acc = jnp.dot(xm_ref[...], w_ref[...], preferred_element_type=F32)
    if rope_main:
        for h in range(acc.shape[1] // MLA_QK_PAD):
            a = acc[:, h * MLA_QK_PAD:(h + 1) * MLA_QK_PAD]
            a = a * c1_ref[...] + pltpu.roll(a, MLA_QK_PAD - MLA_ROPE_DIM, axis=1) * c2_ref[...]
            o_ref[0, :, h * MLA_QK_PAD:(h + 1) * MLA_QK_PAD] = a.astype(o_ref.dtype)
    else:
        o_ref[0] = acc.astype(o_ref.dtype)


def _norm_proj(x, kblk, k, gain, w, *, tm, tn, shift=None, scale=None, w_small=None,
               rope_main=None, rope_small=None, small_dtype=F32, name="proj"):
    b, t, _ = x.shape
    n = w.shape[1]
    modulate = shift is not None
    has_small = w_small is not None
    tm = min(tm, t)
    args = [x, gain.reshape(1, k).astype(F32)]
    in_specs = [pl.BlockSpec((1, tm, k), lambda bi, i, j: (bi, i, kblk)),
                pl.BlockSpec((1, k), lambda bi, i, j: (0, 0))]
    if modulate:
        args += [shift, scale]
        in_specs += [pl.BlockSpec((1, 1, k), lambda bi, i, j: (bi, 0, 0))] * 2
    args.append(w)
    in_specs.append(pl.BlockSpec((k, tn), lambda bi, i, j: (0, j)))
    if has_small:
        ns = w_small.shape[1]
        args.append(w_small)
        in_specs.append(pl.BlockSpec((k, ns), lambda bi, i, j: (0, 0)))
    if rope_main is not None:
        args += list(rope_main)
        in_specs += [pl.BlockSpec((tm, MLA_QK_PAD), lambda bi, i, j: (i, 0))] * 2
    if rope_small is not None:
        args += list(rope_small)
        in_specs += [pl.BlockSpec((tm, LANES), lambda bi, i, j: (i, 0))] * 2
    out_shape = [jax.ShapeDtypeStruct((b, t, n), BF16)]
    out_specs = [pl.BlockSpec((1, tm, tn), lambda bi, i, j: (bi, i, j))]
    if has_small:
        out_shape.append(jax.ShapeDtypeStruct((b, t, ns), small_dtype))
        out_specs.append(pl.BlockSpec((1, tm, ns), lambda bi, i, j: (bi, i, 0)))
    body = functools.partial(_proj_body, modulate=modulate, has_small=has_small,
                             rope_main=rope_main is not None, rope_small=rope_small is not None)
    out = pl.pallas_call(
        body,
        grid=(b, t // tm, n // tn),
        in_specs=in_specs,
        out_specs=out_specs,
        out_shape=out_shape,
        scratch_shapes=[pltpu.VMEM((tm, k), BF16)],
        compiler_params=_cparams("parallel", "parallel", "arbitrary"),
        name=name,
    )(*args)
    return out if has_small else out[0]


def _attn_body(*refs, seg_lens, has_extra, scale):
    it = iter(refs)
    q_ref = next(it)
    segs = []
    for _ in seg_lens:
        km = next(it)
        ke = next(it) if has_extra else None
        segs.append((km, ke, next(it)))
    o_ref, k_scr, v_scr = next(it), next(it), next(it)

    @pl.when(pl.program_id(2) == 0)
    def _():
        off = 0
        for (km, ke, v), ln in zip(segs, seg_lens):
            k_scr[off:off + ln, 0:HEAD_DIM] = km[0]
            if has_extra:
                k_scr[off:off + ln, HEAD_DIM:2 * HEAD_DIM] = ke[0]
            v_scr[off:off + ln, :] = v[0]
            off += ln

    sub = min(ATTN_SUB_ROWS, q_ref.shape[1])
    n_sub = q_ref.shape[1] // sub

    def scores(j):
        s = lax.dot_general(q_ref[0, j * sub:(j + 1) * sub, :], k_scr[...], (((1,), (1,)), ((), ())),
                            preferred_element_type=F32)
        return s if scale == 1.0 else s * scale

    s_next = scores(0)
    for j in range(n_sub):
        s = s_next
        if j + 1 < n_sub:
            s_next = scores(j + 1)
        m = jnp.max(s, axis=-1, keepdims=True)
        e = jnp.exp(s - m)
        l = jnp.sum(e, axis=-1, keepdims=True)
        o = jnp.dot(e.astype(BF16), v_scr[...], preferred_element_type=F32)
        o_ref[0, j * sub:(j + 1) * sub, :] = (o / l).astype(o_ref.dtype)


def _attention(q, q_blk0, dq, segs, heads, *, tq, scale, name):
    b, t, _ = q.shape
    tq = min(tq, t)
    has_extra = segs[0][2] is not None
    seg_lens = tuple(s[0].shape[1] for s in segs)
    tk = sum(seg_lens)
    args = [q]
    in_specs = [pl.BlockSpec((1, tq, dq), lambda bi, h, i: (bi, i, q_blk0 + h))]
    for km, kb0, ke, v, vb0 in segs:
        ln = km.shape[1]
        args.append(km)
        in_specs.append(pl.BlockSpec((1, ln, HEAD_DIM), lambda bi, h, i, kb0=kb0: (bi, 0, kb0 + h)))
        if has_extra:
            args.append(ke)
            in_specs.append(pl.BlockSpec((1, ln, HEAD_DIM), lambda bi, h, i: (bi, 0, 0)))
        args.append(v)
        in_specs.append(pl.BlockSpec((1, ln, HEAD_DIM), lambda bi, h, i, vb0=vb0: (bi, 0, vb0 + h)))
    body = functools.partial(_attn_body, seg_lens=seg_lens, has_extra=has_extra, scale=scale)
    return pl.pallas_call(
        body,
        grid=(b, heads, t // tq),
        in_specs=in_specs,
        out_specs=pl.BlockSpec((1, tq, HEAD_DIM), lambda bi, h, i: (bi, i, h)),
        out_shape=jax.ShapeDtypeStruct((b, t, heads * HEAD_DIM), BF16),
        scratch_shapes=[pltpu.VMEM((tk, dq), BF16), pltpu.VMEM((tk, HEAD_DIM), BF16)],
        compiler_params=_cparams("parallel", "parallel", "arbitrary"),
        name=name,
    )(*args)


def _na_geometry(rows):
    groups = rows // NA_Q_ROWS
    key_rows = min(NA_KEY_ROWS, rows)
    return groups, key_rows


def _na_bias_table(rpb, rows):
    heads = rpb.shape[0]
    groups, key_rows = _na_geometry(rows)
    g = np.arange(groups)
    slab0 = np.clip(NA_Q_ROWS * g - WIN_R // 2, 0, rows - key_rows)
    r = NA_Q_ROWS * g[:, None] + np.arange(NA_Q_ROWS)[None]
    k_abs = slab0[:, None] + np.arange(key_rows)[None]
    win0 = np.clip(r - WIN_R // 2, 0, rows - WIN_R)
    d_row = k_abs[:, None, :] - r[:, :, None] + (WIN_R - 1)
    ok_row = (k_abs[:, None, :] >= win0[:, :, None]) & (k_abs[:, None, :] < win0[:, :, None] + WIN_R)
    col = np.arange(GRID_W)
    c0 = np.clip(col - WIN_C // 2, 0, GRID_W - WIN_C)
    d_col = col[None, :] - col[:, None] + (WIN_C - 1)
    ok_col = (col[None, :] >= c0[:, None]) & (col[None, :] < c0[:, None] + WIN_C)
    sel_row = (np.clip(d_row, 0, 2 * WIN_R - 2)[..., None] == np.arange(2 * WIN_R - 1)).astype(np.float32)
    sel_col = (np.clip(d_col, 0, 2 * WIN_C - 2)[..., None] == np.arange(2 * WIN_C - 1)).astype(np.float32)
    ok = jnp.asarray(ok_row)[:, :, None, :, None] & jnp.asarray(ok_col)[None, None, :, None, :]
    t = jnp.einsum("grka,hab,qcb->hgrqkc", jnp.asarray(sel_row), rpb.astype(F32), jnp.asarray(sel_col),
                   precision=lax.Precision.HIGHEST)
    t = jnp.where(ok[None], t, NEG_BIG)
    return t.reshape(heads, groups, NA_Q_ROWS * GRID_W, key_rows * GRID_W)


def _na_body(q_ref, k_ref, v_ref, kc_ref, vc_ref, bias_ref, o_ref, *, rows, scale):
    groups, key_rows = _na_geometry(rows)
    g = pl.program_id(2)
    slab0 = jnp.clip(NA_Q_ROWS * g - WIN_R // 2, 0, rows - key_rows)
    start = pl.multiple_of(slab0 * GRID_W, 4 * GRID_W)
    nk = key_rows * GRID_W
    nt = (((1,), (1,)), ((), ()))
    sub = min(ATTN_SUB_ROWS, q_ref.shape[1])
    n_sub = q_ref.shape[1] // sub

    def scores(j):
        q = q_ref[0, j * sub:(j + 1) * sub, :]
        s_nb = lax.dot_general(q, k_ref[0, pl.ds(start, nk), :], nt, preferred_element_type=F32)
        s_cx = lax.dot_general(q, kc_ref[0], nt, preferred_element_type=F32)
        return s_nb, s_cx

    s_next = scores(0)
    for j in range(n_sub):
        s_nb, s_cx = s_next
        if j + 1 < n_sub:
            s_next = scores(j + 1)
        s_nb = s_nb * scale + bias_ref[0, g, j * sub:(j + 1) * sub, :]
        s_cx = s_cx * scale
        m = jnp.maximum(jnp.max(s_nb, axis=-1, keepdims=True), jnp.max(s_cx, axis=-1, keepdims=True))
        e_nb = jnp.exp(s_nb - m)
        e_cx = jnp.exp(s_cx - m)
        l = jnp.sum(e_nb, axis=-1, keepdims=True) + jnp.sum(e_cx, axis=-1, keepdims=True)
        o = jnp.dot(e_nb.astype(BF16), v_ref[0, pl.ds(start, nk), :], preferred_element_type=F32)
        o = o + jnp.dot(e_cx.astype(BF16), vc_ref[0], preferred_element_type=F32)
        o_ref[0, j * sub:(j + 1) * sub, :] = (o / l).astype(o_ref.dtype)


def _na_attention(hx, hc, bias):
    b, s, _ = hx.shape
    l = hc.shape[1]
    rows = s // GRID_W
    groups, key_rows = _na_geometry(rows)
    tq = NA_Q_ROWS * GRID_W
    body = functools.partial(_na_body, rows=rows, scale=HEAD_DIM ** -0.5)
    return pl.pallas_call(
        body,
        grid=(NA_HEADS, b, groups),
        in_specs=[pl.BlockSpec((1, tq, HEAD_DIM), lambda h, bi, g: (bi, g, h)),
                  pl.BlockSpec((1, s, HEAD_DIM), lambda h, bi, g: (bi, 0, NA_HEADS + h)),
                  pl.BlockSpec((1, s, HEAD_DIM), lambda h, bi, g: (bi, 0, 2 * NA_HEADS + h)),
                  pl.BlockSpec((1, l, HEAD_DIM), lambda h, bi, g: (bi, 0, NA_HEADS + h)),
                  pl.BlockSpec((1, l, HEAD_DIM), lambda h, bi, g: (bi, 0, 2 * NA_HEADS + h)),
                  pl.BlockSpec((1, groups, tq, key_rows * GRID_W), lambda h, bi, g: (h, 0, 0, 0))],
        out_specs=pl.BlockSpec((1, tq, HEAD_DIM), lambda h, bi, g: (bi, g, h)),
        out_shape=jax.ShapeDtypeStruct((b, s, NA_HEADS * HEAD_DIM), BF16),
        compiler_params=_cparams("parallel", "parallel", "parallel"),
        name="na_attention",
    )(hx, hx, hx, hc, hc, bias)


def _silu(x):
    return x * jax.nn.sigmoid(x)


def _out_ab_body(x_ref, gt_ref, o1_ref, o2_ref, z1_ref, z2_ref, w_ref, y_ref):
    half = o1_ref.shape[2]
    a1 = (o1_ref[0].astype(F32) * _silu(z1_ref[0].astype(F32))).astype(BF16)
    acc = jnp.dot(a1, w_ref[0:half, :], preferred_element_type=F32)
    a2 = (o2_ref[0].astype(F32) * _silu(z2_ref[0].astype(F32))).astype(BF16)
    acc = acc + jnp.dot(a2, w_ref[half:2 * half, :], preferred_element_type=F32)
    y_ref[0] = x_ref[0] + gt_ref[0] * acc


def _out_proj_ab(x, gt, o1, o2, h, z_blk0, w, *, tm):
    b, t, d = x.shape
    half = o1.shape[2]
    tm = min(tm, t)
    return pl.pallas_call(
        _out_ab_body,
        grid=(b, t // tm),
        in_specs=[pl.BlockSpec((1, tm, d), lambda bi, i: (bi, i, 0)),
                  pl.BlockSpec((1, 1, d), lambda bi, i: (bi, 0, 0)),
                  pl.BlockSpec((1, tm, half), lambda bi, i: (bi, i, 0)),
                  pl.BlockSpec((1, tm, half), lambda bi, i: (bi, i, 0)),
                  pl.BlockSpec((1, tm, half), lambda bi, i: (bi, i, z_blk0)),
                  pl.BlockSpec((1, tm, half), lambda bi, i: (bi, i, z_blk0 + 1)),
                  pl.BlockSpec((2 * half, d), lambda bi, i: (0, 0))],
        out_specs=pl.BlockSpec((1, tm, d), lambda bi, i: (bi, i, 0)),
        out_shape=jax.ShapeDtypeStruct((b, t, d), F32),
        compiler_params=_cparams("parallel", "parallel"),
        name="out_proj_ab",
    )(x, gt, o1, o2, h, h, w)


def _out_dn_body(x_ref, gt_ref, o_ref, z_ref, on_ref, w_ref, fn_ref, y_ref, a_scr):
    heads = o_ref.shape[2] // HEAD_DIM
    parts = 2
    acc = None
    for p in range(parts):
        for h in range(p * heads // parts, (p + 1) * heads // parts):
            sl = slice(h * HEAD_DIM, (h + 1) * HEAD_DIM)
            oh = o_ref[0, :, sl].astype(F32)
            ms = jnp.mean(oh * oh, axis=-1, keepdims=True)
            a = oh * lax.rsqrt(ms + EPS) * on_ref[...] * _silu(z_ref[0, :, sl].astype(F32))
            a_scr[:, sl] = a.astype(BF16)
        cols = slice(p * heads // parts * HEAD_DIM, (p + 1) * heads // parts * HEAD_DIM)
        part = jnp.dot(a_scr[:, cols], w_ref[cols, :], preferred_element_type=F32)
        acc = part if acc is None else acc + part
    y = x_ref[0] + gt_ref[0] * acc
    ms = jnp.mean(y * y, axis=-1, keepdims=True)
    y_ref[0] = y * lax.rsqrt(ms + EPS) * fn_ref[...]


def _out_proj_dn(x, gt, o, h, z_blk, o_norm, w, final_norm, *, tm):
    b, t, d = x.shape
    km = o.shape[2]
    tm = min(tm, t)
    return pl.pallas_call(
        _out_dn_body,
        grid=(b, t // tm),
        in_specs=[pl.BlockSpec((1, tm, d), lambda bi, i: (bi, i, 0)),
                  pl.BlockSpec((1, 1, d), lambda bi, i: (bi, 0, 0)),
                  pl.BlockSpec((1, tm, km), lambda bi, i: (bi, i, 0)),
                  pl.BlockSpec((1, tm, km), lambda bi, i: (bi, i, z_blk)),
                  pl.BlockSpec((1, HEAD_DIM), lambda bi, i: (0, 0)),
                  pl.BlockSpec((km, d), lambda bi, i: (0, 0)),
                  pl.BlockSpec((1, d), lambda bi, i: (0, 0))],
        out_specs=pl.BlockSpec((1, tm, d), lambda bi, i: (bi, i, 0)),
        out_shape=jax.ShapeDtypeStruct((b, t, d), F32),
        scratch_shapes=[pltpu.VMEM((tm, km), BF16)],
        compiler_params=_cparams("parallel", "parallel"),
        name="out_proj_dn",
    )(x, gt, o, h, o_norm.reshape(1, HEAD_DIM).astype(F32), w, final_norm.reshape(1, d).astype(F32))


CONV_HALO = 16


SUBLANES = 8
SEG_PAD = 8


def _conv_body(xp_ref, x_ref, xn_ref, w_ref, o_ref, in_scr, out_scr, *, l2_scale):
    i = pl.program_id(1)
    nt = pl.num_programs(1)
    tt = x_ref.shape[1]
    seg = tt // SUBLANES
    pitch = seg + SEG_PAD
    half = CONV_K // 2
    prev = jnp.where(i > 0, xp_ref[0].astype(F32), 0.0)
    nxt = jnp.where(i < nt - 1, xn_ref[0].astype(F32), 0.0)
    for sl in range(x_ref.shape[2] // LANES):
        lanes = slice(sl * LANES, (sl + 1) * LANES)
        in_scr[sl, seg - SUBLANES:seg, :] = prev[CONV_HALO - SUBLANES:, lanes]
        for s in range(SUBLANES):
            in_scr[sl, (s + 1) * pitch:(s + 1) * pitch + seg, :] = x_ref[0, s * seg:(s + 1) * seg, lanes].astype(F32)
        in_scr[sl, (SUBLANES + 1) * pitch:(SUBLANES + 1) * pitch + SUBLANES, :] = nxt[:SUBLANES, lanes]
        w = [w_ref[j:j + 1, lanes] for j in range(CONV_K)]

        def strided(n):
            start = pitch + n - (SEG_PAD if n < 0 else 0) + (SEG_PAD if n >= seg else 0)
            return in_scr[sl, pl.ds(start, SUBLANES, stride=pitch), :]

        window = [strided(n) for n in range(-half, half)]
        for m in range(seg):
            window.append(strided(m + half))
            acc = window[0] * w[0]
            for j in range(1, CONV_K):
                acc = acc + window[j] * w[j]
            window.pop(0)
            y = _silu(acc)
            if l2_scale is not None:
                y = y * (lax.rsqrt(jnp.sum(y * y, axis=-1, keepdims=True) + EPS) * l2_scale)
            out_scr[sl, pl.ds(m, SUBLANES, stride=pitch), :] = y
        for s in range(SUBLANES):
            o_ref[0, s * seg:(s + 1) * seg, lanes] = out_scr[sl, s * pitch:s * pitch + seg, :].astype(o_ref.dtype)


def _dn_conv(h, conv_w, col0, width, l2_scale, *, tt, cb=1024):
    b, t, _ = h.shape
    tt = min(tt, t)
    blk0 = col0 // cb
    hb = tt // CONV_HALO
    nhalo = t // CONV_HALO
    pitch = tt // SUBLANES + SEG_PAD
    body = functools.partial(_conv_body, l2_scale=l2_scale)
    return pl.pallas_call(
        body,
        grid=(b, t // tt, width // cb),
        in_specs=[pl.BlockSpec((1, CONV_HALO, cb), lambda bi, i, c: (bi, jnp.maximum(i * hb - 1, 0), blk0 + c)),
                  pl.BlockSpec((1, tt, cb), lambda bi, i, c: (bi, i, blk0 + c)),
                  pl.BlockSpec((1, CONV_HALO, cb), lambda bi, i, c: (bi, jnp.minimum((i + 1) * hb, nhalo - 1), blk0 + c)),
                  pl.BlockSpec((CONV_K, cb), lambda bi, i, c: (0, blk0 + c))],
        out_specs=pl.BlockSpec((1, tt, cb), lambda bi, i, c: (bi, i, c)),
        out_shape=jax.ShapeDtypeStruct((b, t, width), BF16),
        scratch_shapes=[pltpu.VMEM((cb // LANES, (SUBLANES + 2) * pitch, LANES), F32),
                        pltpu.VMEM((cb // LANES, SUBLANES * pitch, LANES), F32)],
        compiler_params=_cparams("parallel", "parallel", "parallel"),
        name="dn_conv",
    )(h, h, h, conv_w)


GATE_TILE = 256


def _split3(x):
    hi = x.astype(BF16)
    r1 = x - hi.astype(F32)
    mid = r1.astype(BF16)
    lo = (r1 - mid.astype(F32)).astype(BF16)
    return hi, mid, lo


def _gates_body(h_ref, alog_ref, dtb_ref, o_ref):
    t = h_ref[0]
    lane = lax.broadcasted_iota(jnp.int32, t.shape, 1) % 4
    beta = jax.nn.sigmoid(t)
    z = t + dtb_ref[...]
    softplus = jnp.maximum(z, 0.0) + jnp.log1p(jnp.exp(-jnp.abs(z)))
    g = -jnp.exp(alog_ref[...]) * softplus
    n = t.shape[0]
    r = lax.broadcasted_iota(jnp.int32, (n, n), 0)
    c = lax.broadcasted_iota(jnp.int32, (n, n), 1)
    same = (r // CHUNK) == (c // CHUNK)
    tri_f = jnp.where(same & (r >= c), 1.0, 0.0).astype(BF16)
    tri_b = jnp.where(same & (r <= c), 1.0, 0.0).astype(BF16)
    pf = pb = None
    for part in _split3(g):
        df = jnp.dot(tri_f, part, preferred_element_type=F32)
        db = jnp.dot(tri_b, part, preferred_element_type=F32)
        pf = df if pf is None else pf + df
        pb = db if pb is None else pb + db
    o_ref[0] = jnp.where(lane < 2, beta, jnp.where(lane == 2, pf, pb))


def _dn_gates(h_small, alog, dtb):
    b, t, n = h_small.shape
    tt = min(GATE_TILE, t)
    return pl.pallas_call(
        _gates_body,
        grid=(b, t // tt),
        in_specs=[pl.BlockSpec((1, tt, n), lambda bi, i: (bi, i, 0)),
                  pl.BlockSpec((1, n), lambda bi, i: (0, 0)),
                  pl.BlockSpec((1, n), lambda bi, i: (0, 0))],
        out_specs=pl.BlockSpec((1, tt, n), lambda bi, i: (bi, i, 0)),
        out_shape=jax.ShapeDtypeStruct((b, t, n), F32),
        compiler_params=_cparams("parallel", "parallel"),
        name="dn_gates",
    )(h_small, alog, dtb)


HEADS_PER_CALL = 4
N_CHAINS = 4 * HEADS_PER_CALL
GATE_LANES = 8
MERGE_LEVELS = tuple(4 << i for i in range((CHUNK // 4).bit_length() - 1))


def _bdot(a, b):
    return jnp.dot(a.astype(BF16), b.astype(BF16), preferred_element_type=F32)


def _unit_tri_inverses(lms, eye, level):
    l4s = [jnp.where(level == 0, lm, 0.0) for lm in lms]
    ts = [eye - l4 for l4 in l4s]
    sq = [_bdot(l4, l4) for l4 in l4s]
    ts = [t + _bdot(t, p) for t, p in zip(ts, sq)]
    for m in range(1, len(MERGE_LEVELS) + 1):
        cs = [jnp.where(level == m, lm, 0.0).astype(BF16) for lm in lms]
        tbs = [t.astype(BF16) for t in ts]
        xs = [jnp.dot(tb, c, preferred_element_type=F32) for tb, c in zip(tbs, cs)]
        ts = [t - _bdot(x, tb) for t, x, tb in zip(ts, xs, tbs)]
    return ts


def _chunk_body(qc_ref, kc_ref, vc_ref, gcc_ref, grc_ref, qx_ref, kx_ref, vx_ref, gcx_ref, grx_ref,
                o_ref, s_scr, o_scr, u_scr, wq_scr, ak_scr, dl_scr):
    jg = pl.program_id(1)
    group_lanes = GATE_LANES * HEADS_PER_CALL
    lane_shift = (LANES - group_lanes * jg) % LANES
    s_scr[...] = jnp.zeros_like(s_scr)
    o_scr[...] = jnp.zeros_like(o_scr)
    row = lax.broadcasted_iota(jnp.int32, (CHUNK, CHUNK), 0)
    col = lax.broadcasted_iota(jnp.int32, (CHUNK, CHUNK), 1)
    eye = jnp.where(row == col, 1.0, 0.0)
    level = jnp.zeros((CHUNK, CHUNK), jnp.int32)
    for size in MERGE_LEVELS:
        level = level + jnp.where((row // size) == (col // size), 0, 1)
    incl = (row >= col, row <= col)
    strict = (row > col, row < col)
    nt = (((1,), (1,)), ((), ()))
    chains = [(jh, vh, d) for d in range(2) for jh in range(HEADS_PER_CALL) for vh in range(2)]
    streams = [(jh, d) for d in range(2) for jh in range(HEADS_PER_CALL)]

    def head_cols(jh):
        return slice(jh * HEAD_DIM, (jh + 1) * HEAD_DIM)

    def value_cols(jh, vh):
        return slice((2 * jh + vh) * HEAD_DIM, (2 * jh + vh + 1) * HEAD_DIM)

    def run(q_ref, k_ref, v_ref, gc_ref, gr_ref, write_out):
        n_chunks = q_ref.shape[1] // CHUNK

        def chunk_of(i, d):
            n = i if d == 0 else n_chunks - 1 - i
            return n, (n * CHUNK if isinstance(n, int) else pl.multiple_of(n * CHUNK, CHUNK))

        def prepare(i, slot):
            ns, r0s, gcols = {}, {}, {}
            for d in range(2):
                ns[d], r0s[d] = chunk_of(i, d)
                gcols[d] = pltpu.roll(gc_ref[0, pl.ds(r0s[d], CHUNK), :], lane_shift, axis=1)
            ks = {(jh, d): k_ref[0, pl.ds(r0s[d], CHUNK), head_cols(jh)] for jh, d in streams}
            kks = {sd: lax.dot_general(ks[sd], ks[sd], nt, preferred_element_type=F32) for sd in streams}
            if write_out:
                qs = {(jh, d): q_ref[0, pl.ds(r0s[d], CHUNK), head_cols(jh)] for jh, d in streams}
                qks = {sd: lax.dot_general(qs[sd], ks[sd], nt, preferred_element_type=F32) for sd in streams}
            grows = {(jh, d): gr_ref[0, jh, ns[d]] for jh, d in streams}
            lane0 = [GATE_LANES * jh + 4 * vh + d for jh, vh, d in chains]
            beta_c = [gcols[d][:, l:l + 1] for l, (jh, vh, d) in zip(lane0, chains)]
            g_c = [gcols[d][:, l + 2:l + 3] for l, (jh, vh, d) in zip(lane0, chains)]
            beta_r = [grows[jh, d][4 * vh + d:4 * vh + d + 1, :] for jh, vh, d in chains]
            g_r = [grows[jh, d][4 * vh + 2 + d:4 * vh + 3 + d, :] for jh, vh, d in chains]
            g_last = [g[:, CHUNK - 1:CHUNK] if d == 0 else g[:, 0:1] for g, (jh, vh, d) in zip(g_r, chains)]
            dec = [jnp.where(incl[d], jnp.exp(jnp.where(incl[d], gc - gr, 0.0)), 0.0)
                   for gc, gr, (jh, vh, d) in zip(g_c, g_r, chains)]
            lms = [jnp.where(strict[d], kks[jh, d] * dc, 0.0) * bc
                   for dc, bc, (jh, vh, d) in zip(dec, beta_c, chains)]
            tbs = [t * br for t, br in zip(_unit_tri_inverses(lms, eye, level), beta_r)]
            us = [jnp.dot(tb.astype(BF16), v_ref[0, pl.ds(r0s[d], CHUNK), value_cols(jh, vh)],
                          preferred_element_type=F32) for tb, (jh, vh, d) in zip(tbs, chains)]
            ws = [jnp.dot((tb * jnp.exp(gr)).astype(BF16), ks[jh, d], preferred_element_type=F32)
                  for tb, gr, (jh, vh, d) in zip(tbs, g_r, chains)]
            for ci, (jh, vh, d) in enumerate(chains):
                u_scr[slot, ci] = us[ci]
                wq_scr[slot, ci, 0:CHUNK, :] = ws[ci].astype(BF16)
                kg = ks[jh, d].astype(F32) * jnp.exp(g_last[ci] - g_c[ci])
                ak_scr[slot, ci, CHUNK:CHUNK + HEAD_DIM, :] = kg.T.astype(BF16)
                dl_scr[slot, ci] = jnp.broadcast_to(jnp.exp(g_last[ci]), (1, HEAD_DIM))
                if write_out:
                    wq_scr[slot, ci, CHUNK:2 * CHUNK, :] = (qs[jh, d].astype(F32) * jnp.exp(g_c[ci])).astype(BF16)
                    ak_scr[slot, ci, 0:CHUNK, :] = (qks[jh, d] * dec[ci]).astype(BF16)

        def advance(i, slot):
            r0s = [chunk_of(i, d)[1] for d in range(2)]
            s_old = [s_scr[ci] for ci in range(N_CHAINS)]
            sbs = [s.astype(BF16) for s in s_old]
            rows = slice(0, 2 * CHUNK if write_out else CHUNK)
            r1 = [jnp.dot(wq_scr[slot, ci, rows, :], sbs[ci], preferred_element_type=F32) for ci in range(N_CHAINS)]
            vns = [(u_scr[slot, ci] - r1[ci][0:CHUNK]).astype(BF16) for ci in range(N_CHAINS)]
            rows = slice(0 if write_out else CHUNK, CHUNK + HEAD_DIM)
            r2 = [jnp.dot(ak_scr[slot, ci, rows, :], vns[ci], preferred_element_type=F32) for ci in range(N_CHAINS)]
            for ci, (jh, vh, d) in enumerate(chains):
                if write_out:
                    o_scr[pl.ds(r0s[d], CHUNK), value_cols(jh, vh)] += r1[ci][CHUNK:2 * CHUNK] + r2[ci][0:CHUNK]
                s_scr[ci] = s_old[ci] * dl_scr[slot, ci] + r2[ci][-HEAD_DIM:]

        prepare(0, 0)

        def pair_of_steps(m, carry):
            i = 2 * m
            advance(i, 0)
            prepare(i + 1, 1)
            advance(i + 1, 1)
            prepare(i + 2, 0)
            return carry

        lax.fori_loop(0, n_chunks // 2 - 1, pair_of_steps, 0)
        advance(n_chunks - 2, 0)
        prepare(n_chunks - 1, 1)
        advance(n_chunks - 1, 1)

    run(qc_ref, kc_ref, vc_ref, gcc_ref, grc_ref, False)
    run(qx_ref, kx_ref, vx_ref, gcx_ref, grx_ref, True)
    o_ref[0] = o_scr[...].astype(o_ref.dtype)


def _gate_rows(gcol):
    b, t, n = gcol.shape
    g = gcol.reshape(b, t // CHUNK, CHUNK, n // GATE_LANES, GATE_LANES)
    return g.transpose(0, 3, 1, 4, 2)


def _dn_chunk(qc, kc, vc, gcc, qx, kx, vx, gcx):
    b, s, _ = qx.shape
    l = qc.shape[1]
    assert l % (2 * CHUNK) == 0 and s % (2 * CHUNK) == 0, "the chunk loop advances two chunks per trip"
    grc, grx = _gate_rows(gcc), _gate_rows(gcx)
    qk_cols = HEADS_PER_CALL * HEAD_DIM
    v_cols = 2 * qk_cols

    def seq_specs(t):
        return [pl.BlockSpec((1, t, qk_cols), lambda bi, j: (bi, 0, j)),
                pl.BlockSpec((1, t, qk_cols), lambda bi, j: (bi, 0, j)),
                pl.BlockSpec((1, t, v_cols), lambda bi, j: (bi, 0, j)),
                pl.BlockSpec((1, t, LANES), lambda bi, j: (bi, 0, 0)),
                pl.BlockSpec((1, HEADS_PER_CALL, t // CHUNK, GATE_LANES, CHUNK), lambda bi, j: (bi, j, 0, 0, 0))]

    return pl.pallas_call(
        _chunk_body,
        grid=(b, DN_QK_HEADS // HEADS_PER_CALL),
        in_specs=seq_specs(l) + seq_specs(s),
        out_specs=pl.BlockSpec((1, s, v_cols), lambda bi, j: (bi, 0, j)),
        out_shape=jax.ShapeDtypeStruct((b, s, DN_V_HEADS * HEAD_DIM), BF16),
        scratch_shapes=[pltpu.VMEM((N_CHAINS, HEAD_DIM, HEAD_DIM), F32),
                        pltpu.VMEM((s, v_cols), F32),
                        pltpu.VMEM((2, N_CHAINS, CHUNK, HEAD_DIM), F32),
                        pltpu.VMEM((2, N_CHAINS, 2 * CHUNK, HEAD_DIM), BF16),
                        pltpu.VMEM((2, N_CHAINS, CHUNK + HEAD_DIM, CHUNK), BF16),
                        pltpu.VMEM((2, N_CHAINS, 1, HEAD_DIM), F32)],
        compiler_params=_cparams("parallel", "parallel"),
        name="dn_chunk",
    )(qc, kc, vc, gcc, grc, qx, kx, vx, gcx, grx)


def _rot_cols(w):
    q = MLA_ROPE_DIM // 4
    return jnp.concatenate([-w[..., q:2 * q], w[..., 0:q], -w[..., 3 * q:4 * q], w[..., 2 * q:3 * q]], axis=-1)


def _rope_cos_sin(t_len):
    q = MLA_ROPE_DIM // 4
    pos = np.arange(t_len)
    inv = ROPE_BASE ** (-np.arange(q, dtype=np.float32) / q)
    ang_r = (pos // GRID_W).astype(np.float32)[:, None] * inv[None, :]
    ang_c = (pos % GRID_W).astype(np.float32)[:, None] * inv[None, :]
    ang = jnp.asarray(np.concatenate([ang_r, ang_r, ang_c, ang_c], axis=1))
    return jnp.cos(ang), jnp.sin(ang)


def _q_tables(cos, sin):
    t = cos.shape[0]
    one, zero = jnp.ones((t, HEAD_DIM), F32), jnp.zeros((t, MLA_ROPE_DIM), F32)
    c1 = jnp.concatenate([one, cos, zero], axis=1) * MLA_SCALE
    c2 = jnp.concatenate([jnp.zeros((t, HEAD_DIM), F32), sin, zero], axis=1) * MLA_SCALE
    return c1, c2


def _k_tables(cos, sin):
    zero = jnp.zeros_like(cos)
    return jnp.concatenate([cos, zero], axis=1), jnp.concatenate([sin, zero], axis=1)


def _mod_params(c, c_ctx, w_mod, b_mod):
    b, d = c.shape
    rows = ((b + 1 + 7) // 8) * 8
    cond = jnp.concatenate([c, c_ctx[None], jnp.zeros((rows - b - 1, d), F32)], axis=0)
    mod = _adaln(cond, w_mod, b_mod)
    sh, sc, gt = (mod[:b, i * d:(i + 1) * d].reshape(b, 1, d) for i in range(3))
    sh_c, sc_c, gt_c = (jnp.broadcast_to(mod[b, i * d:(i + 1) * d].reshape(1, 1, d), (b, 1, d)) for i in range(3))
    return (sh, sc, gt), (sh_c, sc_c, gt_c)


def _na_mla_layer(x, ctx, c, c_ctx, w_mod, b_mod, norm, w_in, rpb, q_norm, w_qb, kv_norm, w_kvb, w_out):
    b, s, d = x.shape
    l = ctx.shape[1]
    (sh, sc, gt), (sh_c, sc_c, gt_c) = _mod_params(c, c_ctx, w_mod, b_mod)
    na_w = NA_HEADS * HEAD_DIM
    n_main = 3 * na_w + 2 * MLA_RANK
    kr_w = w_in[:, n_main:n_main + MLA_ROPE_DIM]
    w_main = jnp.concatenate([w_in[:, :n_main], w_in[:, n_main + MLA_ROPE_DIM:]], axis=1).astype(BF16)
    w_small = jnp.concatenate([kr_w, _rot_cols(kr_w)], axis=1).astype(BF16)
    cos, sin = _rope_cos_sin(s)
    ident = (jnp.ones((l, MLA_ROPE_DIM), F32), jnp.zeros((l, MLA_ROPE_DIM), F32))
    hx, krx = _norm_proj(x, 0, d, norm, w_main, tm=IN_PROJ_ROWS, tn=1024, shift=sh, scale=sc, w_small=w_small,
                         rope_small=_k_tables(cos, sin), small_dtype=BF16, name="in_proj_ab_x")
    ident_flat = tuple(jnp.tile(t, (b, 1)) for t in ident)
    hc, krc = _norm_proj(ctx.reshape(1, b * l, d), 0, d, norm, w_main, tm=IN_PROJ_ROWS, tn=1024, shift=sh_c[:1],
                         scale=sc_c[:1], w_small=w_small, rope_small=_k_tables(*ident_flat), small_dtype=BF16,
                         name="in_proj_ab_ctx")
    hc, krc = hc.reshape(b, l, -1), krc.reshape(b, l, -1)

    wq = w_qb.reshape(MLA_RANK, MLA_HEADS, HEAD_DIM + MLA_ROPE_DIM)
    wq_r = wq[..., HEAD_DIM:]
    wq_p = jnp.concatenate([wq[..., :HEAD_DIM], wq_r, _rot_cols(wq_r)], axis=-1)
    wq_p = wq_p.reshape(MLA_RANK, MLA_HEADS * MLA_QK_PAD).astype(BF16)
    wkv = w_kvb.reshape(MLA_RANK, MLA_HEADS, 2 * HEAD_DIM)
    wkv_p = jnp.concatenate([wkv[..., :HEAD_DIM].reshape(MLA_RANK, -1), wkv[..., HEAD_DIM:].reshape(MLA_RANK, -1)],
                            axis=1).astype(BF16)
    cq_blk = 3 * na_w // MLA_RANK
    nq = MLA_HEADS * MLA_QK_PAD
    qx = _norm_proj(hx, cq_blk, MLA_RANK, q_norm, wq_p, tm=1024, tn=nq, rope_main=_q_tables(cos, sin), name="mla_q_x")
    qc = _norm_proj(hc, cq_blk, MLA_RANK, q_norm, wq_p, tm=1024, tn=nq, rope_main=_q_tables(*ident), name="mla_q_ctx")
    kvx = _norm_proj(hx, cq_blk + 1, MLA_RANK, kv_norm, wkv_p, tm=1024, tn=nq, name="mla_kv_x")
    kvc = _norm_proj(hc, cq_blk + 1, MLA_RANK, kv_norm, wkv_p, tm=1024, tn=nq, name="mla_kv_ctx")

    o_na = _na_attention(hx, hc, _na_bias_table(rpb, s // GRID_W))
    seg_x = (kvx, 0, krx, kvx, MLA_HEADS)
    seg_c = (kvc, 0, krc, kvc, MLA_HEADS)
    o_mla = _attention(qx, 0, MLA_QK_PAD, [seg_x, seg_c], MLA_HEADS, tq=2048, scale=1.0, name="mla_attention")
    z_blk0 = n_main // na_w
    w_out_b = w_out.astype(BF16)
    x_new = _out_proj_ab(x, gt, o_na, o_mla, hx, z_blk0, w_out_b, tm=512)

    o_na_c = _attention(hc, 0, HEAD_DIM, [(hc, NA_HEADS, None, hc, 2 * NA_HEADS)], NA_HEADS, tq=l,
                        scale=HEAD_DIM ** -0.5, name="na_ctx_attention")
    o_mla_c = _attention(qc, 0, MLA_QK_PAD, [seg_c], MLA_HEADS, tq=l, scale=1.0, name="mla_ctx_attention")
    ctx_new = _out_proj_ab(ctx, gt_c, o_na_c, o_mla_c, hc, z_blk0, w_out_b, tm=512)
    return x_new, ctx_new


def _deltanet_final_layer(x, ctx, c, c_ctx, w_mod, b_mod, norm, w_in, conv_w, a_log, dt_bias, o_norm, w_out, final_norm):
    b, s, d = x.shape
    (sh, sc, gt), (sh_c, sc_c, _) = _mod_params(c, c_ctx, w_mod, b_mod)
    qk_w = DN_QK_HEADS * HEAD_DIM
    v_w = DN_V_HEADS * HEAD_DIM
    n_main = 2 * qk_w + 2 * v_w
    lane = np.arange(4 * DN_V_HEADS)
    perm = (lane % 4) * DN_V_HEADS + lane // 4
    w_main = w_in[:, :n_main].astype(BF16)
    w_small = w_in[:, n_main:][:, perm].astype(BF16)
    zeros = jnp.zeros((2 * DN_V_HEADS,), F32)
    alog = jnp.concatenate([zeros, a_log.reshape(-1).astype(F32)])[perm].reshape(1, -1)
    dtb = jnp.concatenate([zeros, dt_bias.reshape(-1).astype(F32)])[perm].reshape(1, -1)
    conv_w = conv_w.astype(F32)

    def inputs(t, shift, scale, name, shared_modulation=False):
        bt, tl, _ = t.shape
        if shared_modulation:
            t, shift, scale = t.reshape(1, bt * tl, d), shift[:1], scale[:1]
        h, hs = _norm_proj(t, 0, d, norm, w_main, tm=IN_PROJ_ROWS, tn=1024, shift=shift, scale=scale, w_small=w_small,
                           name=name)
        h, hs = h.reshape(bt, tl, -1), hs.reshape(bt, tl, -1)
        q = _dn_conv(h, conv_w, 0, qk_w, HEAD_DIM ** -0.5, tt=512)
        k = _dn_conv(h, conv_w, qk_w, qk_w, 1.0, tt=512)
        v = _dn_conv(h, conv_w, 2 * qk_w, v_w, None, tt=512)
        return h, q, k, v, _dn_gates(hs, alog, dtb)

    hx, qx, kx, vx, gx = inputs(x, sh, sc, "in_proj_dn_x")
    _, qc, kc, vc, gc = inputs(ctx, sh_c, sc_c, "in_proj_dn_ctx", shared_modulation=True)
    o = _dn_chunk(qc, kc, vc, gc, qx, kx, vx, gx)
    return _out_proj_dn(x, gt, o, hx, (2 * qk_w + v_w) // v_w, o_norm, w_out.astype(BF16), final_norm, tm=256)


def kernel(x, c, ctx, c_ctx, ab_w_mod, ab_b_mod, ab_norm, ab_w_in, ab_rpb, ab_q_norm, ab_w_qb, ab_kv_norm, ab_w_kvb, ab_w_out, dn_w_mod, dn_b_mod, dn_norm, dn_w_in, dn_conv, dn_a_log, dn_dt_bias, dn_o_norm, dn_w_out, final_norm):
    assert ab_w_mod.shape[0] == 1 and dn_w_mod.shape[0] == 1, "depth-2 trunk: one even and one odd layer"
    x, ctx = _na_mla_layer(x, ctx, c, c_ctx, ab_w_mod[0], ab_b_mod[0], ab_norm[0], ab_w_in[0], ab_rpb[0],
                           ab_q_norm[0], ab_w_qb[0], ab_kv_norm[0], ab_w_kvb[0], ab_w_out[0])
    return _deltanet_final_layer(x, ctx, c, c_ctx, dn_w_mod[0], dn_b_mod[0], dn_norm[0], dn_w_in[0], dn_conv[0],
                                 dn_a_log[0], dn_dt_bias[0], dn_o_norm[0], dn_w_out[0], final_norm)
```
